```python
import jax, jax.numpy as jnp
from jax import lax
import numpy as np

D_MODEL = 1024
BATCH = 2
SEQ = 16384
DEPTH = 2

GRID_W = 64
CTX_LEN = 256
EPS = 1e-6
ROPE_BASE = 10000.0
NEG_INF = -1e30

GLA_HEADS = 4
GLA_HEAD_K = D_MODEL // (2 * GLA_HEADS)
GLA_HEAD_V = D_MODEL // GLA_HEADS
GLA_QK = GLA_HEADS * GLA_HEAD_K
GLA_V = GLA_HEADS * GLA_HEAD_V
GLA_RANK = 16
GLA_GATE_NORM = 16.0
GLA_CHUNK = 64

SWA_HEADS = 8
SWA_KV_HEADS = 2
SWA_GROUP = SWA_HEADS // SWA_KV_HEADS
SWA_HEAD_DIM = D_MODEL // SWA_HEADS
SWA_WINDOW = 128
SWA_BLOCK = 128
SWA_SCALE = SWA_HEAD_DIM ** -0.5

MLA_HEADS = 8
MLA_Q_RANK = 384
MLA_KV_RANK = 256
MLA_NOPE = 128
MLA_ROPE = 64
MLA_V = D_MODEL // MLA_HEADS
MLA_QUERY_BLOCK = 128
MLA_SCALE = (MLA_NOPE + MLA_ROPE) ** -0.5

N_BRANCH = 3
BRANCH_W = D_MODEL
FF_DIM = 4 * D_MODEL

IN_SIZES = (GLA_QK, GLA_QK, GLA_V, GLA_V, GLA_RANK, GLA_RANK,
            SWA_HEADS * SWA_HEAD_DIM, SWA_KV_HEADS * SWA_HEAD_DIM, SWA_KV_HEADS * SWA_HEAD_DIM,
            MLA_Q_RANK, MLA_KV_RANK, MLA_ROPE, N_BRANCH * D_MODEL)
IN_WIDTH = sum(IN_SIZES)
IN_OFFSETS = tuple(int(v) for v in np.cumsum(IN_SIZES)[:-1])

kernel_name = 'hybrid_gla_swa_mla_prefix_dit'


def rms_norm(x, w):
    xf = x.astype(jnp.float32)
    y = xf * lax.rsqrt(jnp.mean(xf * xf, axis=-1, keepdims=True) + EPS)
    return (y * w.astype(jnp.float32)).astype(x.dtype)


def modulate(x, w, shift, scale):
    return rms_norm(x, w) * (1 + scale[..., None, :]) + shift[..., None, :]


def squared_relu_mlp(h, w1, w2):
    return jnp.square(jax.nn.relu(h @ w1)) @ w2


def axial_tables(row, col, dim, dtype):
    d_axis = dim // 2
    inv = ROPE_BASE ** (-jnp.arange(0, d_axis, 2, dtype=jnp.float32) / d_axis)
    ang_r = row.astype(jnp.float32)[:, None] * inv
    ang_c = col.astype(jnp.float32)[:, None] * inv
    return tuple(t[:, None, :].astype(dtype) for t in (jnp.cos(ang_r), jnp.sin(ang_r), jnp.cos(ang_c), jnp.sin(ang_c)))


def rope_rotate(x, cos, sin):
    x1, x2 = jnp.split(x, 2, axis=-1)
    return jnp.concatenate([x1 * cos - x2 * sin, x2 * cos + x1 * sin], axis=-1)


def apply_axial(x, tabs):
    cos_r, sin_r, cos_c, sin_c = tabs
    x_row, x_col = jnp.split(x, 2, axis=-1)
    return jnp.concatenate([rope_rotate(x_row, cos_r, sin_r), rope_rotate(x_col, cos_c, sin_c)], axis=-1)


def to_heads(t, n_heads):
    b, t_len, _ = t.shape
    return t.reshape(b, t_len, n_heads, -1).transpose(0, 2, 1, 3)


def from_heads(t):
    b, h, t_len, d = t.shape
    return t.transpose(0, 2, 1, 3).reshape(b, t_len, h * d)


def gla_log_decay(z, a_w, a_b):
    return jax.nn.log_sigmoid((z @ a_w + a_b).astype(jnp.float32)) / GLA_GATE_NORM


def gla_chunked(q, k, v, log_a, s0):
    b_, h_, t_len, dk = q.shape
    dv = v.shape[-1]
    n = t_len // GLA_CHUNK
    f32 = jnp.float32
    qc = q.astype(f32).reshape(b_, h_, n, GLA_CHUNK, dk)
    kc = k.astype(f32).reshape(b_, h_, n, GLA_CHUNK, dk)
    vc = v.astype(f32).reshape(b_, h_, n, GLA_CHUNK, dv)
    bcum = jnp.cumsum(log_a.astype(f32).reshape(b_, h_, n, GLA_CHUNK, dk), axis=3)
    b_last = bcum[:, :, :, -1:, :]
    q_dec = qc * jnp.exp(bcum)
    k_inv = kc * jnp.exp(-bcum)
    k_end = kc * jnp.exp(b_last - bcum)
    causal = jnp.tril(jnp.ones((GLA_CHUNK, GLA_CHUNK), bool))
    a_intra = jnp.where(causal, jnp.einsum('bhnid,bhnjd->bhnij', q_dec, k_inv), 0.0)
    o_intra = jnp.einsum('bhnij,bhnjv->bhniv', a_intra, vc)
    u = jnp.einsum('bhncd,bhncv->bhndv', k_end, vc)
    g = jnp.exp(b_last[:, :, :, 0, :])

    def step(s, inp):
        g_n, u_n = inp
        return g_n[..., None] * s + u_n, s

    s_final, s_prev = lax.scan(step, s0, (jnp.moveaxis(g, 2, 0), jnp.moveaxis(u, 2, 0)))
    o_inter = jnp.einsum('bhncd,nbhdv->bhncv', q_dec, s_prev)
    o = (o_intra + o_inter).reshape(b_, h_, t_len, dv).astype(v.dtype)
    return o, s_final


def gla_bidirectional(q, k, v, la_f, la_b, s0_f, s0_b):
    o_f, s_f = gla_chunked(q, k, v, la_f, s0_f)
    flip = lambda t: jnp.flip(t, axis=2)
    o_b, s_b = gla_chunked(flip(q), flip(k), flip(v), flip(la_b), s0_b)
    return o_f + flip(o_b), s_f, s_b


def gla_output(o, r, norm_w):
    return from_heads(rms_norm(o, norm_w)) * jax.nn.silu(r)


def swa_latent(q, k, v, k_ctx, v_ctx, sink):
    b_, t_len = q.shape[0], q.shape[1]
    nb = t_len // SWA_BLOCK
    qb = q.reshape(b_, nb, SWA_BLOCK, SWA_KV_HEADS, SWA_GROUP, SWA_HEAD_DIM)

    def band(t):
        pad = jnp.zeros((b_, SWA_BLOCK) + t.shape[2:], t.dtype)
        tp = jnp.concatenate([pad, t, pad], axis=1).reshape((b_, nb + 2, SWA_BLOCK) + t.shape[2:])
        return jnp.concatenate([tp[:, :-2], tp[:, 1:-1], tp[:, 2:]], axis=2)

    kb, vb = band(k), band(v)
    blk = jnp.arange(nb)[:, None, None]
    q_pos = blk * SWA_BLOCK + jnp.arange(SWA_BLOCK)[None, :, None]
    k_pos = (blk - 1) * SWA_BLOCK + jnp.arange(3 * SWA_BLOCK)[None, None, :]
    valid = (jnp.abs(q_pos - k_pos) <= SWA_WINDOW) & (k_pos >= 0) & (k_pos < t_len)
    s_loc = jnp.einsum('bnqkgd,bnskd->bkgnqs', qb, kb, preferred_element_type=jnp.float32) * SWA_SCALE
    s_loc = jnp.where(valid, s_loc, NEG_INF)
    s_ctx = jnp.einsum('bnqkgd,bskd->bkgnqs', qb, k_ctx, preferred_element_type=jnp.float32) * SWA_SCALE
    sink_col = jnp.broadcast_to(sink.astype(jnp.float32).reshape(1, SWA_KV_HEADS, SWA_GROUP, 1, 1, 1),
                                s_loc.shape[:-1] + (1,))
    p = jax.nn.softmax(jnp.concatenate([s_loc, s_ctx, sink_col], axis=-1), axis=-1).astype(v.dtype)
    n_loc = 3 * SWA_BLOCK
    o = (jnp.einsum('bkgnqs,bnskd->bnqkgd', p[..., :n_loc], vb)
         + jnp.einsum('bkgnqs,bskd->bnqkgd', p[..., n_loc:n_loc + k_ctx.shape[1]], v_ctx))
    return o.reshape(b_, t_len, SWA_HEADS * SWA_HEAD_DIM)


def swa_context(q, k, v, sink):
    b_, l_c = q.shape[0], q.shape[1]
    qg = q.reshape(b_, l_c, SWA_KV_HEADS, SWA_GROUP, SWA_HEAD_DIM)
    s = jnp.einsum('bqkgd,bskd->bkgqs', qg, k, preferred_element_type=jnp.float32) * SWA_SCALE
    sink_col = jnp.broadcast_to(sink.astype(jnp.float32).reshape(1, SWA_KV_HEADS, SWA_GROUP, 1, 1),
                                s.shape[:-1] + (1,))
    p = jax.nn.softmax(jnp.concatenate([s, sink_col], axis=-1), axis=-1).astype(v.dtype)
    o = jnp.einsum('bkgqs,bskd->bqkgd', p[..., :l_c], v)
    return o.reshape(b_, l_c, SWA_HEADS * SWA_HEAD_DIM)


def mla_keys(c_kv, k_rope, norm_w, w_ukv, rope):
    b_, t_len, _ = c_kv.shape
    kv = (rms_norm(c_kv, norm_w) @ w_ukv).reshape(b_, t_len, MLA_HEADS, MLA_NOPE + MLA_V)
    k_nope, v = jnp.split(kv, [MLA_NOPE], axis=-1)
    k_pe = k_rope[:, :, None, :]
    if rope is not None:
        k_pe = apply_axial(k_pe, rope)
    k = jnp.concatenate([k_nope, jnp.broadcast_to(k_pe, (b_, t_len, MLA_HEADS, MLA_ROPE))], axis=-1)
    return k, v


def mla_queries(c_q, norm_w, w_uq, rope):
    b_, t_len, _ = c_q.shape
    q = (rms_norm(c_q, norm_w) @ w_uq).reshape(b_, t_len, MLA_HEADS, MLA_NOPE + MLA_ROPE)
    if rope is not None:
        q_nope, q_pe = jnp.split(q, [MLA_NOPE], axis=-1)
        q = jnp.concatenate([q_nope, apply_axial(q_pe, rope)], axis=-1)
    return q


def dense_attention(q, k, v, scale):
    b_, t_len, h_, dq = q.shape
    nb = t_len // MLA_QUERY_BLOCK
    qb = q.reshape(b_, nb, MLA_QUERY_BLOCK, h_, dq).transpose(1, 0, 2, 3, 4)

    def attend_block(q_blk):
        s = jnp.einsum('bqhd,bshd->bhqs', q_blk, k, preferred_element_type=jnp.float32) * scale
        p = jax.nn.softmax(s, axis=-1).astype(v.dtype)
        return jnp.einsum('bhqs,bshv->bqhv', p, v)

    o = lax.map(attend_block, qb)
    return o.transpose(1, 0, 2, 3, 4).reshape(b_, t_len, h_ * v.shape[-1])


def merge_branches(gates, o_gla, o_swa, o_mla, w_branch, w_out):
    g = jax.nn.sigmoid(gates.reshape(gates.shape[:-1] + (N_BRANCH, D_MODEL)))
    merged = (g[..., 0, :] * (o_gla @ w_branch[0])
              + g[..., 1, :] * (o_swa @ w_branch[1])
              + g[..., 2, :] * (o_mla @ w_branch[2]))
    return merged @ w_out


def token_mixers(h, hc, w_in, gla_a_w, gla_a_b, gla_norm_w, swa_sink, mla_q_norm_w, mla_w_uq,
                 mla_kv_norm_w, mla_w_ukv, w_branch, w_out, rope_swa, rope_mla, with_ctx_out):
    b_, t_len, _ = h.shape
    l_c = hc.shape[1]
    (gq, gk, gv, gr, gzf, gzb, sq, sk, sv, mcq, mckv, mkr, gates) = jnp.split(h @ w_in, IN_OFFSETS, axis=-1)
    (cgq, cgk, cgv, cgr, cgzf, cgzb, csq, csk, csv, cmcq, cmckv, cmkr, cgates) = jnp.split(hc @ w_in, IN_OFFSETS, axis=-1)

    def gla_inputs(q, k, v, zf, zb):
        return (to_heads(q * GLA_HEAD_K ** -0.5, GLA_HEADS), to_heads(k, GLA_HEADS), to_heads(v, GLA_HEADS),
                to_heads(gla_log_decay(zf, gla_a_w[0], gla_a_b[0]), GLA_HEADS),
                to_heads(gla_log_decay(zb, gla_a_w[1], gla_a_b[1]), GLA_HEADS))

    s0 = jnp.zeros((b_, GLA_HEADS, GLA_HEAD_K, GLA_HEAD_V), jnp.float32)
    o_gc, s_f, s_b = gla_bidirectional(*gla_inputs(cgq, cgk, cgv, cgzf, cgzb), s0, s0)
    o_g, _, _ = gla_bidirectional(*gla_inputs(gq, gk, gv, gzf, gzb), s_f, s_b)
    o_gla = gla_output(o_g, gr, gla_norm_w)

    q_s = apply_axial(sq.reshape(b_, t_len, SWA_HEADS, SWA_HEAD_DIM), rope_swa)
    k_s = apply_axial(sk.reshape(b_, t_len, SWA_KV_HEADS, SWA_HEAD_DIM), rope_swa)
    v_s = sv.reshape(b_, t_len, SWA_KV_HEADS, SWA_HEAD_DIM)
    k_sc = csk.reshape(b_, l_c, SWA_KV_HEADS, SWA_HEAD_DIM)
    v_sc = csv.reshape(b_, l_c, SWA_KV_HEADS, SWA_HEAD_DIM)
    o_swa = swa_latent(q_s, k_s, v_s, k_sc, v_sc, swa_sink)

    k_m, v_m = mla_keys(mckv, mkr, mla_kv_norm_w, mla_w_ukv, rope_mla)
    k_mc, v_mc = mla_keys(cmckv, cmkr, mla_kv_norm_w, mla_w_ukv, None)
    q_m = mla_queries(mcq, mla_q_norm_w, mla_w_uq, rope_mla)
    o_mla = dense_attention(q_m, jnp.concatenate([k_m, k_mc], axis=1), jnp.concatenate([v_m, v_mc], axis=1), MLA_SCALE)

    y = merge_branches(gates, o_gla, o_swa, o_mla, w_branch, w_out)
    if not with_ctx_out:
        return y, None

    o_gla_c = gla_output(o_gc, cgr, gla_norm_w)
    o_swa_c = swa_context(csq.reshape(b_, l_c, SWA_HEADS, SWA_HEAD_DIM), k_sc, v_sc, swa_sink)
    o_mla_c = dense_attention(mla_queries(cmcq, mla_q_norm_w, mla_w_uq, None), k_mc, v_mc, MLA_SCALE)
    yc = merge_branches(cgates, o_gla_c, o_swa_c, o_mla_c, w_branch, w_out)
    return y, yc


def setup_inputs(seed: int = 0) -> dict:
    key = jax.random.key(seed)
    ks = jax.random.split(key, 22)
    f32 = jnp.float32

    def dense(k, shape, fan_in):
        return jax.random.normal(k, shape, f32) * fan_in ** -0.5

    def gain(k, shape):
        return 1.0 + 0.02 * jax.random.normal(k, shape, f32)

    return {
        'x': jax.random.normal(ks[0], (BATCH, SEQ, D_MODEL), f32),
        'c': jax.random.normal(ks[1], (BATCH, D_MODEL), f32),
        'ctx': jax.random.normal(ks[2], (BATCH, CTX_LEN, D_MODEL), f32),
        'c_ctx': jax.random.normal(ks[3], (D_MODEL,), f32),
        'ada_w': dense(ks[4], (DEPTH, D_MODEL, 6 * D_MODEL), D_MODEL),
        'ada_b': 0.02 * jax.random.normal(ks[5], (DEPTH, 6 * D_MODEL), f32),
        'norm_mix_w': gain(ks[6], (DEPTH, D_MODEL)),
        'w_in': dense(ks[7], (DEPTH, D_MODEL, IN_WIDTH), D_MODEL),
        'gla_a_w': dense(ks[8], (DEPTH, 2, GLA_RANK, GLA_QK), GLA_RANK),
        'gla_a_b': 0.1 * jax.random.normal(ks[9], (DEPTH, 2, GLA_QK), f32),
        'gla_norm_w': gain(ks[10], (DEPTH, GLA_HEAD_V)),
        'swa_sink': 0.5 * jax.random.normal(ks[11], (DEPTH, SWA_HEADS), f32),
        'mla_q_norm_w': gain(ks[12], (DEPTH, MLA_Q_RANK)),
        'mla_w_uq': dense(ks[13], (DEPTH, MLA_Q_RANK, MLA_HEADS * (MLA_NOPE + MLA_ROPE)), MLA_Q_RANK),
        'mla_kv_norm_w': gain(ks[14], (DEPTH, MLA_KV_RANK)),
        'mla_w_ukv': dense(ks[15], (DEPTH, MLA_KV_RANK, MLA_HEADS * (MLA_NOPE + MLA_V)), MLA_KV_RANK),
        'w_branch': dense(ks[16], (DEPTH, N_BRANCH, BRANCH_W, D_MODEL), BRANCH_W),
        'w_out': dense(ks[17], (DEPTH, D_MODEL, D_MODEL), D_MODEL),
        'norm_mlp_w': gain(ks[18], (DEPTH, D_MODEL)),
        'mlp_w1': dense(ks[19], (DEPTH, D_MODEL, FF_DIM), D_MODEL),
        'mlp_w2': dense(ks[20], (DEPTH, FF_DIM, D_MODEL), FF_DIM),
        'final_norm_w': gain(ks[21], (D_MODEL,)),
    }


def reference(x, c, ctx, c_ctx, ada_w, ada_b, norm_mix_w, w_in, gla_a_w, gla_a_b, gla_norm_w, swa_sink,
              mla_q_norm_w, mla_w_uq, mla_kv_norm_w, mla_w_ukv, w_branch, w_out, norm_mlp_w, mlp_w1, mlp_w2,
              final_norm_w):
    n_tok = x.shape[1]
    rows = n_tok // GRID_W
    row = jnp.repeat(jnp.arange(rows), GRID_W)
    col = jnp.tile(jnp.arange(GRID_W), rows)
    rope_swa = axial_tables(row, col, SWA_HEAD_DIM, x.dtype)
    rope_mla = axial_tables(row, col, MLA_ROPE, x.dtype)
    xc = ctx
    for layer in range(DEPTH):
        with_ctx_out = layer < DEPTH - 1
        mod = jax.nn.silu(c) @ ada_w[layer] + ada_b[layer]
        mod_c = jax.nn.silu(c_ctx) @ ada_w[layer] + ada_b[layer]
        sh_m, sc_m, g_m, sh_f, sc_f, g_f = jnp.split(mod, 6, axis=-1)
        csh_m, csc_m, cg_m, csh_f, csc_f, cg_f = jnp.split(mod_c, 6, axis=-1)
        y, yc = token_mixers(modulate(x, norm_mix_w[layer], sh_m, sc_m),
                             modulate(xc, norm_mix_w[layer], csh_m, csc_m),
                             w_in[layer], gla_a_w[layer], gla_a_b[layer], gla_norm_w[layer], swa_sink[layer],
                             mla_q_norm_w[layer], mla_w_uq[layer], mla_kv_norm_w[layer], mla_w_ukv[layer],
                             w_branch[layer], w_out[layer], rope_swa, rope_mla, with_ctx_out)
        x = x + g_m[:, None, :] * y
        x = x + g_f[:, None, :] * squared_relu_mlp(modulate(x, norm_mlp_w[layer], sh_f, sc_f),
                                                   mlp_w1[layer], mlp_w2[layer])
        if with_ctx_out:
            xc = xc + cg_m * yc
            xc = xc + cg_f * squared_relu_mlp(modulate(xc, norm_mlp_w[layer], csh_f, csc_f),
                                              mlp_w1[layer], mlp_w2[layer])
    return rms_norm(x, final_norm_w)
```

```python
import functools

import jax
import jax.numpy as jnp
import numpy as np
from jax import lax
from jax.experimental import pallas as pl
from jax.experimental.pallas import tpu as pltpu

F32 = jnp.float32
BF16 = jnp.bfloat16

D_MODEL = 1024
EPS = 1e-6
ROPE_BASE = 10000.0
NEG_INF = -1e30
GRID_W = 64

GLA_HEADS = 4
GLA_HEAD_K = 128
GLA_HEAD_V = 256
GLA_RANK = 16
GLA_GATE_NORM = 16.0
GLA_CHUNK = 64
GLA_BLOCK = 256

SWA_HEADS = 8
SWA_KV_HEADS = 2
SWA_GROUP = 4
SWA_HEAD_DIM = 128
SWA_WINDOW = 128
SWA_BLOCK = 128
SWA_SCALE = SWA_HEAD_DIM ** -0.5

MLA_HEADS = 8
MLA_Q_RANK = 384
MLA_KV_RANK = 256
MLA_NOPE = 128
MLA_ROPE = 64
MLA_V = 128
MLA_SCALE = (MLA_NOPE + MLA_ROPE) ** -0.5

FF_DIM = 4 * D_MODEL
FF_CHUNK = 1024

LANE = 128
MOD_ROWS = 8
VMEM_LIMIT = 56 * 1024 * 1024

PA_GATES, PA_GR, PA_GV, PA_GQ, PA_GK = 0, 3072, 4096, 5120, 5632
PA_MCQ, PA_SV, PA_MCKV, PA_Z = 6144, 6656, 6912, 7168
PA_WIDTH = 7296
PA_BN = 2432
MCQ_PAD = 512
PR_SQ, PR_SK, PR_KPE = 0, 1024, 1280
PR_WIDTH = 1408
PR_SLABS = PR_WIDTH // LANE


def _params(sem):
    return pltpu.CompilerParams(dimension_semantics=sem, vmem_limit_bytes=VMEM_LIMIT)


def _row_tile(s):
    for tr in (640, 512, 256, 128):
        if s % tr == 0:
            return tr
    raise ValueError(f"unsupported stream length {s}")


def _rms(x, w):
    return x * lax.rsqrt(jnp.mean(x * x, axis=-1, keepdims=True) + EPS) * w


def _mod_rows(mod_ref, b, n_batch, row0, rows, t_lat, idx):
    lo = idx * D_MODEL
    lat = mod_ref[pl.ds(b, 1), lo:lo + D_MODEL]
    ctx = mod_ref[n_batch:n_batch + 1, lo:lo + D_MODEL]
    rid = row0 + lax.broadcasted_iota(jnp.int32, (rows, 1), 0)
    return jnp.where(rid >= t_lat, ctx, lat)


def _swap_halves(x, half):
    lane = lax.broadcasted_iota(jnp.int32, x.shape, 1)
    return jnp.where((lane & half) == 0, pltpu.roll(x, LANE - half, 1), pltpu.roll(x, half, 1))


def _log_sigmoid(x):
    return jnp.minimum(x, 0.0) - jnp.log1p(jnp.exp(-jnp.abs(x)))


def _ada_kernel(c_ref, w_ref, b_ref, o_ref):
    c = c_ref[...]
    a = (c * jax.nn.sigmoid(c)).astype(BF16)
    o_ref[0] = jnp.dot(a, w_ref[0].astype(BF16), preferred_element_type=F32) + b_ref[0]


def _ada_call(cin, ada_w, ada_b):
    depth = ada_w.shape[0]
    nblk = ada_w.shape[2] // D_MODEL
    return pl.pallas_call(
        _ada_kernel,
        grid=(depth, nblk),
        in_specs=[
            pl.BlockSpec((MOD_ROWS, D_MODEL), lambda l, j: (0, 0)),
            pl.BlockSpec((1, D_MODEL, D_MODEL), lambda l, j: (l, 0, j)),
            pl.BlockSpec((1, 1, D_MODEL), lambda l, j: (l, 0, j)),
        ],
        out_specs=pl.BlockSpec((1, MOD_ROWS, D_MODEL), lambda l, j: (l, 0, j)),
        out_shape=jax.ShapeDtypeStruct((depth, MOD_ROWS, ada_w.shape[2]), F32),
        compiler_params=_params(("parallel", "parallel")),
        name="ada_mod",
    )(cin, ada_w, ada_b.reshape(depth, 1, -1))


def _proj_kernel(x_ref, mod_ref, nw_ref, w_ref, *rest, n_batch, t_lat, tr, rope):
    o_ref = rest[-1]
    b = pl.program_id(1)
    row0 = pl.program_id(2) * tr
    x = x_ref[0]
    shift = _mod_rows(mod_ref, b, n_batch, row0, tr, t_lat, 0)
    scale = _mod_rows(mod_ref, b, n_batch, row0, tr, t_lat, 1)
    h = _rms(x, nw_ref[...]) * (1.0 + scale) + shift
    acc = jnp.dot(h.astype(BF16), w_ref[...], preferred_element_type=F32)
    if not rope:
        o_ref[0] = acc.astype(BF16)
        return
    cs_ref, sn_ref, cm_ref, sm_ref = rest[:4]
    cos_s, sin_s = cs_ref[...], sn_ref[...]
    for s in range(PR_SLABS):
        a = acc[:, s * LANE:(s + 1) * LANE]
        if s * LANE < PR_KPE:
            r = a * cos_s + _swap_halves(a, SWA_HEAD_DIM // 4) * sin_s
            if s * LANE < PR_SK:
                r = r * SWA_SCALE
        else:
            r = a * cm_ref[...] + _swap_halves(a, MLA_ROPE // 4) * sm_ref[...]
        o_ref[0, :, s * LANE:(s + 1) * LANE] = r.astype(BF16)


def _proj_call(xa, mod, nw, w, tabs, *, t_lat, bn, rope, name):
    n_batch, s_len, _ = xa.shape
    tr = _row_tile(s_len)
    width = w.shape[1]
    grid = (width // bn, n_batch, s_len // tr)
    in_specs = [
        pl.BlockSpec((1, tr, D_MODEL), lambda j, b, i: (b, i, 0)),
        pl.BlockSpec(mod.shape, lambda j, b, i: (0, 0)),
        pl.BlockSpec((1, D_MODEL), lambda j, b, i: (0, 0)),
        pl.BlockSpec((D_MODEL, bn), lambda j, b, i: (0, j)),
    ]
    args = [xa, mod, nw, w]
    if rope:
        in_specs += [pl.BlockSpec((tr, LANE), lambda j, b, i: (i, 0))] * 4
        args += list(tabs)
    return pl.pallas_call(
        functools.partial(_proj_kernel, n_batch=n_batch, t_lat=t_lat, tr=tr, rope=rope),
        grid=grid,
        in_specs=in_specs,
        out_specs=pl.BlockSpec((1, tr, bn), lambda j, b, i: (b, i, j)),
        out_shape=jax.ShapeDtypeStruct((n_batch, s_len, width), BF16),
        compiler_params=_params(("parallel", "parallel", "parallel")),
        name=name,
    )(*args)


def _gla_kernel(q_ref, k_ref, v_ref, z_ref, aw_ref, ab_ref, o_ref, st_ref, *, reverse):
    n = pl.program_id(2)

    @pl.when(n == 0)
    def _():
        st_ref[...] = jnp.zeros_like(st_ref)

    gb, ch = GLA_BLOCK, GLA_CHUNK
    la = jnp.dot(z_ref[0], aw_ref[0], preferred_element_type=F32) + ab_ref[0]
    la = _log_sigmoid(la) * (1.0 / GLA_GATE_NORM)

    r = lax.broadcasted_iota(jnp.int32, (gb, gb), 0)
    c = lax.broadcasted_iota(jnp.int32, (gb, gb), 1)
    same = (r // ch) == (c // ch)
    tri = (c >= r) if reverse else (c <= r)
    tmat = jnp.where(same & tri, 1.0, 0.0).astype(BF16)
    hi = la.astype(BF16)
    lo = (la - hi.astype(F32)).astype(BF16)
    res = jnp.dot(tmat, jnp.concatenate([hi, lo], axis=1), preferred_element_type=F32)
    bcum = res[:, :GLA_HEAD_K] + res[:, GLA_HEAD_K:]

    q = q_ref[0].astype(F32)
    k = k_ref[0].astype(F32)
    q_dec = (q * (GLA_HEAD_K ** -0.5) * jnp.exp(bcum)).astype(BF16)
    k_inv = (k * jnp.exp(-bcum)).astype(BF16)

    ri = lax.broadcasted_iota(jnp.int32, (ch, ch), 0)
    ci = lax.broadcasted_iota(jnp.int32, (ch, ch), 1)
    amask = (ci >= ri) if reverse else (ci <= ri)

    st = st_ref[...]
    n_ch = gb // ch
    order = range(n_ch - 1, -1, -1) if reverse else range(n_ch)
    for cidx in order:
        lo_r, hi_r = cidx * ch, (cidx + 1) * ch
        last = lo_r if reverse else hi_r - 1
        tot = bcum[last:last + 1, :]
        k_end = (k[lo_r:hi_r] * jnp.exp(tot - bcum[lo_r:hi_r])).astype(BF16)
        qd = q_dec[lo_r:hi_r]
        vc = v_ref[0, lo_r:hi_r, :]
        a = lax.dot_general(qd, k_inv[lo_r:hi_r], (((1,), (1,)), ((), ())), preferred_element_type=F32)
        a = jnp.where(amask, a, 0.0).astype(BF16)
        o = jnp.dot(a, vc, preferred_element_type=F32)
        o += lax.dot_general(qd, st.astype(BF16), (((1,), (1,)), ((), ())), preferred_element_type=F32)
        o_ref[0, lo_r:hi_r, :] = o.astype(BF16)
        v_t = vc.astype(F32).T.astype(BF16)
        st = st * jnp.exp(tot) + jnp.dot(v_t, k_end, preferred_element_type=F32)
    st_ref[...] = st


def _gla_call(pa, aw, ab, *, t_lat, reverse, name):
    n_batch, s_len, _ = pa.shape
    gb = GLA_BLOCK
    n_lat, n_ctx = t_lat // gb, (s_len - t_lat) // gb
    nblk = n_lat + n_ctx

    def blk(n):
        if reverse:
            return jnp.where(n < n_ctx, n_lat + n_ctx - 1 - n, n_lat - 1 - (n - n_ctx))
        return jnp.where(n < n_ctx, n_lat + n, n - n_ctx)

    return pl.pallas_call(
        functools.partial(_gla_kernel, reverse=reverse),
        grid=(n_batch, GLA_HEADS, nblk),
        in_specs=[
            pl.BlockSpec((1, gb, GLA_HEAD_K), lambda b, h, n: (b, blk(n), PA_GQ // GLA_HEAD_K + h)),
            pl.BlockSpec((1, gb, GLA_HEAD_K), lambda b, h, n: (b, blk(n), PA_GK // GLA_HEAD_K + h)),
            pl.BlockSpec((1, gb, GLA_HEAD_V), lambda b, h, n: (b, blk(n), PA_GV // GLA_HEAD_V + h)),
            pl.BlockSpec((1, gb, LANE), lambda b, h, n: (b, blk(n), PA_Z // LANE)),
            pl.BlockSpec((1, LANE, GLA_HEAD_K), lambda b, h, n: (h, 0, 0)),
            pl.BlockSpec((1, 1, GLA_HEAD_K), lambda b, h, n: (h, 0, 0)),
        ],
        out_specs=pl.BlockSpec((1, gb, GLA_HEAD_V), lambda b, h, n: (b, blk(n), h)),
        out_shape=jax.ShapeDtypeStruct((n_batch, s_len, GLA_HEADS * GLA_HEAD_V), BF16),
        scratch_shapes=[pltpu.VMEM((GLA_HEAD_V, GLA_HEAD_K), F32)],
        compiler_params=_params(("parallel", "parallel", "arbitrary")),
        name=name,
    )(pa, pa, pa, pa, aw, ab)


def _swa_kernel(q_ref, kp_ref, kc_ref, kn_ref, vp_ref, vc_ref, vn_ref, kx_ref, vx_ref, sink_ref, o_ref,
                *, t_lat):
    n = pl.program_id(2)
    blk = SWA_BLOCK
    rows = SWA_GROUP * blk
    qg = jnp.concatenate([q_ref[0, :, g * LANE:(g + 1) * LANE] for g in range(SWA_GROUP)], axis=0)
    kb = jnp.concatenate([kp_ref[0], kc_ref[0], kn_ref[0]], axis=0)
    vb = jnp.concatenate([vp_ref[0], vc_ref[0], vn_ref[0]], axis=0)
    nt = (((1,), (1,)), ((), ()))
    s_loc = lax.dot_general(qg, kb, nt, preferred_element_type=F32)
    s_ctx = lax.dot_general(qg, kx_ref[0], nt, preferred_element_type=F32)
    q_pos = n * blk + (lax.broadcasted_iota(jnp.int32, (rows, 3 * blk), 0) & (blk - 1))
    k_pos = (n - 1) * blk + lax.broadcasted_iota(jnp.int32, (rows, 3 * blk), 1)
    valid = (jnp.abs(q_pos - k_pos) <= SWA_WINDOW) & (k_pos >= 0) & (k_pos < t_lat) & (q_pos < t_lat)
    s_loc = jnp.where(valid, s_loc, NEG_INF)
    sink = sink_ref[0]
    m = jnp.maximum(jnp.maximum(jnp.max(s_loc, axis=1, keepdims=True),
                                jnp.max(s_ctx, axis=1, keepdims=True)), sink)
    p_loc = jnp.exp(s_loc - m)
    p_ctx = jnp.exp(s_ctx - m)
    den = (jnp.sum(p_loc, axis=1, keepdims=True) + jnp.sum(p_ctx, axis=1, keepdims=True)
           + jnp.exp(sink - m))
    o = jnp.dot(p_loc.astype(BF16), vb, preferred_element_type=F32)
    o += jnp.dot(p_ctx.astype(BF16), vx_ref[0], preferred_element_type=F32)
    o = o / den
    for g in range(SWA_GROUP):
        o_ref[0, :, g * LANE:(g + 1) * LANE] = o[g * blk:(g + 1) * blk].astype(BF16)


def _swa_call(pr, pa, sink_col, *, t_lat):
    n_batch, s_len, _ = pr.shape
    blk = SWA_BLOCK
    l_ctx = s_len - t_lat
    n_lat = t_lat // blk
    n_all = s_len // blk
    ctx_blk = t_lat // l_ctx
    gw = SWA_GROUP * LANE
    kcol = lambda kh: PR_SK // LANE + kh
    vcol = lambda kh: PA_SV // LANE + kh
    prev = lambda n: jnp.clip(n - 1, 0, n_lat - 1)
    cur = lambda n: jnp.minimum(n, n_lat - 1)
    nxt = lambda n: jnp.minimum(n + 1, n_lat - 1)
    band = lambda col, pos: pl.BlockSpec((1, blk, LANE), lambda b, kh, n: (b, pos(n), col(kh)))
    return pl.pallas_call(
        functools.partial(_swa_kernel, t_lat=t_lat),
        grid=(n_batch, SWA_KV_HEADS, n_all),
        in_specs=[
            pl.BlockSpec((1, blk, gw), lambda b, kh, n: (b, n, PR_SQ // gw + kh)),
            band(kcol, prev), band(kcol, cur), band(kcol, nxt),
            band(vcol, prev), band(vcol, cur), band(vcol, nxt),
            pl.BlockSpec((1, l_ctx, LANE), lambda b, kh, n: (b, ctx_blk, kcol(kh))),
            pl.BlockSpec((1, l_ctx, LANE), lambda b, kh, n: (b, ctx_blk, vcol(kh))),
            pl.BlockSpec((1, SWA_GROUP * blk, 1), lambda b, kh, n: (kh, 0, 0)),
        ],
        out_specs=pl.BlockSpec((1, blk, gw), lambda b, kh, n: (b, n, kh)),
        out_shape=jax.ShapeDtypeStruct((n_batch, s_len, SWA_HEADS * SWA_HEAD_DIM), BF16),
        compiler_params=_params(("parallel", "parallel", "parallel")),
        name="swa",
    )(pr, pr, pr, pr, pa, pa, pa, pr, pa, sink_col)


def _mla_proj_kernel(cq_ref, ckv_ref, qnw_ref, kvnw_ref, wq_ref, wkv_ref, cm_ref, sm_ref,
                     q_ref, kn_ref, v_ref):
    cq = cq_ref[0].astype(F32)
    ms = jnp.sum(cq * cq, axis=-1, keepdims=True) * (1.0 / MLA_Q_RANK)
    cqn = (cq * lax.rsqrt(ms + EPS) * qnw_ref[...]).astype(BF16)
    q = jnp.dot(cqn, wq_ref[...], preferred_element_type=F32)
    cos_m, sin_m = cm_ref[...], sm_ref[...]
    hw = 2 * LANE
    for h in range(MLA_HEADS):
        q_ref[0, :, h * hw:h * hw + LANE] = (q[:, h * hw:h * hw + LANE] * MLA_SCALE).astype(BF16)
        pe = q[:, h * hw + LANE:(h + 1) * hw]
        pe = (pe * cos_m + _swap_halves(pe, MLA_ROPE // 4) * sin_m) * MLA_SCALE
        q_ref[0, :, h * hw + LANE:(h + 1) * hw] = pe.astype(BF16)
    ckv = _rms(ckv_ref[0].astype(F32), kvnw_ref[...]).astype(BF16)
    kv = jnp.dot(ckv, wkv_ref[...], preferred_element_type=F32)
    half = MLA_HEADS * MLA_NOPE
    kn_ref[0] = kv[:, :half].astype(BF16)
    v_ref[0] = kv[:, half:].astype(BF16)


def _mla_proj_call(pa, qnw, kvnw, wq, wkv, cos_m, sin_m):
    n_batch, s_len, _ = pa.shape
    tr = _row_tile(s_len)
    qw = MLA_HEADS * 2 * LANE
    kw = MLA_HEADS * MLA_NOPE
    const = lambda shape: pl.BlockSpec(shape, lambda b, i: (0,) * len(shape))
    return pl.pallas_call(
        _mla_proj_kernel,
        grid=(n_batch, s_len // tr),
        in_specs=[
            pl.BlockSpec((1, tr, MCQ_PAD), lambda b, i: (b, i, PA_MCQ // MCQ_PAD)),
            pl.BlockSpec((1, tr, MLA_KV_RANK), lambda b, i: (b, i, PA_MCKV // MLA_KV_RANK)),
            const((1, MCQ_PAD)), const((1, MLA_KV_RANK)),
            const((MCQ_PAD, qw)), const((MLA_KV_RANK, 2 * kw)),
            pl.BlockSpec((tr, LANE), lambda b, i: (i, 0)),
            pl.BlockSpec((tr, LANE), lambda b, i: (i, 0)),
        ],
        out_specs=[
            pl.BlockSpec((1, tr, qw), lambda b, i: (b, i, 0)),
            pl.BlockSpec((1, tr, kw), lambda b, i: (b, i, 0)),
            pl.BlockSpec((1, tr, kw), lambda b, i: (b, i, 0)),
        ],
        out_shape=[
            jax.ShapeDtypeStruct((n_batch, s_len, qw), BF16),
            jax.ShapeDtypeStruct((n_batch, s_len, kw), BF16),
            jax.ShapeDtypeStruct((n_batch, s_len, kw), BF16),
        ],
        compiler_params=_params(("parallel", "parallel")),
        name="mla_proj",
    )(pa, pa, qnw, kvnw, wq, wkv, cos_m, sin_m)


def _flash_kernel(*refs, aliased):
    if aliased:
        refs = refs[1:]
    q_ref, kn_ref, kpe_ref, v_ref, o_ref, m_ref, l_ref, acc_ref = refs
    j = pl.program_id(3)

    @pl.when(j == 0)
    def _():
        m_ref[...] = jnp.full_like(m_ref, NEG_INF)
        l_ref[...] = jnp.zeros_like(l_ref)
        acc_ref[...] = jnp.zeros_like(acc_ref)

    k = jnp.concatenate([kn_ref[0], kpe_ref[0]], axis=1)
    s = lax.dot_general(q_ref[0], k, (((1,), (1,)), ((), ())), preferred_element_type=F32)
    m_prev = m_ref[...]
    m_new = jnp.maximum(m_prev, jnp.max(s, axis=1, keepdims=True))
    alpha = jnp.exp(m_prev - m_new)
    p = jnp.exp(s - m_new)
    l_ref[...] = alpha * l_ref[...] + jnp.sum(p, axis=1, keepdims=True)
    acc_ref[...] = alpha * acc_ref[...] + jnp.dot(p.astype(BF16), v_ref[0], preferred_element_type=F32)
    m_ref[...] = m_new

    @pl.when(j == pl.num_programs(3) - 1)
    def _():
        o_ref[0] = (acc_ref[...] / l_ref[...]).astype(BF16)


def _flash_call(q, kn, pr, v, prev_out, *, bq, bk, q_blk0, n_q, kv_blk0, n_kv, name):
    n_batch, s_len, _ = q.shape
    aliased = prev_out is not None
    in_specs = [
        pl.BlockSpec((1, bq, 2 * LANE), lambda b, h, i, j: (b, q_blk0 + i, h)),
        pl.BlockSpec((1, bk, LANE), lambda b, h, i, j: (b, kv_blk0 + j, h)),
        pl.BlockSpec((1, bk, LANE), lambda b, h, i, j: (b, kv_blk0 + j, PR_KPE // LANE)),
        pl.BlockSpec((1, bk, LANE), lambda b, h, i, j: (b, kv_blk0 + j, h)),
    ]
    args = [q, kn, pr, v]
    if aliased:
        in_specs = [pl.BlockSpec(memory_space=pl.ANY)] + in_specs
        args = [prev_out] + args
    return pl.pallas_call(
        functools.partial(_flash_kernel, aliased=aliased),
        grid=(n_batch, MLA_HEADS, n_q, n_kv),
        in_specs=in_specs,
        out_specs=pl.BlockSpec((1, bq, LANE), lambda b, h, i, j: (b, q_blk0 + i, h)),
        out_shape=jax.ShapeDtypeStruct((n_batch, s_len, MLA_HEADS * MLA_V), BF16),
        scratch_shapes=[pltpu.VMEM((bq, 1), F32), pltpu.VMEM((bq, 1), F32), pltpu.VMEM((bq, LANE), F32)],
        input_output_aliases={0: 0} if aliased else {},
        compiler_params=_params(("parallel", "parallel", "parallel", "arbitrary")),
        name=name,
    )(*args)


def _kv_block(s_len):
    for bk in (1280, 640, 256, 128):
        if s_len % bk == 0:
            return bk
    raise ValueError(f"unsupported stream length {s_len}")


def _merge_kernel(x_ref, mod_ref, g_ref, gr_ref, of_ref, ob_ref, osw_ref, om_ref, gw_ref, wb_ref, wo_ref,
                  o_ref, *, n_batch, t_lat, tr):
    b = pl.program_id(0)
    row0 = pl.program_id(1) * tr
    og = of_ref[0].astype(F32) + ob_ref[0].astype(F32)
    gw = gw_ref[...]
    heads = []
    for h in range(GLA_HEADS):
        heads.append(_rms(og[:, h * GLA_HEAD_V:(h + 1) * GLA_HEAD_V], gw))
    gr = gr_ref[0].astype(F32)
    o_gla = (jnp.concatenate(heads, axis=1) * (gr * jax.nn.sigmoid(gr))).astype(BF16)
    branches = (o_gla, osw_ref[0], om_ref[0])
    merged = None
    for idx, ob in enumerate(branches):
        gate = jax.nn.sigmoid(g_ref[0, :, idx * D_MODEL:(idx + 1) * D_MODEL].astype(F32))
        term = gate * jnp.dot(ob, wb_ref[idx], preferred_element_type=F32)
        merged = term if merged is None else merged + term
    y = jnp.dot(merged.astype(BF16), wo_ref[...], preferred_element_type=F32)
    g_m = _mod_rows(mod_ref, b, n_batch, row0, tr, t_lat, 2)
    o_ref[0] = x_ref[0] + g_m * y


def _merge_call(xa, mod, pa, o_f, o_b, o_swa, o_mla, gw, wb, wo, *, t_lat):
    n_batch, s_len, _ = xa.shape
    tr = 320 if s_len % 320 == 0 else _row_tile(s_len)
    row = lambda width, col: pl.BlockSpec((1, tr, width), lambda b, i: (b, i, col))
    return pl.pallas_call(
        functools.partial(_merge_kernel, n_batch=n_batch, t_lat=t_lat, tr=tr),
        grid=(n_batch, s_len // tr),
        in_specs=[
            row(D_MODEL, 0),
            pl.BlockSpec(mod.shape, lambda b, i: (0, 0)),
            row(3 * D_MODEL, PA_GATES // (3 * D_MODEL)),
            row(D_MODEL, PA_GR // D_MODEL),
            row(D_MODEL, 0), row(D_MODEL, 0), row(D_MODEL, 0), row(D_MODEL, 0),
            pl.BlockSpec((1, GLA_HEAD_V), lambda b, i: (0, 0)),
            pl.BlockSpec((3, D_MODEL, D_MODEL), lambda b, i: (0, 0, 0)),
            pl.BlockSpec((D_MODEL, D_MODEL), lambda b, i: (0, 0)),
        ],
        out_specs=row(D_MODEL, 0),
        out_shape=jax.ShapeDtypeStruct(xa.shape, F32),
        compiler_params=_params(("parallel", "parallel")),
        name="merge",
    )(xa, mod, pa, pa, o_f, o_b, o_swa, o_mla, gw, wb, wo)


def _mlp_kernel(x_ref, mod_ref, nw_ref, w1_ref, w2_ref, fw_ref, o_ref, h_ref, acc_ref,
                *, n_batch, t_lat, tr, final):
    b = pl.program_id(0)
    row0 = pl.program_id(1) * tr
    c = pl.program_id(2)

    @pl.when(c == 0)
    def _():
        shift = _mod_rows(mod_ref, b, n_batch, row0, tr, t_lat, 3)
        scale = _mod_rows(mod_ref, b, n_batch, row0, tr, t_lat, 4)
        h_ref[...] = (_rms(x_ref[0], nw_ref[...]) * (1.0 + scale) + shift).astype(BF16)
        acc_ref[...] = jnp.zeros_like(acc_ref)

    u = jnp.maximum(jnp.dot(h_ref[...], w1_ref[...], preferred_element_type=F32), 0.0)
    acc_ref[...] += jnp.dot((u * u).astype(BF16), w2_ref[...], preferred_element_type=F32)

    @pl.when(c == pl.num_programs(2) - 1)
    def _():
        g_f = _mod_rows(mod_ref, b, n_batch, row0, tr, t_lat, 5)
        y = x_ref[0] + g_f * acc_ref[...]
        if final:
            y = _rms(y, fw_ref[...])
        o_ref[0] = y


def _mlp_call(xa, mod, nw, w1, w2, fw, *, t_lat, final):
    n_batch, s_len, _ = xa.shape
    tr = _row_tile(s_len)
    return pl.pallas_call(
        functools.partial(_mlp_kernel, n_batch=n_batch, t_lat=t_lat, tr=tr, final=final),
        grid=(n_batch, s_len // tr, FF_DIM // FF_CHUNK),
        in_specs=[
            pl.BlockSpec((1, tr, D_MODEL), lambda b, i, c: (b, i, 0)),
            pl.BlockSpec(mod.shape, lambda b, i, c: (0, 0)),
            pl.BlockSpec((1, D_MODEL), lambda b, i, c: (0, 0)),
            pl.BlockSpec((D_MODEL, FF_CHUNK), lambda b, i, c: (0, c)),
            pl.BlockSpec((FF_CHUNK, D_MODEL), lambda b, i, c: (c, 0)),
            pl.BlockSpec((1, D_MODEL), lambda b, i, c: (0, 0)),
        ],
        out_specs=pl.BlockSpec((1, tr, D_MODEL), lambda b, i, c: (b, i, 0)),
        out_shape=jax.ShapeDtypeStruct(xa.shape, F32),
        scratch_shapes=[pltpu.VMEM((tr, D_MODEL), BF16), pltpu.VMEM((tr, D_MODEL), F32)],
        compiler_params=_params(("parallel", "parallel", "arbitrary")),
        name="mlp",
    )(xa, mod, nw, w1, w2, fw)


def _rope_tables(t_lat, l_ctx, dim):
    pos = jnp.arange(t_lat)
    row, col = pos // GRID_W, pos % GRID_W
    d_axis = dim // 2
    inv = ROPE_BASE ** (-jnp.arange(0, d_axis, 2, dtype=F32) / d_axis)
    ang_r = row.astype(F32)[:, None] * inv
    ang_c = col.astype(F32)[:, None] * inv
    cos = jnp.concatenate([jnp.cos(ang_r)] * 2 + [jnp.cos(ang_c)] * 2, axis=1)
    sin = jnp.concatenate([-jnp.sin(ang_r), jnp.sin(ang_r), -jnp.sin(ang_c), jnp.sin(ang_c)], axis=1)
    if dim < LANE:
        cos = jnp.concatenate([cos, jnp.ones((t_lat, LANE - dim), F32)], axis=1)
        sin = jnp.concatenate([sin, jnp.zeros((t_lat, LANE - dim), F32)], axis=1)
    cos = jnp.concatenate([cos, jnp.ones((l_ctx, LANE), F32)], axis=0)
    sin = jnp.concatenate([sin, jnp.zeros((l_ctx, LANE), F32)], axis=0)
    return cos, sin


def _split_w_in(w):
    sizes = (512, 512, 1024, 1024, GLA_RANK, GLA_RANK, 1024, 256, 256, MLA_Q_RANK, MLA_KV_RANK, MLA_ROPE,
             3 * D_MODEL)
    offs = np.cumsum((0,) + sizes)
    return [w[:, offs[i]:offs[i + 1]] for i in range(len(sizes))]


def _layer_weights(w_in, gla_a_w, gla_a_b, mla_q_norm_w, mla_w_uq, mla_kv_norm_w, mla_w_ukv, swa_sink):
    gq, gk, gv, gr, gzf, gzb, sq, sk, sv, mcq, mckv, mkr, gates = _split_w_in(w_in)
    zpad = lambda n: jnp.zeros((D_MODEL, n), F32)
    w_plain = jnp.concatenate(
        [gates, gr, gv, gq, gk, mcq, zpad(MCQ_PAD - MLA_Q_RANK), sv, mckv, gzf, gzb, zpad(LANE - 2 * GLA_RANK)],
        axis=1).astype(BF16)
    w_rope = jnp.concatenate([sq, sk, mkr, zpad(LANE - MLA_ROPE)], axis=1).astype(BF16)
    aw = jnp.zeros((2, LANE, GLA_HEADS * GLA_HEAD_K), F32)
    aw = aw.at[0, :GLA_RANK].set(gla_a_w[0]).at[1, GLA_RANK:2 * GLA_RANK].set(gla_a_w[1])
    aw = aw.reshape(2, LANE, GLA_HEADS, GLA_HEAD_K).transpose(0, 2, 1, 3).astype(BF16)
    ab = gla_a_b.reshape(2, GLA_HEADS, 1, GLA_HEAD_K)
    wq = mla_w_uq.reshape(MLA_Q_RANK, MLA_HEADS, MLA_NOPE + MLA_ROPE)
    wq = jnp.pad(wq, ((0, MCQ_PAD - MLA_Q_RANK), (0, 0), (0, 2 * LANE - MLA_NOPE - MLA_ROPE)))
    wq = wq.reshape(MCQ_PAD, MLA_HEADS * 2 * LANE).astype(BF16)
    qnw = jnp.pad(mla_q_norm_w, (0, MCQ_PAD - MLA_Q_RANK)).reshape(1, MCQ_PAD)
    wkv = mla_w_ukv.reshape(MLA_KV_RANK, MLA_HEADS, MLA_NOPE + MLA_V)
    wkv = jnp.concatenate([wkv[:, :, :MLA_NOPE].reshape(MLA_KV_RANK, -1),
                           wkv[:, :, MLA_NOPE:].reshape(MLA_KV_RANK, -1)], axis=1).astype(BF16)
    kvnw = mla_kv_norm_w.reshape(1, MLA_KV_RANK)
    sink_col = jnp.repeat(swa_sink.reshape(SWA_KV_HEADS, SWA_GROUP), SWA_BLOCK, axis=1)[..., None]
    return w_plain, w_rope, aw, ab, wq, qnw, wkv, kvnw, sink_col


def kernel(x, c, ctx, c_ctx, ada_w, ada_b, norm_mix_w, w_in, gla_a_w, gla_a_b, gla_norm_w, swa_sink,
           mla_q_norm_w, mla_w_uq, mla_kv_norm_w, mla_w_ukv, w_branch, w_out, norm_mlp_w, mlp_w1, mlp_w2,
           final_norm_w):
    n_batch, t_lat, d_model = x.shape
    l_ctx = ctx.shape[1]
    depth = ada_w.shape[0]
    s_len = t_lat + l_ctx
    assert d_model == D_MODEL and n_batch + 1 <= MOD_ROWS
    assert t_lat % GLA_BLOCK == 0 and l_ctx % GLA_BLOCK == 0 and t_lat % l_ctx == 0 and t_lat % GRID_W == 0

    xa = jnp.concatenate([x, ctx], axis=1)
    cin = jnp.zeros((MOD_ROWS, D_MODEL), F32).at[:n_batch].set(c).at[n_batch].set(c_ctx)
    mod_all = _ada_call(cin, ada_w, ada_b)

    cos_s, sin_s = _rope_tables(t_lat, l_ctx, SWA_HEAD_DIM)
    cos_m, sin_m = _rope_tables(t_lat, l_ctx, MLA_ROPE)
    tabs = (cos_s, sin_s, cos_m, sin_m)

    bq = min(1024, t_lat)
    bk = _kv_block(s_len)
    row_vec = lambda v: v.reshape(1, -1)

    for layer in range(depth):
        mod = mod_all[layer]
        (w_plain, w_rope, aw, ab, wq, qnw, wkv, kvnw, sink_col) = _layer_weights(
            w_in[layer], gla_a_w[layer], gla_a_b[layer], mla_q_norm_w[layer], mla_w_uq[layer],
            mla_kv_norm_w[layer], mla_w_ukv[layer], swa_sink[layer])
        nw = row_vec(norm_mix_w[layer])
        pa = _proj_call(xa, mod, nw, w_plain, None, t_lat=t_lat, bn=PA_BN, rope=False, name="proj_plain")
        pr = _proj_call(xa, mod, nw, w_rope, tabs, t_lat=t_lat, bn=PR_WIDTH, rope=True, name="proj_rope")

        o_f = _gla_call(pa, aw[0], ab[0], t_lat=t_lat, reverse=False, name="gla_fwd")
        o_b = _gla_call(pa, aw[1], ab[1], t_lat=t_lat, reverse=True, name="gla_bwd")
        o_swa = _swa_call(pr, pa, sink_col, t_lat=t_lat)

        q, kn, v = _mla_proj_call(pa, qnw, kvnw, wq, wkv, cos_m, sin_m)
        o_mla = _flash_call(q, kn, pr, v, None, bq=bq, bk=bk, q_blk0=0, n_q=t_lat // bq, kv_blk0=0,
                            n_kv=s_len // bk, name="mla_flash")
        o_mla = _flash_call(q, kn, pr, v, o_mla, bq=l_ctx, bk=l_ctx, q_blk0=t_lat // l_ctx, n_q=1,
                            kv_blk0=t_lat // l_ctx, n_kv=1, name="mla_flash_ctx")

        xa = _merge_call(xa, mod, pa, o_f, o_b, o_swa, o_mla, row_vec(gla_norm_w[layer]),
                         w_branch[layer].astype(BF16), w_out[layer].astype(BF16), t_lat=t_lat)
        xa = _mlp_call(xa, mod, row_vec(norm_mlp_w[layer]), mlp_w1[layer].astype(BF16),
                       mlp_w2[layer].astype(BF16), row_vec(final_norm_w), t_lat=t_lat,
                       final=layer == depth - 1)
    return xa[:, :t_lat]
```

```python
import functools

import jax
import jax.numpy as jnp
import numpy as np
from jax import lax
from jax.experimental import pallas as pl
from jax.experimental.pallas import tpu as pltpu

F32 = jnp.float32
BF16 = jnp.bfloat16

D_MODEL = 1024
EPS = 1e-6
ROPE_BASE = 10000.0
NEG_INF = -1e30
GRID_W = 64

GLA_HEADS = 4
GLA_HEAD_K = 128
GLA_HEAD_V = 256
GLA_RANK = 16
GLA_GATE_NORM = 16.0
GLA_CHUNK = 64
GLA_BLOCK = 256

SWA_HEADS = 8
SWA_KV_HEADS = 2
SWA_GROUP = 4
SWA_HEAD_DIM = 128
SWA_WINDOW = 128
SWA_BLOCK = 128
SWA_SCALE = SWA_HEAD_DIM ** -0.5

MLA_HEADS = 8
MLA_Q_RANK = 384
MLA_KV_RANK = 256
MLA_NOPE = 128
MLA_ROPE = 64
MLA_V = 128
MLA_SCALE = (MLA_NOPE + MLA_ROPE) ** -0.5

FF_DIM = 4 * D_MODEL
FF_CHUNK = 1024
FLASH_SUB = 256

LANE = 128
MOD_ROWS = 8
VMEM_LIMIT = 56 * 1024 * 1024

PA_GATES, PA_GR, PA_GV, PA_GQ, PA_GK = 0, 3072, 4096, 5120, 5632
PA_MCQ, PA_SV, PA_MCKV, PA_Z = 6144, 6656, 6912, 7168
PA_WIDTH = 7296
PA_BN = 2432
MCQ_PAD = 512
PR_SQ, PR_SK, PR_KPE = 0, 1024, 1280
PR_WIDTH = 1408
PR_SLABS = PR_WIDTH // LANE


def _params(sem):
    return pltpu.CompilerParams(dimension_semantics=sem, vmem_limit_bytes=VMEM_LIMIT)


def _row_tile(s):
    for tr in (640, 512, 256, 128):
        if s % tr == 0:
            return tr
    raise ValueError(f"unsupported stream length {s}")


def _rms(x, w):
    return x * lax.rsqrt(jnp.mean(x * x, axis=-1, keepdims=True) + EPS) * w


def _mod_rows(mod_ref, b, n_batch, row0, rows, t_lat, idx):
    lo = idx * D_MODEL
    lat = mod_ref[pl.ds(b, 1), lo:lo + D_MODEL]
    ctx = mod_ref[n_batch:n_batch + 1, lo:lo + D_MODEL]
    rid = row0 + lax.broadcasted_iota(jnp.int32, (rows, 1), 0)
    return jnp.where(rid >= t_lat, ctx, lat)


def _swap_halves(x, half):
    lane = lax.broadcasted_iota(jnp.int32, x.shape, 1)
    return jnp.where((lane & half) == 0, pltpu.roll(x, LANE - half, 1), pltpu.roll(x, half, 1))


def _log_sigmoid(x):
    return jnp.minimum(x, 0.0) - jnp.log1p(jnp.exp(-jnp.abs(x)))


def _ada_kernel(c_ref, w_ref, b_ref, o_ref):
    c = c_ref[...]
    a = (c * jax.nn.sigmoid(c)).astype(BF16)
    o_ref[0] = jnp.dot(a, w_ref[0].astype(BF16), preferred_element_type=F32) + b_ref[0]


def _ada_call(cin, ada_w, ada_b):
    depth = ada_w.shape[0]
    nblk = ada_w.shape[2] // D_MODEL
    return pl.pallas_call(
        _ada_kernel,
        grid=(depth, nblk),
        in_specs=[
            pl.BlockSpec((MOD_ROWS, D_MODEL), lambda l, j: (0, 0)),
            pl.BlockSpec((1, D_MODEL, D_MODEL), lambda l, j: (l, 0, j)),
            pl.BlockSpec((1, 1, D_MODEL), lambda l, j: (l, 0, j)),
        ],
        out_specs=pl.BlockSpec((1, MOD_ROWS, D_MODEL), lambda l, j: (l, 0, j)),
        out_shape=jax.ShapeDtypeStruct((depth, MOD_ROWS, ada_w.shape[2]), F32),
        compiler_params=_params(("parallel", "parallel")),
        name="ada_mod",
    )(cin, ada_w, ada_b.reshape(depth, 1, -1))


def _proj_kernel(x_ref, mod_ref, nw_ref, w_ref, *rest, n_batch, t_lat, tr, rope):
    o_ref = rest[-1]
    b = pl.program_id(1)
    row0 = pl.program_id(2) * tr
    x = x_ref[0]
    shift = _mod_rows(mod_ref, b, n_batch, row0, tr, t_lat, 0)
    scale = _mod_rows(mod_ref, b, n_batch, row0, tr, t_lat, 1)
    h = _rms(x, nw_ref[...]) * (1.0 + scale) + shift
    acc = jnp.dot(h.astype(BF16), w_ref[...], preferred_element_type=F32)
    if not rope:
        o_ref[0] = acc.astype(BF16)
        return
    cs_ref, sn_ref, cm_ref, sm_ref = rest[:4]
    cos_s, sin_s = cs_ref[...], sn_ref[...]
    for s in range(PR_SLABS):
        a = acc[:, s * LANE:(s + 1) * LANE]
        if s * LANE < PR_KPE:
            r = a * cos_s + _swap_halves(a, SWA_HEAD_DIM // 4) * sin_s
            if s * LANE < PR_SK:
                r = r * SWA_SCALE
        else:
            r = a * cm_ref[...] + _swap_halves(a, MLA_ROPE // 4) * sm_ref[...]
        o_ref[0, :, s * LANE:(s + 1) * LANE] = r.astype(BF16)


def _proj_call(xa, mod, nw, w, tabs, *, t_lat, bn, rope, name):
    n_batch, s_len, _ = xa.shape
    tr = _row_tile(s_len)
    width = w.shape[1]
    grid = (width // bn, n_batch, s_len // tr)
    in_specs = [
        pl.BlockSpec((1, tr, D_MODEL), lambda j, b, i: (b, i, 0)),
        pl.BlockSpec(mod.shape, lambda j, b, i: (0, 0)),
        pl.BlockSpec((1, D_MODEL), lambda j, b, i: (0, 0)),
        pl.BlockSpec((D_MODEL, bn), lambda j, b, i: (0, j)),
    ]
    args = [xa, mod, nw, w]
    if rope:
        in_specs += [pl.BlockSpec((tr, LANE), lambda j, b, i: (i, 0))] * 4
        args += list(tabs)
    return pl.pallas_call(
        functools.partial(_proj_kernel, n_batch=n_batch, t_lat=t_lat, tr=tr, rope=rope),
        grid=grid,
        in_specs=in_specs,
        out_specs=pl.BlockSpec((1, tr, bn), lambda j, b, i: (b, i, j)),
        out_shape=jax.ShapeDtypeStruct((n_batch, s_len, width), BF16),
        compiler_params=_params(("parallel", "parallel", "parallel")),
        name=name,
    )(*args)


def _gla_kernel(q_ref, k_ref, v_ref, z_ref, aw_ref, ab_ref, o_ref, st_ref, *, reverse):
    n = pl.program_id(2)

    @pl.when(n == 0)
    def _():
        st_ref[...] = jnp.zeros_like(st_ref)

    gb, ch = GLA_BLOCK, GLA_CHUNK
    la = jnp.dot(z_ref[0], aw_ref[0], preferred_element_type=F32) + ab_ref[0]
    la = _log_sigmoid(la) * (1.0 / GLA_GATE_NORM)

    r = lax.broadcasted_iota(jnp.int32, (gb, gb), 0)
    c = lax.broadcasted_iota(jnp.int32, (gb, gb), 1)
    same = (r // ch) == (c // ch)
    tri = (c >= r) if reverse else (c <= r)
    tmat = jnp.where(same & tri, 1.0, 0.0).astype(BF16)
    hi = la.astype(BF16)
    lo = (la - hi.astype(F32)).astype(BF16)
    res = jnp.dot(tmat, jnp.concatenate([hi, lo], axis=1), preferred_element_type=F32)
    bcum = res[:, :GLA_HEAD_K] + res[:, GLA_HEAD_K:]

    q = q_ref[0].astype(F32)
    k = k_ref[0].astype(F32)
    q_dec = (q * (GLA_HEAD_K ** -0.5) * jnp.exp(bcum)).astype(BF16)
    k_inv = (k * jnp.exp(-bcum)).astype(BF16)

    ri = lax.broadcasted_iota(jnp.int32, (ch, ch), 0)
    ci = lax.broadcasted_iota(jnp.int32, (ch, ch), 1)
    amask = (ci >= ri) if reverse else (ci <= ri)

    st = st_ref[...]
    n_ch = gb // ch
    order = range(n_ch - 1, -1, -1) if reverse else range(n_ch)
    for cidx in order:
        lo_r, hi_r = cidx * ch, (cidx + 1) * ch
        last = lo_r if reverse else hi_r - 1
        tot = bcum[last:last + 1, :]
        k_end = (k[lo_r:hi_r] * jnp.exp(tot - bcum[lo_r:hi_r])).astype(BF16)
        qd = q_dec[lo_r:hi_r]
        vc = v_ref[0, lo_r:hi_r, :]
        a = lax.dot_general(qd, k_inv[lo_r:hi_r], (((1,), (1,)), ((), ())), preferred_element_type=F32)
        a = jnp.where(amask, a, 0.0).astype(BF16)
        o = jnp.dot(a, vc, preferred_element_type=F32)
        o += lax.dot_general(qd, st.astype(BF16), (((1,), (1,)), ((), ())), preferred_element_type=F32)
        o_ref[0, lo_r:hi_r, :] = o.astype(BF16)
        v_t = vc.astype(F32).T.astype(BF16)
        st = st * jnp.exp(tot) + jnp.dot(v_t, k_end, preferred_element_type=F32)
    st_ref[...] = st


def _gla_call(pa, aw, ab, *, t_lat, reverse, name):
    n_batch, s_len, _ = pa.shape
    gb = GLA_BLOCK
    n_lat, n_ctx = t_lat // gb, (s_len - t_lat) // gb
    nblk = n_lat + n_ctx

    def blk(n):
        if reverse:
            return jnp.where(n < n_ctx, n_lat + n_ctx - 1 - n, n_lat - 1 - (n - n_ctx))
        return jnp.where(n < n_ctx, n_lat + n, n - n_ctx)

    return pl.pallas_call(
        functools.partial(_gla_kernel, reverse=reverse),
        grid=(n_batch, GLA_HEADS, nblk),
        in_specs=[
            pl.BlockSpec((1, gb, GLA_HEAD_K), lambda b, h, n: (b, blk(n), PA_GQ // GLA_HEAD_K + h)),
            pl.BlockSpec((1, gb, GLA_HEAD_K), lambda b, h, n: (b, blk(n), PA_GK // GLA_HEAD_K + h)),
            pl.BlockSpec((1, gb, GLA_HEAD_V), lambda b, h, n: (b, blk(n), PA_GV // GLA_HEAD_V + h)),
            pl.BlockSpec((1, gb, LANE), lambda b, h, n: (b, blk(n), PA_Z // LANE)),
            pl.BlockSpec((1, LANE, GLA_HEAD_K), lambda b, h, n: (h, 0, 0)),
            pl.BlockSpec((1, 1, GLA_HEAD_K), lambda b, h, n: (h, 0, 0)),
        ],
        out_specs=pl.BlockSpec((1, gb, GLA_HEAD_V), lambda b, h, n: (b, blk(n), h)),
        out_shape=jax.ShapeDtypeStruct((n_batch, s_len, GLA_HEADS * GLA_HEAD_V), BF16),
        scratch_shapes=[pltpu.VMEM((GLA_HEAD_V, GLA_HEAD_K), F32)],
        compiler_params=_params(("parallel", "parallel", "arbitrary")),
        name=name,
    )(pa, pa, pa, pa, aw, ab)


def _swa_kernel(q_ref, kp_ref, kc_ref, kn_ref, vp_ref, vc_ref, vn_ref, kx_ref, vx_ref, sink_ref, o_ref,
                *, t_lat):
    n = pl.program_id(2)
    blk = SWA_BLOCK
    rows = SWA_GROUP * blk
    qg = jnp.concatenate([q_ref[0, :, g * LANE:(g + 1) * LANE] for g in range(SWA_GROUP)], axis=0)
    kb = jnp.concatenate([kp_ref[0], kc_ref[0], kn_ref[0]], axis=0)
    vb = jnp.concatenate([vp_ref[0], vc_ref[0], vn_ref[0]], axis=0)
    nt = (((1,), (1,)), ((), ()))
    s_loc = lax.dot_general(qg, kb, nt, preferred_element_type=F32)
    s_ctx = lax.dot_general(qg, kx_ref[0], nt, preferred_element_type=F32)
    q_pos = n * blk + (lax.broadcasted_iota(jnp.int32, (rows, 3 * blk), 0) & (blk - 1))
    k_pos = (n - 1) * blk + lax.broadcasted_iota(jnp.int32, (rows, 3 * blk), 1)
    valid = (jnp.abs(q_pos - k_pos) <= SWA_WINDOW) & (k_pos >= 0) & (k_pos < t_lat) & (q_pos < t_lat)
    s_loc = jnp.where(valid, s_loc, NEG_INF)
    sink = sink_ref[0]
    m = jnp.maximum(jnp.maximum(jnp.max(s_loc, axis=1, keepdims=True),
                                jnp.max(s_ctx, axis=1, keepdims=True)), sink)
    p_loc = jnp.exp(s_loc - m)
    p_ctx = jnp.exp(s_ctx - m)
    den = (jnp.sum(p_loc, axis=1, keepdims=True) + jnp.sum(p_ctx, axis=1, keepdims=True)
           + jnp.exp(sink - m))
    o = jnp.dot(p_loc.astype(BF16), vb, preferred_element_type=F32)
    o += jnp.dot(p_ctx.astype(BF16), vx_ref[0], preferred_element_type=F32)
    o = o / den
    for g in range(SWA_GROUP):
        o_ref[0, :, g * LANE:(g + 1) * LANE] = o[g * blk:(g + 1) * blk].astype(BF16)


def _swa_call(pr, pa, sink_col, *, t_lat):
    n_batch, s_len, _ = pr.shape
    blk = SWA_BLOCK
    l_ctx = s_len - t_lat
    n_lat = t_lat // blk
    n_all = s_len // blk
    ctx_blk = t_lat // l_ctx
    gw = SWA_GROUP * LANE
    kcol = lambda kh: PR_SK // LANE + kh
    vcol = lambda kh: PA_SV // LANE + kh
    prev = lambda n: jnp.clip(n - 1, 0, n_lat - 1)
    cur = lambda n: jnp.minimum(n, n_lat - 1)
    nxt = lambda n: jnp.minimum(n + 1, n_lat - 1)
    band = lambda col, pos: pl.BlockSpec((1, blk, LANE), lambda b, kh, n: (b, pos(n), col(kh)))
    return pl.pallas_call(
        functools.partial(_swa_kernel, t_lat=t_lat),
        grid=(n_batch, SWA_KV_HEADS, n_all),
        in_specs=[
            pl.BlockSpec((1, blk, gw), lambda b, kh, n: (b, n, PR_SQ // gw + kh)),
            band(kcol, prev), band(kcol, cur), band(kcol, nxt),
            band(vcol, prev), band(vcol, cur), band(vcol, nxt),
            pl.BlockSpec((1, l_ctx, LANE), lambda b, kh, n: (b, ctx_blk, kcol(kh))),
            pl.BlockSpec((1, l_ctx, LANE), lambda b, kh, n: (b, ctx_blk, vcol(kh))),
            pl.BlockSpec((1, SWA_GROUP * blk, 1), lambda b, kh, n: (kh, 0, 0)),
        ],
        out_specs=pl.BlockSpec((1, blk, gw), lambda b, kh, n: (b, n, kh)),
        out_shape=jax.ShapeDtypeStruct((n_batch, s_len, SWA_HEADS * SWA_HEAD_DIM), BF16),
        compiler_params=_params(("parallel", "parallel", "parallel")),
        name="swa",
    )(pr, pr, pr, pr, pa, pa, pa, pr, pa, sink_col)


def _mla_proj_kernel(cq_ref, ckv_ref, qnw_ref, kvnw_ref, wq_ref, wkv_ref, cm_ref, sm_ref,
                     q_ref, kn_ref, v_ref):
    cq = cq_ref[0].astype(F32)
    ms = jnp.sum(cq * cq, axis=-1, keepdims=True) * (1.0 / MLA_Q_RANK)
    cqn = (cq * lax.rsqrt(ms + EPS) * qnw_ref[...]).astype(BF16)
    q = jnp.dot(cqn, wq_ref[...], preferred_element_type=F32)
    cos_m, sin_m = cm_ref[...], sm_ref[...]
    hw = 2 * LANE
    for h in range(MLA_HEADS):
        q_ref[0, :, h * hw:h * hw + LANE] = (q[:, h * hw:h * hw + LANE] * MLA_SCALE).astype(BF16)
        pe = q[:, h * hw + LANE:(h + 1) * hw]
        pe = (pe * cos_m + _swap_halves(pe, MLA_ROPE // 4) * sin_m) * MLA_SCALE
        q_ref[0, :, h * hw + LANE:(h + 1) * hw] = pe.astype(BF16)
    ckv = _rms(ckv_ref[0].astype(F32), kvnw_ref[...]).astype(BF16)
    kv = jnp.dot(ckv, wkv_ref[...], preferred_element_type=F32)
    half = MLA_HEADS * MLA_NOPE
    kn_ref[0] = kv[:, :half].astype(BF16)
    v_ref[0] = kv[:, half:].astype(BF16)


def _mla_proj_call(pa, qnw, kvnw, wq, wkv, cos_m, sin_m):
    n_batch, s_len, _ = pa.shape
    tr = _row_tile(s_len)
    qw = MLA_HEADS * 2 * LANE
    kw = MLA_HEADS * MLA_NOPE
    const = lambda shape: pl.BlockSpec(shape, lambda b, i: (0,) * len(shape))
    return pl.pallas_call(
        _mla_proj_kernel,
        grid=(n_batch, s_len // tr),
        in_specs=[
            pl.BlockSpec((1, tr, MCQ_PAD), lambda b, i: (b, i, PA_MCQ // MCQ_PAD)),
            pl.BlockSpec((1, tr, MLA_KV_RANK), lambda b, i: (b, i, PA_MCKV // MLA_KV_RANK)),
            const((1, MCQ_PAD)), const((1, MLA_KV_RANK)),
            const((MCQ_PAD, qw)), const((MLA_KV_RANK, 2 * kw)),
            pl.BlockSpec((tr, LANE), lambda b, i: (i, 0)),
            pl.BlockSpec((tr, LANE), lambda b, i: (i, 0)),
        ],
        out_specs=[
            pl.BlockSpec((1, tr, qw), lambda b, i: (b, i, 0)),
            pl.BlockSpec((1, tr, kw), lambda b, i: (b, i, 0)),
            pl.BlockSpec((1, tr, kw), lambda b, i: (b, i, 0)),
        ],
        out_shape=[
            jax.ShapeDtypeStruct((n_batch, s_len, qw), BF16),
            jax.ShapeDtypeStruct((n_batch, s_len, kw), BF16),
            jax.ShapeDtypeStruct((n_batch, s_len, kw), BF16),
        ],
        compiler_params=_params(("parallel", "parallel")),
        name="mla_proj",
    )(pa, pa, qnw, kvnw, wq, wkv, cos_m, sin_m)


def _flash_kernel(*refs, aliased, sub):
    if aliased:
        refs = refs[1:]
    q_ref, kn_ref, kpe_ref, v_ref, o_ref, m_ref, l_ref, acc_ref = refs
    j = pl.program_id(3)

    @pl.when(j == 0)
    def _():
        m_ref[...] = jnp.full_like(m_ref, NEG_INF)
        l_ref[...] = jnp.zeros_like(l_ref)
        acc_ref[...] = jnp.zeros_like(acc_ref)

    k = jnp.concatenate([kn_ref[0], kpe_ref[0]], axis=1)
    v = v_ref[0]
    bq = q_ref.shape[1]
    sub = min(sub, bq)
    chains = [slice(c * sub, (c + 1) * sub) for c in range(bq // sub)]
    scores = [lax.dot_general(q_ref[0, rows, :], k, (((1,), (1,)), ((), ())), preferred_element_type=F32)
              for rows in chains]
    probs, alphas = [], []
    for rows, s in zip(chains, scores):
        m_prev = m_ref[rows, :]
        m_new = jnp.maximum(m_prev, jnp.max(s, axis=1, keepdims=True))
        alpha = jnp.exp(m_prev - m_new)
        p = jnp.exp(s - m_new)
        l_ref[rows, :] = alpha * l_ref[rows, :] + jnp.sum(p, axis=1, keepdims=True)
        m_ref[rows, :] = m_new
        probs.append(p.astype(BF16))
        alphas.append(alpha)
    for rows, p, alpha in zip(chains, probs, alphas):
        acc_ref[rows, :] = alpha * acc_ref[rows, :] + jnp.dot(p, v, preferred_element_type=F32)

    @pl.when(j == pl.num_programs(3) - 1)
    def _():
        o_ref[0] = (acc_ref[...] / l_ref[...]).astype(BF16)


def _flash_call(q, kn, pr, v, prev_out, *, bq, bk, q_blk0, n_q, kv_blk0, n_kv, name):
    n_batch, s_len, _ = q.shape
    aliased = prev_out is not None
    in_specs = [
        pl.BlockSpec((1, bq, 2 * LANE), lambda b, h, i, j: (b, q_blk0 + i, h)),
        pl.BlockSpec((1, bk, LANE), lambda b, h, i, j: (b, kv_blk0 + j, h)),
        pl.BlockSpec((1, bk, LANE), lambda b, h, i, j: (b, kv_blk0 + j, PR_KPE // LANE)),
        pl.BlockSpec((1, bk, LANE), lambda b, h, i, j: (b, kv_blk0 + j, h)),
    ]
    args = [q, kn, pr, v]
    if aliased:
        in_specs = [pl.BlockSpec(memory_space=pl.ANY)] + in_specs
        args = [prev_out] + args
    return pl.pallas_call(
        functools.partial(_flash_kernel, aliased=aliased, sub=FLASH_SUB),
        grid=(n_batch, MLA_HEADS, n_q, n_kv),
        in_specs=in_specs,
        out_specs=pl.BlockSpec((1, bq, LANE), lambda b, h, i, j: (b, q_blk0 + i, h)),
        out_shape=jax.ShapeDtypeStruct((n_batch, s_len, MLA_HEADS * MLA_V), BF16),
        scratch_shapes=[pltpu.VMEM((bq, 1), F32), pltpu.VMEM((bq, 1), F32), pltpu.VMEM((bq, LANE), F32)],
        input_output_aliases={0: 0} if aliased else {},
        compiler_params=_params(("parallel", "parallel", "parallel", "arbitrary")),
        name=name,
    )(*args)


def _kv_block(s_len):
    for bk in (1280, 640, 256, 128):
        if s_len % bk == 0:
            return bk
    raise ValueError(f"unsupported stream length {s_len}")


def _merge_kernel(x_ref, mod_ref, g_ref, gr_ref, of_ref, ob_ref, osw_ref, om_ref, gw_ref, wb_ref, wo_ref,
                  o_ref, *, n_batch, t_lat, tr):
    b = pl.program_id(0)
    row0 = pl.program_id(1) * tr
    og = of_ref[0].astype(F32) + ob_ref[0].astype(F32)
    gw = gw_ref[...]
    heads = []
    for h in range(GLA_HEADS):
        heads.append(_rms(og[:, h * GLA_HEAD_V:(h + 1) * GLA_HEAD_V], gw))
    gr = gr_ref[0].astype(F32)
    o_gla = (jnp.concatenate(heads, axis=1) * (gr * jax.nn.sigmoid(gr))).astype(BF16)
    branches = (o_gla, osw_ref[0], om_ref[0])
    merged = None
    for idx, ob in enumerate(branches):
        gate = jax.nn.sigmoid(g_ref[0, :, idx * D_MODEL:(idx + 1) * D_MODEL].astype(F32))
        term = gate * jnp.dot(ob, wb_ref[idx], preferred_element_type=F32)
        merged = term if merged is None else merged + term
    y = jnp.dot(merged.astype(BF16), wo_ref[...], preferred_element_type=F32)
    g_m = _mod_rows(mod_ref, b, n_batch, row0, tr, t_lat, 2)
    o_ref[0] = x_ref[0] + g_m * y


def _merge_call(xa, mod, pa, o_f, o_b, o_swa, o_mla, gw, wb, wo, *, t_lat):
    n_batch, s_len, _ = xa.shape
    tr = 320 if s_len % 320 == 0 else _row_tile(s_len)
    row = lambda width, col: pl.BlockSpec((1, tr, width), lambda b, i: (b, i, col))
    return pl.pallas_call(
        functools.partial(_merge_kernel, n_batch=n_batch, t_lat=t_lat, tr=tr),
        grid=(n_batch, s_len // tr),
        in_specs=[
            row(D_MODEL, 0),
            pl.BlockSpec(mod.shape, lambda b, i: (0, 0)),
            row(3 * D_MODEL, PA_GATES // (3 * D_MODEL)),
            row(D_MODEL, PA_GR // D_MODEL),
            row(D_MODEL, 0), row(D_MODEL, 0), row(D_MODEL, 0), row(D_MODEL, 0),
            pl.BlockSpec((1, GLA_HEAD_V), lambda b, i: (0, 0)),
            pl.BlockSpec((3, D_MODEL, D_MODEL), lambda b, i: (0, 0, 0)),
            pl.BlockSpec((D_MODEL, D_MODEL), lambda b, i: (0, 0)),
        ],
        out_specs=row(D_MODEL, 0),
        out_shape=jax.ShapeDtypeStruct(xa.shape, F32),
        compiler_params=_params(("parallel", "parallel")),
        name="merge",
    )(xa, mod, pa, pa, o_f, o_b, o_swa, o_mla, gw, wb, wo)


def _mlp_kernel(x_ref, mod_ref, nw_ref, w1_ref, w2_ref, fw_ref, o_ref, h_ref, acc_ref,
                *, n_batch, t_lat, tr, final):
    b = pl.program_id(0)
    row0 = pl.program_id(1) * tr
    c = pl.program_id(2)

    @pl.when(c == 0)
    def _():
        shift = _mod_rows(mod_ref, b, n_batch, row0, tr, t_lat, 3)
        scale = _mod_rows(mod_ref, b, n_batch, row0, tr, t_lat, 4)
        h_ref[...] = (_rms(x_ref[0], nw_ref[...]) * (1.0 + scale) + shift).astype(BF16)
        acc_ref[...] = jnp.zeros_like(acc_ref)

    u = jnp.maximum(jnp.dot(h_ref[...], w1_ref[...], preferred_element_type=F32), 0.0)
    acc_ref[...] += jnp.dot((u * u).astype(BF16), w2_ref[...], preferred_element_type=F32)

    @pl.when(c == pl.num_programs(2) - 1)
    def _():
        g_f = _mod_rows(mod_ref, b, n_batch, row0, tr, t_lat, 5)
        y = x_ref[0] + g_f * acc_ref[...]
        if final:
            y = _rms(y, fw_ref[...])
        o_ref[0] = y


def _mlp_call(xa, mod, nw, w1, w2, fw, *, t_lat, final):
    n_batch, s_len, _ = xa.shape
    tr = _row_tile(s_len)
    return pl.pallas_call(
        functools.partial(_mlp_kernel, n_batch=n_batch, t_lat=t_lat, tr=tr, final=final),
        grid=(n_batch, s_len // tr, FF_DIM // FF_CHUNK),
        in_specs=[
            pl.BlockSpec((1, tr, D_MODEL), lambda b, i, c: (b, i, 0)),
            pl.BlockSpec(mod.shape, lambda b, i, c: (0, 0)),
            pl.BlockSpec((1, D_MODEL), lambda b, i, c: (0, 0)),
            pl.BlockSpec((D_MODEL, FF_CHUNK), lambda b, i, c: (0, c)),
            pl.BlockSpec((FF_CHUNK, D_MODEL), lambda b, i, c: (c, 0)),
            pl.BlockSpec((1, D_MODEL), lambda b, i, c: (0, 0)),
        ],
        out_specs=pl.BlockSpec((1, tr, D_MODEL), lambda b, i, c: (b, i, 0)),
        out_shape=jax.ShapeDtypeStruct(xa.shape, F32),
        scratch_shapes=[pltpu.VMEM((tr, D_MODEL), BF16), pltpu.VMEM((tr, D_MODEL), F32)],
        compiler_params=_params(("parallel", "parallel", "arbitrary")),
        name="mlp",
    )(xa, mod, nw, w1, w2, fw)


def _rope_tables(t_lat, l_ctx, dim):
    pos = jnp.arange(t_lat)
    row, col = pos // GRID_W, pos % GRID_W
    d_axis = dim // 2
    inv = ROPE_BASE ** (-jnp.arange(0, d_axis, 2, dtype=F32) / d_axis)
    ang_r = row.astype(F32)[:, None] * inv
    ang_c = col.astype(F32)[:, None] * inv
    cos = jnp.concatenate([jnp.cos(ang_r)] * 2 + [jnp.cos(ang_c)] * 2, axis=1)
    sin = jnp.concatenate([-jnp.sin(ang_r), jnp.sin(ang_r), -jnp.sin(ang_c), jnp.sin(ang_c)], axis=1)
    if dim < LANE:
        cos = jnp.concatenate([cos, jnp.ones((t_lat, LANE - dim), F32)], axis=1)
        sin = jnp.concatenate([sin, jnp.zeros((t_lat, LANE - dim), F32)], axis=1)
    cos = jnp.concatenate([cos, jnp.ones((l_ctx, LANE), F32)], axis=0)
    sin = jnp.concatenate([sin, jnp.zeros((l_ctx, LANE), F32)], axis=0)
    return cos, sin


def _split_w_in(w):
    sizes = (512, 512, 1024, 1024, GLA_RANK, GLA_RANK, 1024, 256, 256, MLA_Q_RANK, MLA_KV_RANK, MLA_ROPE,
             3 * D_MODEL)
    offs = np.cumsum((0,) + sizes)
    return [w[:, offs[i]:offs[i + 1]] for i in range(len(sizes))]


def _layer_weights(w_in, gla_a_w, gla_a_b, mla_q_norm_w, mla_w_uq, mla_kv_norm_w, mla_w_ukv, swa_sink):
    gq, gk, gv, gr, gzf, gzb, sq, sk, sv, mcq, mckv, mkr, gates = _split_w_in(w_in)
    zpad = lambda n: jnp.zeros((D_MODEL, n), F32)
    w_plain = jnp.concatenate(
        [gates, gr, gv, gq, gk, mcq, zpad(MCQ_PAD - MLA_Q_RANK), sv, mckv, gzf, gzb, zpad(LANE - 2 * GLA_RANK)],
        axis=1).astype(BF16)
    w_rope = jnp.concatenate([sq, sk, mkr, zpad(LANE - MLA_ROPE)], axis=1).astype(BF16)
    aw = jnp.zeros((2, LANE, GLA_HEADS * GLA_HEAD_K), F32)
    aw = aw.at[0, :GLA_RANK].set(gla_a_w[0]).at[1, GLA_RANK:2 * GLA_RANK].set(gla_a_w[1])
    aw = aw.reshape(2, LANE, GLA_HEADS, GLA_HEAD_K).transpose(0, 2, 1, 3).astype(BF16)
    ab = gla_a_b.reshape(2, GLA_HEADS, 1, GLA_HEAD_K)
    wq = mla_w_uq.reshape(MLA_Q_RANK, MLA_HEADS, MLA_NOPE + MLA_ROPE)
    wq = jnp.pad(wq, ((0, MCQ_PAD - MLA_Q_RANK), (0, 0), (0, 2 * LANE - MLA_NOPE - MLA_ROPE)))
    wq = wq.reshape(MCQ_PAD, MLA_HEADS * 2 * LANE).astype(BF16)
    qnw = jnp.pad(mla_q_norm_w, (0, MCQ_PAD - MLA_Q_RANK)).reshape(1, MCQ_PAD)
    wkv = mla_w_ukv.reshape(MLA_KV_RANK, MLA_HEADS, MLA_NOPE + MLA_V)
    wkv = jnp.concatenate([wkv[:, :, :MLA_NOPE].reshape(MLA_KV_RANK, -1),
                           wkv[:, :, MLA_NOPE:].reshape(MLA_KV_RANK, -1)], axis=1).astype(BF16)
    kvnw = mla_kv_norm_w.reshape(1, MLA_KV_RANK)
    sink_col = jnp.repeat(swa_sink.reshape(SWA_KV_HEADS, SWA_GROUP), SWA_BLOCK, axis=1)[..., None]
    return w_plain, w_rope, aw, ab, wq, qnw, wkv, kvnw, sink_col


def kernel(x, c, ctx, c_ctx, ada_w, ada_b, norm_mix_w, w_in, gla_a_w, gla_a_b, gla_norm_w, swa_sink,
           mla_q_norm_w, mla_w_uq, mla_kv_norm_w, mla_w_ukv, w_branch, w_out, norm_mlp_w, mlp_w1, mlp_w2,
           final_norm_w):
    n_batch, t_lat, d_model = x.shape
    l_ctx = ctx.shape[1]
    depth = ada_w.shape[0]
    s_len = t_lat + l_ctx
    assert d_model == D_MODEL and n_batch + 1 <= MOD_ROWS
    assert t_lat % GLA_BLOCK == 0 and l_ctx % GLA_BLOCK == 0 and t_lat % l_ctx == 0 and t_lat % GRID_W == 0

    xa = jnp.concatenate([x, ctx], axis=1)
    cin = jnp.zeros((MOD_ROWS, D_MODEL), F32).at[:n_batch].set(c).at[n_batch].set(c_ctx)
    mod_all = _ada_call(cin, ada_w, ada_b)

    cos_s, sin_s = _rope_tables(t_lat, l_ctx, SWA_HEAD_DIM)
    cos_m, sin_m = _rope_tables(t_lat, l_ctx, MLA_ROPE)
    tabs = (cos_s, sin_s, cos_m, sin_m)

    bq = min(1024, t_lat)
    bk = _kv_block(s_len)
    row_vec = lambda v: v.reshape(1, -1)

    for layer in range(depth):
        mod = mod_all[layer]
        (w_plain, w_rope, aw, ab, wq, qnw, wkv, kvnw, sink_col) = _layer_weights(
            w_in[layer], gla_a_w[layer], gla_a_b[layer], mla_q_norm_w[layer], mla_w_uq[layer],
            mla_kv_norm_w[layer], mla_w_ukv[layer], swa_sink[layer])
        nw = row_vec(norm_mix_w[layer])
        pa = _proj_call(xa, mod, nw, w_plain, None, t_lat=t_lat, bn=PA_BN, rope=False, name="proj_plain")
        pr = _proj_call(xa, mod, nw, w_rope, tabs, t_lat=t_lat, bn=PR_WIDTH, rope=True, name="proj_rope")

        o_f = _gla_call(pa, aw[0], ab[0], t_lat=t_lat, reverse=False, name="gla_fwd")
        o_b = _gla_call(pa, aw[1], ab[1], t_lat=t_lat, reverse=True, name="gla_bwd")
        o_swa = _swa_call(pr, pa, sink_col, t_lat=t_lat)

        q, kn, v = _mla_proj_call(pa, qnw, kvnw, wq, wkv, cos_m, sin_m)
        o_mla = _flash_call(q, kn, pr, v, None, bq=bq, bk=bk, q_blk0=0, n_q=t_lat // bq, kv_blk0=0,
                            n_kv=s_len // bk, name="mla_flash")
        o_mla = _flash_call(q, kn, pr, v, o_mla, bq=l_ctx, bk=l_ctx, q_blk0=t_lat // l_ctx, n_q=1,
                            kv_blk0=t_lat // l_ctx, n_kv=1, name="mla_flash_ctx")

        xa = _merge_call(xa, mod, pa, o_f, o_b, o_swa, o_mla, row_vec(gla_norm_w[layer]),
                         w_branch[layer].astype(BF16), w_out[layer].astype(BF16), t_lat=t_lat)
        xa = _mlp_call(xa, mod, row_vec(norm_mlp_w[layer]), mlp_w1[layer].astype(BF16),
                       mlp_w2[layer].astype(BF16), row_vec(final_norm_w), t_lat=t_lat,
                       final=layer == depth - 1)
    return xa[:, :t_lat]
```

```python
import functools

import jax
import jax.numpy as jnp
import numpy as np
from jax import lax
from jax.experimental import pallas as pl
from jax.experimental.pallas import tpu as pltpu

F32 = jnp.float32
BF16 = jnp.bfloat16

D_MODEL = 1024
EPS = 1e-6
ROPE_BASE = 10000.0
NEG_INF = -1e30
LOG2E = 1.4426950408889634
GRID_W = 64

GLA_HEADS = 4
GLA_HEAD_K = 128
GLA_HEAD_V = 256
GLA_RANK = 16
GLA_GATE_NORM = 16.0
GLA_CHUNK = 64
GLA_BLOCK = 256

SWA_HEADS = 8
SWA_KV_HEADS = 2
SWA_GROUP = 4
SWA_HEAD_DIM = 128
SWA_WINDOW = 128
SWA_BLOCK = 128
SWA_SCALE = SWA_HEAD_DIM ** -0.5

MLA_HEADS = 8
MLA_Q_RANK = 384
MLA_KV_RANK = 256
MLA_NOPE = 128
MLA_ROPE = 64
MLA_V = 128
MLA_SCALE = (MLA_NOPE + MLA_ROPE) ** -0.5

FF_DIM = 4 * D_MODEL
FF_CHUNK = 1024
FLASH_SUB = 256

LANE = 128
MOD_ROWS = 8
VMEM_LIMIT = 56 * 1024 * 1024

PA_GATES, PA_GR, PA_GV, PA_GQ, PA_GK = 0, 3072, 4096, 5120, 5632
PA_MCQ, PA_SV, PA_MCKV, PA_Z = 6144, 6656, 6912, 7168
PA_WIDTH = 7296
PA_BN = 2432
MCQ_PAD = 512
PR_SQ, PR_SK, PR_KPE = 0, 1024, 1280
PR_WIDTH = 1408
PR_SLABS = PR_WIDTH // LANE


def _params(sem):
    return pltpu.CompilerParams(dimension_semantics=sem, vmem_limit_bytes=VMEM_LIMIT)


def _row_tile(s):
    for tr in (640, 512, 256, 128):
        if s % tr == 0:
            return tr
    raise ValueError(f"unsupported stream length {s}")


def _rms(x, w):
    return x * lax.rsqrt(jnp.mean(x * x, axis=-1, keepdims=True) + EPS) * w


def _mod_rows(mod_ref, b, n_batch, row0, rows, t_lat, idx):
    lo = idx * D_MODEL
    lat = mod_ref[pl.ds(b, 1), lo:lo + D_MODEL]
    ctx = mod_ref[n_batch:n_batch + 1, lo:lo + D_MODEL]
    rid = row0 + lax.broadcasted_iota(jnp.int32, (rows, 1), 0)
    return jnp.where(rid >= t_lat, ctx, lat)


def _swap_halves(x, half):
    lane = lax.broadcasted_iota(jnp.int32, x.shape, 1)
    return jnp.where((lane & half) == 0, pltpu.roll(x, LANE - half, 1), pltpu.roll(x, half, 1))


def _log_sigmoid(x):
    return jnp.minimum(x, 0.0) - jnp.log1p(jnp.exp(-jnp.abs(x)))


def _ada_kernel(c_ref, w_ref, b_ref, o_ref):
    c = c_ref[...]
    a = (c * jax.nn.sigmoid(c)).astype(BF16)
    o_ref[0] = jnp.dot(a, w_ref[0].astype(BF16), preferred_element_type=F32) + b_ref[0]


def _ada_call(cin, ada_w, ada_b):
    depth = ada_w.shape[0]
    nblk = ada_w.shape[2] // D_MODEL
    return pl.pallas_call(
        _ada_kernel,
        grid=(depth, nblk),
        in_specs=[
            pl.BlockSpec((MOD_ROWS, D_MODEL), lambda l, j: (0, 0)),
            pl.BlockSpec((1, D_MODEL, D_MODEL), lambda l, j: (l, 0, j)),
            pl.BlockSpec((1, 1, D_MODEL), lambda l, j: (l, 0, j)),
        ],
        out_specs=pl.BlockSpec((1, MOD_ROWS, D_MODEL), lambda l, j: (l, 0, j)),
        out_shape=jax.ShapeDtypeStruct((depth, MOD_ROWS, ada_w.shape[2]), F32),
        compiler_params=_params(("parallel", "parallel")),
        name="ada_mod",
    )(cin, ada_w, ada_b.reshape(depth, 1, -1))


def _proj_kernel(x_ref, mod_ref, nw_ref, w_ref, *rest, n_batch, t_lat, tr, rope):
    o_ref = rest[-1]
    b = pl.program_id(1)
    row0 = pl.program_id(2) * tr
    x = x_ref[0]
    shift = _mod_rows(mod_ref, b, n_batch, row0, tr, t_lat, 0)
    scale = _mod_rows(mod_ref, b, n_batch, row0, tr, t_lat, 1)
    h = _rms(x, nw_ref[...]) * (1.0 + scale) + shift
    acc = jnp.dot(h.astype(BF16), w_ref[...], preferred_element_type=F32)
    if not rope:
        o_ref[0] = acc.astype(BF16)
        return
    cs_ref, sn_ref, cm_ref, sm_ref = rest[:4]
    cos_s, sin_s = cs_ref[...], sn_ref[...]
    for s in range(PR_SLABS):
        a = acc[:, s * LANE:(s + 1) * LANE]
        if s * LANE < PR_KPE:
            r = a * cos_s + _swap_halves(a, SWA_HEAD_DIM // 4) * sin_s
            if s * LANE < PR_SK:
                r = r * SWA_SCALE
        else:
            r = a * cm_ref[...] + _swap_halves(a, MLA_ROPE // 4) * sm_ref[...]
        o_ref[0, :, s * LANE:(s + 1) * LANE] = r.astype(BF16)


def _proj_call(xa, mod, nw, w, tabs, *, t_lat, bn, rope, name):
    n_batch, s_len, _ = xa.shape
    tr = _row_tile(s_len)
    width = w.shape[1]
    grid = (width // bn, n_batch, s_len // tr)
    in_specs = [
        pl.BlockSpec((1, tr, D_MODEL), lambda j, b, i: (b, i, 0)),
        pl.BlockSpec(mod.shape, lambda j, b, i: (0, 0)),
        pl.BlockSpec((1, D_MODEL), lambda j, b, i: (0, 0)),
        pl.BlockSpec((D_MODEL, bn), lambda j, b, i: (0, j)),
    ]
    args = [xa, mod, nw, w]
    if rope:
        in_specs += [pl.BlockSpec((tr, LANE), lambda j, b, i: (i, 0))] * 4
        args += list(tabs)
    return pl.pallas_call(
        functools.partial(_proj_kernel, n_batch=n_batch, t_lat=t_lat, tr=tr, rope=rope),
        grid=grid,
        in_specs=in_specs,
        out_specs=pl.BlockSpec((1, tr, bn), lambda j, b, i: (b, i, j)),
        out_shape=jax.ShapeDtypeStruct((n_batch, s_len, width), BF16),
        compiler_params=_params(("parallel", "parallel", "parallel")),
        name=name,
    )(*args)


def _gla_kernel(q_ref, k_ref, v_ref, z_ref, aw_ref, ab_ref, o_ref, st_ref, *, reverse):
    n = pl.program_id(2)

    @pl.when(n == 0)
    def _():
        st_ref[...] = jnp.zeros_like(st_ref)

    gb, ch = GLA_BLOCK, GLA_CHUNK
    la = jnp.dot(z_ref[0], aw_ref[0], preferred_element_type=F32) + ab_ref[0]
    la = _log_sigmoid(la) * (1.0 / GLA_GATE_NORM)

    r = lax.broadcasted_iota(jnp.int32, (gb, gb), 0)
    c = lax.broadcasted_iota(jnp.int32, (gb, gb), 1)
    same = (r // ch) == (c // ch)
    tri = (c >= r) if reverse else (c <= r)
    tmat = jnp.where(same & tri, 1.0, 0.0).astype(BF16)
    hi = la.astype(BF16)
    lo = (la - hi.astype(F32)).astype(BF16)
    res = jnp.dot(tmat, jnp.concatenate([hi, lo], axis=1), preferred_element_type=F32)
    bcum = res[:, :GLA_HEAD_K] + res[:, GLA_HEAD_K:]

    q = q_ref[0].astype(F32)
    k = k_ref[0].astype(F32)
    q_dec = (q * (GLA_HEAD_K ** -0.5) * jnp.exp(bcum)).astype(BF16)
    k_inv = (k * jnp.exp(-bcum)).astype(BF16)

    ri = lax.broadcasted_iota(jnp.int32, (ch, ch), 0)
    ci = lax.broadcasted_iota(jnp.int32, (ch, ch), 1)
    amask = (ci >= ri) if reverse else (ci <= ri)

    st = st_ref[...]
    n_ch = gb // ch
    order = range(n_ch - 1, -1, -1) if reverse else range(n_ch)
    for cidx in order:
        lo_r, hi_r = cidx * ch, (cidx + 1) * ch
        last = lo_r if reverse else hi_r - 1
        tot = bcum[last:last + 1, :]
        k_end = (k[lo_r:hi_r] * jnp.exp(tot - bcum[lo_r:hi_r])).astype(BF16)
        qd = q_dec[lo_r:hi_r]
        vc = v_ref[0, lo_r:hi_r, :]
        a = lax.dot_general(qd, k_inv[lo_r:hi_r], (((1,), (1,)), ((), ())), preferred_element_type=F32)
        a = jnp.where(amask, a, 0.0).astype(BF16)
        o = jnp.dot(a, vc, preferred_element_type=F32)
        o += lax.dot_general(qd, st.astype(BF16), (((1,), (1,)), ((), ())), preferred_element_type=F32)
        o_ref[0, lo_r:hi_r, :] = o.astype(BF16)
        v_t = vc.astype(F32).T.astype(BF16)
        st = st * jnp.exp(tot) + jnp.dot(v_t, k_end, preferred_element_type=F32)
    st_ref[...] = st


def _gla_call(pa, aw, ab, *, t_lat, reverse, name):
    n_batch, s_len, _ = pa.shape
    gb = GLA_BLOCK
    n_lat, n_ctx = t_lat // gb, (s_len - t_lat) // gb
    nblk = n_lat + n_ctx

    def blk(n):
        if reverse:
            return jnp.where(n < n_ctx, n_lat + n_ctx - 1 - n, n_lat - 1 - (n - n_ctx))
        return jnp.where(n < n_ctx, n_lat + n, n - n_ctx)

    return pl.pallas_call(
        functools.partial(_gla_kernel, reverse=reverse),
        grid=(n_batch, GLA_HEADS, nblk),
        in_specs=[
            pl.BlockSpec((1, gb, GLA_HEAD_K), lambda b, h, n: (b, blk(n), PA_GQ // GLA_HEAD_K + h)),
            pl.BlockSpec((1, gb, GLA_HEAD_K), lambda b, h, n: (b, blk(n), PA_GK // GLA_HEAD_K + h)),
            pl.BlockSpec((1, gb, GLA_HEAD_V), lambda b, h, n: (b, blk(n), PA_GV // GLA_HEAD_V + h)),
            pl.BlockSpec((1, gb, LANE), lambda b, h, n: (b, blk(n), PA_Z // LANE)),
            pl.BlockSpec((1, LANE, GLA_HEAD_K), lambda b, h, n: (h, 0, 0)),
            pl.BlockSpec((1, 1, GLA_HEAD_K), lambda b, h, n: (h, 0, 0)),
        ],
        out_specs=pl.BlockSpec((1, gb, GLA_HEAD_V), lambda b, h, n: (b, blk(n), h)),
        out_shape=jax.ShapeDtypeStruct((n_batch, s_len, GLA_HEADS * GLA_HEAD_V), BF16),
        scratch_shapes=[pltpu.VMEM((GLA_HEAD_V, GLA_HEAD_K), F32)],
        compiler_params=_params(("parallel", "parallel", "arbitrary")),
        name=name,
    )(pa, pa, pa, pa, aw, ab)


def _swa_kernel(q_ref, kp_ref, kc_ref, kn_ref, vp_ref, vc_ref, vn_ref, kx_ref, vx_ref, sink_ref, o_ref,
                *, t_lat):
    n = pl.program_id(2)
    blk = SWA_BLOCK
    rows = SWA_GROUP * blk
    qg = jnp.concatenate([q_ref[0, :, g * LANE:(g + 1) * LANE] for g in range(SWA_GROUP)], axis=0)
    kb = jnp.concatenate([kp_ref[0], kc_ref[0], kn_ref[0]], axis=0)
    vb = jnp.concatenate([vp_ref[0], vc_ref[0], vn_ref[0]], axis=0)
    nt = (((1,), (1,)), ((), ()))
    s_loc = lax.dot_general(qg, kb, nt, preferred_element_type=F32)
    s_ctx = lax.dot_general(qg, kx_ref[0], nt, preferred_element_type=F32)
    q_pos = n * blk + (lax.broadcasted_iota(jnp.int32, (rows, 3 * blk), 0) & (blk - 1))
    k_pos = (n - 1) * blk + lax.broadcasted_iota(jnp.int32, (rows, 3 * blk), 1)
    valid = (jnp.abs(q_pos - k_pos) <= SWA_WINDOW) & (k_pos >= 0) & (k_pos < t_lat) & (q_pos < t_lat)
    s_loc = jnp.where(valid, s_loc, NEG_INF)
    sink = sink_ref[0]
    m = jnp.maximum(jnp.maximum(jnp.max(s_loc, axis=1, keepdims=True),
                                jnp.max(s_ctx, axis=1, keepdims=True)), sink)
    p_loc = jnp.exp(s_loc - m)
    p_ctx = jnp.exp(s_ctx - m)
    den = (jnp.sum(p_loc, axis=1, keepdims=True) + jnp.sum(p_ctx, axis=1, keepdims=True)
           + jnp.exp(sink - m))
    o = jnp.dot(p_loc.astype(BF16), vb, preferred_element_type=F32)
    o += jnp.dot(p_ctx.astype(BF16), vx_ref[0], preferred_element_type=F32)
    o = o / den
    for g in range(SWA_GROUP):
        o_ref[0, :, g * LANE:(g + 1) * LANE] = o[g * blk:(g + 1) * blk].astype(BF16)


def _swa_call(pr, pa, sink_col, *, t_lat):
    n_batch, s_len, _ = pr.shape
    blk = SWA_BLOCK
    l_ctx = s_len - t_lat
    n_lat = t_lat // blk
    n_all = s_len // blk
    ctx_blk = t_lat // l_ctx
    gw = SWA_GROUP * LANE
    kcol = lambda kh: PR_SK // LANE + kh
    vcol = lambda kh: PA_SV // LANE + kh
    prev = lambda n: jnp.clip(n - 1, 0, n_lat - 1)
    cur = lambda n: jnp.minimum(n, n_lat - 1)
    nxt = lambda n: jnp.minimum(n + 1, n_lat - 1)
    band = lambda col, pos: pl.BlockSpec((1, blk, LANE), lambda b, kh, n: (b, pos(n), col(kh)))
    return pl.pallas_call(
        functools.partial(_swa_kernel, t_lat=t_lat),
        grid=(n_batch, SWA_KV_HEADS, n_all),
        in_specs=[
            pl.BlockSpec((1, blk, gw), lambda b, kh, n: (b, n, PR_SQ // gw + kh)),
            band(kcol, prev), band(kcol, cur), band(kcol, nxt),
            band(vcol, prev), band(vcol, cur), band(vcol, nxt),
            pl.BlockSpec((1, l_ctx, LANE), lambda b, kh, n: (b, ctx_blk, kcol(kh))),
            pl.BlockSpec((1, l_ctx, LANE), lambda b, kh, n: (b, ctx_blk, vcol(kh))),
            pl.BlockSpec((1, SWA_GROUP * blk, 1), lambda b, kh, n: (kh, 0, 0)),
        ],
        out_specs=pl.BlockSpec((1, blk, gw), lambda b, kh, n: (b, n, kh)),
        out_shape=jax.ShapeDtypeStruct((n_batch, s_len, SWA_HEADS * SWA_HEAD_DIM), BF16),
        compiler_params=_params(("parallel", "parallel", "parallel")),
        name="swa",
    )(pr, pr, pr, pr, pa, pa, pa, pr, pa, sink_col)


def _mla_proj_kernel(cq_ref, ckv_ref, kpe_ref, qnw_ref, kvnw_ref, wq_ref, wkv_ref, cm_ref, sm_ref,
                     q_ref, k_ref, v_ref):
    cq = cq_ref[0].astype(F32)
    ms = jnp.sum(cq * cq, axis=-1, keepdims=True) * (1.0 / MLA_Q_RANK)
    cqn = (cq * lax.rsqrt(ms + EPS) * qnw_ref[...]).astype(BF16)
    q = jnp.dot(cqn, wq_ref[...], preferred_element_type=F32)
    cos_m, sin_m = cm_ref[...], sm_ref[...]
    qs = MLA_SCALE * LOG2E
    hw = 2 * LANE
    for h in range(MLA_HEADS):
        q_ref[0, :, h * hw:h * hw + LANE] = (q[:, h * hw:h * hw + LANE] * qs).astype(BF16)
        pe = q[:, h * hw + LANE:(h + 1) * hw]
        pe = (pe * cos_m + _swap_halves(pe, MLA_ROPE // 4) * sin_m) * qs
        q_ref[0, :, h * hw + LANE:(h + 1) * hw] = pe.astype(BF16)
    ckv = _rms(ckv_ref[0].astype(F32), kvnw_ref[...]).astype(BF16)
    kv = jnp.dot(ckv, wkv_ref[...], preferred_element_type=F32)
    half = MLA_HEADS * MLA_NOPE
    kpe = kpe_ref[0]
    one_hot = (lax.broadcasted_iota(jnp.int32, kpe.shape, 1) == 0).astype(BF16)
    for h in range(MLA_HEADS):
        k_ref[0, :, h * hw:h * hw + LANE] = kv[:, h * LANE:(h + 1) * LANE].astype(BF16)
        k_ref[0, :, h * hw + LANE:(h + 1) * hw] = kpe
        v_ref[0, :, h * hw:h * hw + LANE] = kv[:, half + h * LANE:half + (h + 1) * LANE].astype(BF16)
        v_ref[0, :, h * hw + LANE:(h + 1) * hw] = one_hot


def _mla_proj_call(pa, pr, qnw, kvnw, wq, wkv, cos_m, sin_m):
    n_batch, s_len, _ = pa.shape
    tr = _row_tile(s_len)
    qw = MLA_HEADS * 2 * LANE
    kw = MLA_HEADS * MLA_NOPE
    const = lambda shape: pl.BlockSpec(shape, lambda b, i: (0,) * len(shape))
    wide = pl.BlockSpec((1, tr, qw), lambda b, i: (b, i, 0))
    return pl.pallas_call(
        _mla_proj_kernel,
        grid=(n_batch, s_len // tr),
        in_specs=[
            pl.BlockSpec((1, tr, MCQ_PAD), lambda b, i: (b, i, PA_MCQ // MCQ_PAD)),
            pl.BlockSpec((1, tr, MLA_KV_RANK), lambda b, i: (b, i, PA_MCKV // MLA_KV_RANK)),
            pl.BlockSpec((1, tr, LANE), lambda b, i: (b, i, PR_KPE // LANE)),
            const((1, MCQ_PAD)), const((1, MLA_KV_RANK)),
            const((MCQ_PAD, qw)), const((MLA_KV_RANK, 2 * kw)),
            pl.BlockSpec((tr, LANE), lambda b, i: (i, 0)),
            pl.BlockSpec((tr, LANE), lambda b, i: (i, 0)),
        ],
        out_specs=[wide, wide, wide],
        out_shape=[jax.ShapeDtypeStruct((n_batch, s_len, qw), BF16)] * 3,
        compiler_params=_params(("parallel", "parallel")),
        name="mla_proj",
    )(pa, pa, pr, qnw, kvnw, wq, wkv, cos_m, sin_m)


def _flash_kernel(*refs, aliased, sub):
    if aliased:
        refs = refs[1:]
    q_ref, k_ref, v_ref, o_ref, m_ref, acc_ref = refs
    j = pl.program_id(3)

    @pl.when(j == 0)
    def _():
        m_ref[...] = jnp.full_like(m_ref, NEG_INF)
        acc_ref[...] = jnp.zeros_like(acc_ref)

    k = k_ref[0]
    v = v_ref[0]
    bq = q_ref.shape[1]
    sub = min(sub, bq)
    chains = [slice(c * sub, (c + 1) * sub) for c in range(bq // sub)]
    scores = [lax.dot_general(q_ref[0, rows, :], k, (((1,), (1,)), ((), ())), preferred_element_type=F32)
              for rows in chains]
    probs, alphas = [], []
    for rows, s in zip(chains, scores):
        m_prev = m_ref[rows, :]
        m_new = jnp.maximum(m_prev, jnp.max(s, axis=1, keepdims=True))
        alphas.append(jnp.exp2(m_prev - m_new))
        probs.append(jnp.exp2(s - m_new).astype(BF16))
        m_ref[rows, :] = m_new
    for rows, p, alpha in zip(chains, probs, alphas):
        acc_ref[rows, :] = alpha * acc_ref[rows, :] + jnp.dot(p, v, preferred_element_type=F32)

    @pl.when(j == pl.num_programs(3) - 1)
    def _():
        acc = acc_ref[...]
        o_ref[0] = (acc[:, :MLA_V] / acc[:, MLA_V:MLA_V + 1]).astype(BF16)


def _flash_call(q, k, v, prev_out, *, bq, bk, q_blk0, n_q, kv_blk0, n_kv, name):
    n_batch, s_len, _ = q.shape
    aliased = prev_out is not None
    hw = 2 * LANE
    in_specs = [
        pl.BlockSpec((1, bq, hw), lambda b, h, i, j: (b, q_blk0 + i, h)),
        pl.BlockSpec((1, bk, hw), lambda b, h, i, j: (b, kv_blk0 + j, h)),
        pl.BlockSpec((1, bk, hw), lambda b, h, i, j: (b, kv_blk0 + j, h)),
    ]
    args = [q, k, v]
    if aliased:
        in_specs = [pl.BlockSpec(memory_space=pl.ANY)] + in_specs
        args = [prev_out] + args
    return pl.pallas_call(
        functools.partial(_flash_kernel, aliased=aliased, sub=FLASH_SUB),
        grid=(n_batch, MLA_HEADS, n_q, n_kv),
        in_specs=in_specs,
        out_specs=pl.BlockSpec((1, bq, LANE), lambda b, h, i, j: (b, q_blk0 + i, h)),
        out_shape=jax.ShapeDtypeStruct((n_batch, s_len, MLA_HEADS * MLA_V), BF16),
        scratch_shapes=[pltpu.VMEM((bq, 1), F32), pltpu.VMEM((bq, hw), F32)],
        input_output_aliases={0: 0} if aliased else {},
        compiler_params=_params(("parallel", "parallel", "parallel", "arbitrary")),
        name=name,
    )(*args)


def _kv_block(s_len):
    for bk in (3328, 1280, 640, 256, 128):
        if s_len % bk == 0:
            return bk
    raise ValueError(f"unsupported stream length {s_len}")


def _merge_kernel(x_ref, mod_ref, g_ref, gr_ref, of_ref, ob_ref, osw_ref, om_ref, gw_ref, wb_ref, wo_ref,
                  o_ref, *, n_batch, t_lat, tr):
    b = pl.program_id(0)
    row0 = pl.program_id(1) * tr
    og = of_ref[0].astype(F32) + ob_ref[0].astype(F32)
    gw = gw_ref[...]
    heads = []
    for h in range(GLA_HEADS):
        heads.append(_rms(og[:, h * GLA_HEAD_V:(h + 1) * GLA_HEAD_V], gw))
    gr = gr_ref[0].astype(F32)
    o_gla = (jnp.concatenate(heads, axis=1) * (gr * jax.nn.sigmoid(gr))).astype(BF16)
    branches = (o_gla, osw_ref[0], om_ref[0])
    merged = None
    for idx, ob in enumerate(branches):
        gate = jax.nn.sigmoid(g_ref[0, :, idx * D_MODEL:(idx + 1) * D_MODEL].astype(F32))
        term = gate * jnp.dot(ob, wb_ref[idx], preferred_element_type=F32)
        merged = term if merged is None else merged + term
    y = jnp.dot(merged.astype(BF16), wo_ref[...], preferred_element_type=F32)
    g_m = _mod_rows(mod_ref, b, n_batch, row0, tr, t_lat, 2)
    o_ref[0] = x_ref[0] + g_m * y


def _merge_call(xa, mod, pa, o_f, o_b, o_swa, o_mla, gw, wb, wo, *, t_lat):
    n_batch, s_len, _ = xa.shape
    tr = 320 if s_len % 320 == 0 else _row_tile(s_len)
    row = lambda width, col: pl.BlockSpec((1, tr, width), lambda b, i: (b, i, col))
    return pl.pallas_call(
        functools.partial(_merge_kernel, n_batch=n_batch, t_lat=t_lat, tr=tr),
        grid=(n_batch, s_len // tr),
        in_specs=[
            row(D_MODEL, 0),
            pl.BlockSpec(mod.shape, lambda b, i: (0, 0)),
            row(3 * D_MODEL, PA_GATES // (3 * D_MODEL)),
            row(D_MODEL, PA_GR // D_MODEL),
            row(D_MODEL, 0), row(D_MODEL, 0), row(D_MODEL, 0), row(D_MODEL, 0),
            pl.BlockSpec((1, GLA_HEAD_V), lambda b, i: (0, 0)),
            pl.BlockSpec((3, D_MODEL, D_MODEL), lambda b, i: (0, 0, 0)),
            pl.BlockSpec((D_MODEL, D_MODEL), lambda b, i: (0, 0)),
        ],
        out_specs=row(D_MODEL, 0),
        out_shape=jax.ShapeDtypeStruct(xa.shape, F32),
        compiler_params=_params(("parallel", "parallel")),
        name="merge",
    )(xa, mod, pa, pa, o_f, o_b, o_swa, o_mla, gw, wb, wo)


def _mlp_kernel(x_ref, mod_ref, nw_ref, w1_ref, w2_ref, fw_ref, o_ref, h_ref, acc_ref,
                *, n_batch, t_lat, tr, final):
    b = pl.program_id(0)
    row0 = pl.program_id(1) * tr
    c = pl.program_id(2)

    @pl.when(c == 0)
    def _():
        shift = _mod_rows(mod_ref, b, n_batch, row0, tr, t_lat, 3)
        scale = _mod_rows(mod_ref, b, n_batch, row0, tr, t_lat, 4)
        h_ref[...] = (_rms(x_ref[0], nw_ref[...]) * (1.0 + scale) + shift).astype(BF16)
        acc_ref[...] = jnp.zeros_like(acc_ref)

    u = jnp.maximum(jnp.dot(h_ref[...], w1_ref[...], preferred_element_type=F32), 0.0)
    acc_ref[...] += jnp.dot((u * u).astype(BF16), w2_ref[...], preferred_element_type=F32)

    @pl.when(c == pl.num_programs(2) - 1)
    def _():
        g_f = _mod_rows(mod_ref, b, n_batch, row0, tr, t_lat, 5)
        y = x_ref[0] + g_f * acc_ref[...]
        if final:
            y = _rms(y, fw_ref[...])
        o_ref[0] = y


def _mlp_call(xa, mod, nw, w1, w2, fw, *, t_lat, final):
    n_batch, s_len, _ = xa.shape
    tr = _row_tile(s_len)
    return pl.pallas_call(
        functools.partial(_mlp_kernel, n_batch=n_batch, t_lat=t_lat, tr=tr, final=final),
        grid=(n_batch, s_len // tr, FF_DIM // FF_CHUNK),
        in_specs=[
            pl.BlockSpec((1, tr, D_MODEL), lambda b, i, c: (b, i, 0)),
            pl.BlockSpec(mod.shape, lambda b, i, c: (0, 0)),
            pl.BlockSpec((1, D_MODEL), lambda b, i, c: (0, 0)),
            pl.BlockSpec((D_MODEL, FF_CHUNK), lambda b, i, c: (0, c)),
            pl.BlockSpec((FF_CHUNK, D_MODEL), lambda b, i, c: (c, 0)),
            pl.BlockSpec((1, D_MODEL), lambda b, i, c: (0, 0)),
        ],
        out_specs=pl.BlockSpec((1, tr, D_MODEL), lambda b, i, c: (b, i, 0)),
        out_shape=jax.ShapeDtypeStruct(xa.shape, F32),
        scratch_shapes=[pltpu.VMEM((tr, D_MODEL), BF16), pltpu.VMEM((tr, D_MODEL), F32)],
        compiler_params=_params(("parallel", "parallel", "arbitrary")),
        name="mlp",
    )(xa, mod, nw, w1, w2, fw)


def _rope_tables(t_lat, l_ctx, dim):
    pos = jnp.arange(t_lat)
    row, col = pos // GRID_W, pos % GRID_W
    d_axis = dim // 2
    inv = ROPE_BASE ** (-jnp.arange(0, d_axis, 2, dtype=F32) / d_axis)
    ang_r = row.astype(F32)[:, None] * inv
    ang_c = col.astype(F32)[:, None] * inv
    cos = jnp.concatenate([jnp.cos(ang_r)] * 2 + [jnp.cos(ang_c)] * 2, axis=1)
    sin = jnp.concatenate([-jnp.sin(ang_r), jnp.sin(ang_r), -jnp.sin(ang_c), jnp.sin(ang_c)], axis=1)
    if dim < LANE:
        cos = jnp.concatenate([cos, jnp.ones((t_lat, LANE - dim), F32)], axis=1)
        sin = jnp.concatenate([sin, jnp.zeros((t_lat, LANE - dim), F32)], axis=1)
    cos = jnp.concatenate([cos, jnp.ones((l_ctx, LANE), F32)], axis=0)
    sin = jnp.concatenate([sin, jnp.zeros((l_ctx, LANE), F32)], axis=0)
    return cos, sin


def _split_w_in(w):
    sizes = (512, 512, 1024, 1024, GLA_RANK, GLA_RANK, 1024, 256, 256, MLA_Q_RANK, MLA_KV_RANK, MLA_ROPE,
             3 * D_MODEL)
    offs = np.cumsum((0,) + sizes)
    return [w[:, offs[i]:offs[i + 1]] for i in range(len(sizes))]


def _layer_weights(w_in, gla_a_w, gla_a_b, mla_q_norm_w, mla_w_uq, mla_kv_norm_w, mla_w_ukv, swa_sink):
    gq, gk, gv, gr, gzf, gzb, sq, sk, sv, mcq, mckv, mkr, gates = _split_w_in(w_in)
    zpad = lambda n: jnp.zeros((D_MODEL, n), F32)
    w_plain = jnp.concatenate(
        [gates, gr, gv, gq, gk, mcq, zpad(MCQ_PAD - MLA_Q_RANK), sv, mckv, gzf, gzb, zpad(LANE - 2 * GLA_RANK)],
        axis=1).astype(BF16)
    w_rope = jnp.concatenate([sq, sk, mkr, zpad(LANE - MLA_ROPE)], axis=1).astype(BF16)
    aw = jnp.zeros((2, LANE, GLA_HEADS * GLA_HEAD_K), F32)
    aw = aw.at[0, :GLA_RANK].set(gla_a_w[0]).at[1, GLA_RANK:2 * GLA_RANK].set(gla_a_w[1])
    aw = aw.reshape(2, LANE, GLA_HEADS, GLA_HEAD_K).transpose(0, 2, 1, 3).astype(BF16)
    ab = gla_a_b.reshape(2, GLA_HEADS, 1, GLA_HEAD_K)
    wq = mla_w_uq.reshape(MLA_Q_RANK, MLA_HEADS, MLA_NOPE + MLA_ROPE)
    wq = jnp.pad(wq, ((0, MCQ_PAD - MLA_Q_RANK), (0, 0), (0, 2 * LANE - MLA_NOPE - MLA_ROPE)))
    wq = wq.reshape(MCQ_PAD, MLA_HEADS * 2 * LANE).astype(BF16)
    qnw = jnp.pad(mla_q_norm_w, (0, MCQ_PAD - MLA_Q_RANK)).reshape(1, MCQ_PAD)
    wkv = mla_w_ukv.reshape(MLA_KV_RANK, MLA_HEADS, MLA_NOPE + MLA_V)
    wkv = jnp.concatenate([wkv[:, :, :MLA_NOPE].reshape(MLA_KV_RANK, -1),
                           wkv[:, :, MLA_NOPE:].reshape(MLA_KV_RANK, -1)], axis=1).astype(BF16)
    kvnw = mla_kv_norm_w.reshape(1, MLA_KV_RANK)
    sink_col = jnp.repeat(swa_sink.reshape(SWA_KV_HEADS, SWA_GROUP), SWA_BLOCK, axis=1)[..., None]
    return w_plain, w_rope, aw, ab, wq, qnw, wkv, kvnw, sink_col


def kernel(x, c, ctx, c_ctx, ada_w, ada_b, norm_mix_w, w_in, gla_a_w, gla_a_b, gla_norm_w, swa_sink,
           mla_q_norm_w, mla_w_uq, mla_kv_norm_w, mla_w_ukv, w_branch, w_out, norm_mlp_w, mlp_w1, mlp_w2,
           final_norm_w):
    n_batch, t_lat, d_model = x.shape
    l_ctx = ctx.shape[1]
    depth = ada_w.shape[0]
    s_len = t_lat + l_ctx
    assert d_model == D_MODEL and n_batch + 1 <= MOD_ROWS
    assert t_lat % GLA_BLOCK == 0 and l_ctx % GLA_BLOCK == 0 and t_lat % l_ctx == 0 and t_lat % GRID_W == 0

    xa = jnp.concatenate([x, ctx], axis=1)
    cin = jnp.zeros((MOD_ROWS, D_MODEL), F32).at[:n_batch].set(c).at[n_batch].set(c_ctx)
    mod_all = _ada_call(cin, ada_w, ada_b)

    cos_s, sin_s = _rope_tables(t_lat, l_ctx, SWA_HEAD_DIM)
    cos_m, sin_m = _rope_tables(t_lat, l_ctx, MLA_ROPE)
    tabs = (cos_s, sin_s, cos_m, sin_m)

    bq = min(1024, t_lat)
    bk = _kv_block(s_len)
    row_vec = lambda v: v.reshape(1, -1)

    for layer in range(depth):
        mod = mod_all[layer]
        (w_plain, w_rope, aw, ab, wq, qnw, wkv, kvnw, sink_col) = _layer_weights(
            w_in[layer], gla_a_w[layer], gla_a_b[layer], mla_q_norm_w[layer], mla_w_uq[layer],
            mla_kv_norm_w[layer], mla_w_ukv[layer], swa_sink[layer])
        nw = row_vec(norm_mix_w[layer])
        pa = _proj_call(xa, mod, nw, w_plain, None, t_lat=t_lat, bn=PA_BN, rope=False, name="proj_plain")
        pr = _proj_call(xa, mod, nw, w_rope, tabs, t_lat=t_lat, bn=PR_WIDTH, rope=True, name="proj_rope")

        o_f = _gla_call(pa, aw[0], ab[0], t_lat=t_lat, reverse=False, name="gla_fwd")
        o_b = _gla_call(pa, aw[1], ab[1], t_lat=t_lat, reverse=True, name="gla_bwd")
        o_swa = _swa_call(pr, pa, sink_col, t_lat=t_lat)

        q, k, v = _mla_proj_call(pa, pr, qnw, kvnw, wq, wkv, cos_m, sin_m)
        o_mla = _flash_call(q, k, v, None, bq=bq, bk=bk, q_blk0=0, n_q=t_lat // bq, kv_blk0=0,
                            n_kv=s_len // bk, name="mla_flash")
        o_mla = _flash_call(q, k, v, o_mla, bq=l_ctx, bk=l_ctx, q_blk0=t_lat // l_ctx, n_q=1,
                            kv_blk0=t_lat // l_ctx, n_kv=1, name="mla_flash_ctx")

        xa = _merge_call(xa, mod, pa, o_f, o_b, o_swa, o_mla, row_vec(gla_norm_w[layer]),
                         w_branch[layer].astype(BF16), w_out[layer].astype(BF16), t_lat=t_lat)
        xa = _mlp_call(xa, mod, row_vec(norm_mlp_w[layer]), mlp_w1[layer].astype(BF16),
                       mlp_w2[layer].astype(BF16), row_vec(final_norm_w), t_lat=t_lat,
                       final=layer == depth - 1)
    return xa[:, :t_lat]
```

```python
import functools

import jax
import jax.numpy as jnp
import numpy as np
from jax import lax
from jax.experimental import pallas as pl
from jax.experimental.pallas import tpu as pltpu

F32 = jnp.float32
BF16 = jnp.bfloat16

D_MODEL = 1024
EPS = 1e-6
ROPE_BASE = 10000.0
NEG_INF = -1e30
LOG2E = 1.4426950408889634
GRID_W = 64

GLA_HEADS = 4
GLA_HEAD_K = 128
GLA_HEAD_V = 256
GLA_RANK = 16
GLA_GATE_NORM = 16.0
GLA_CHUNK = 64
GLA_BLOCK = 256

SWA_HEADS = 8
SWA_KV_HEADS = 2
SWA_GROUP = 4
SWA_HEAD_DIM = 128
SWA_WINDOW = 128
SWA_BLOCK = 128
SWA_SCALE = SWA_HEAD_DIM ** -0.5
SWA_QTILE = 512

MLA_HEADS = 8
MLA_Q_RANK = 384
MLA_KV_RANK = 256
MLA_NOPE = 128
MLA_ROPE = 64
MLA_V = 128
MLA_SCALE = (MLA_NOPE + MLA_ROPE) ** -0.5

FF_DIM = 4 * D_MODEL
FF_CHUNK = 1024
FLASH_SUB = 256

LANE = 128
MOD_ROWS = 8
VMEM_LIMIT = 56 * 1024 * 1024

PA_GATES, PA_GR, PA_GV, PA_GQ, PA_GK = 0, 3072, 4096, 5120, 5632
PA_MCQ, PA_SV, PA_MCKV, PA_Z = 6144, 6656, 6912, 7168
PA_WIDTH = 7296
PA_BN = 2432
MCQ_PAD = 512
PR_SQ, PR_SK, PR_KPE = 0, 1024, 1280
PR_WIDTH = 1408
PR_SLABS = PR_WIDTH // LANE


def _params(sem):
    return pltpu.CompilerParams(dimension_semantics=sem, vmem_limit_bytes=VMEM_LIMIT)


def _row_tile(s):
    for tr in (640, 512, 256, 128):
        if s % tr == 0:
            return tr
    raise ValueError(f"unsupported stream length {s}")


def _rms(x, w):
    return x * lax.rsqrt(jnp.mean(x * x, axis=-1, keepdims=True) + EPS) * w


def _mod_rows(mod_ref, b, n_batch, row0, rows, t_lat, idx):
    lo = idx * D_MODEL
    lat = mod_ref[pl.ds(b, 1), lo:lo + D_MODEL]
    ctx = mod_ref[n_batch:n_batch + 1, lo:lo + D_MODEL]
    rid = row0 + lax.broadcasted_iota(jnp.int32, (rows, 1), 0)
    return jnp.where(rid >= t_lat, ctx, lat)


def _swap_halves(x, half):
    lane = lax.broadcasted_iota(jnp.int32, x.shape, 1)
    return jnp.where((lane & half) == 0, pltpu.roll(x, LANE - half, 1), pltpu.roll(x, half, 1))


def _log_sigmoid(x):
    return jnp.minimum(x, 0.0) - jnp.log1p(jnp.exp(-jnp.abs(x)))


def _ada_kernel(c_ref, w_ref, b_ref, o_ref):
    c = c_ref[...]
    a = (c * jax.nn.sigmoid(c)).astype(BF16)
    o_ref[0] = jnp.dot(a, w_ref[0].astype(BF16), preferred_element_type=F32) + b_ref[0]


def _ada_call(cin, ada_w, ada_b):
    depth = ada_w.shape[0]
    nblk = ada_w.shape[2] // D_MODEL
    return pl.pallas_call(
        _ada_kernel,
        grid=(depth, nblk),
        in_specs=[
            pl.BlockSpec((MOD_ROWS, D_MODEL), lambda l, j: (0, 0)),
            pl.BlockSpec((1, D_MODEL, D_MODEL), lambda l, j: (l, 0, j)),
            pl.BlockSpec((1, 1, D_MODEL), lambda l, j: (l, 0, j)),
        ],
        out_specs=pl.BlockSpec((1, MOD_ROWS, D_MODEL), lambda l, j: (l, 0, j)),
        out_shape=jax.ShapeDtypeStruct((depth, MOD_ROWS, ada_w.shape[2]), F32),
        compiler_params=_params(("parallel", "parallel")),
        name="ada_mod",
    )(cin, ada_w, ada_b.reshape(depth, 1, -1))


def _proj_kernel(x_ref, mod_ref, nw_ref, w_ref, *rest, n_batch, t_lat, tr, rope):
    o_ref = rest[-1]
    b = pl.program_id(1)
    row0 = pl.program_id(2) * tr
    x = x_ref[0]
    shift = _mod_rows(mod_ref, b, n_batch, row0, tr, t_lat, 0)
    scale = _mod_rows(mod_ref, b, n_batch, row0, tr, t_lat, 1)
    h = _rms(x, nw_ref[...]) * (1.0 + scale) + shift
    acc = jnp.dot(h.astype(BF16), w_ref[...], preferred_element_type=F32)
    if not rope:
        o_ref[0] = acc.astype(BF16)
        return
    cs_ref, sn_ref, cm_ref, sm_ref = rest[:4]
    cos_s, sin_s = cs_ref[...], sn_ref[...]
    for s in range(PR_SLABS):
        a = acc[:, s * LANE:(s + 1) * LANE]
        if s * LANE < PR_KPE:
            r = a * cos_s + _swap_halves(a, SWA_HEAD_DIM // 4) * sin_s
            if s * LANE < PR_SK:
                r = r * (SWA_SCALE * LOG2E)
        else:
            r = a * cm_ref[...] + _swap_halves(a, MLA_ROPE // 4) * sm_ref[...]
        o_ref[0, :, s * LANE:(s + 1) * LANE] = r.astype(BF16)


def _proj_call(xa, mod, nw, w, tabs, *, t_lat, bn, rope, name):
    n_batch, s_len, _ = xa.shape
    tr = _row_tile(s_len)
    width = w.shape[1]
    grid = (width // bn, n_batch, s_len // tr)
    in_specs = [
        pl.BlockSpec((1, tr, D_MODEL), lambda j, b, i: (b, i, 0)),
        pl.BlockSpec(mod.shape, lambda j, b, i: (0, 0)),
        pl.BlockSpec((1, D_MODEL), lambda j, b, i: (0, 0)),
        pl.BlockSpec((D_MODEL, bn), lambda j, b, i: (0, j)),
    ]
    args = [xa, mod, nw, w]
    if rope:
        in_specs += [pl.BlockSpec((tr, LANE), lambda j, b, i: (i, 0))] * 4
        args += list(tabs)
    return pl.pallas_call(
        functools.partial(_proj_kernel, n_batch=n_batch, t_lat=t_lat, tr=tr, rope=rope),
        grid=grid,
        in_specs=in_specs,
        out_specs=pl.BlockSpec((1, tr, bn), lambda j, b, i: (b, i, j)),
        out_shape=jax.ShapeDtypeStruct((n_batch, s_len, width), BF16),
        compiler_params=_params(("parallel", "parallel", "parallel")),
        name=name,
    )(*args)


def _gla_decay(z_ref, aw, ab, reverse):
    gb, ch, dk = GLA_BLOCK, GLA_CHUNK, GLA_HEAD_K
    la = jnp.dot(z_ref[0], aw, preferred_element_type=F32) + ab
    la = _log_sigmoid(la) * (1.0 / GLA_GATE_NORM)

    r = lax.broadcasted_iota(jnp.int32, (gb, gb), 0)
    c = lax.broadcasted_iota(jnp.int32, (gb, gb), 1)
    same = (r // ch) == (c // ch)
    tri = (c >= r) if reverse else (c <= r)
    tmat = jnp.where(same & tri, 1.0, 0.0).astype(BF16)
    hi = la.astype(BF16)
    lo = (la - hi.astype(F32)).astype(BF16)
    hw = GLA_HEADS * dk
    res = jnp.dot(tmat, jnp.concatenate([hi, lo], axis=1), preferred_element_type=F32)
    bcum_all = res[:, :hw] + res[:, hw:]

    return bcum_all, same & tri


def _gla_kernel(qf_ref, kf_ref, vf_ref, zf_ref, qb_ref, kb_ref, vb_ref, zb_ref, aw_ref, ab_ref,
                of_ref, ob_ref, stf_ref, stb_ref):
    @pl.when(pl.program_id(1) == 0)
    def _():
        stf_ref[...] = jnp.zeros_like(stf_ref)
        stb_ref[...] = jnp.zeros_like(stb_ref)

    gb, ch, dk, dv = GLA_BLOCK, GLA_CHUNK, GLA_HEAD_K, GLA_HEAD_V
    n_ch = gb // ch
    nt = (((1,), (1,)), ((), ()))
    dirs = ((qf_ref, kf_ref, vf_ref, zf_ref, of_ref, stf_ref, False),
            (qb_ref, kb_ref, vb_ref, zb_ref, ob_ref, stb_ref, True))
    fronts = [_gla_decay(z_ref, aw_ref[d], ab_ref[d], rev) for d, (_, _, _, z_ref, _, _, rev) in enumerate(dirs)]

    chains = []
    for (q_ref, k_ref, v_ref, _, o_ref, st_ref, rev), (bcum_all, mask) in zip(dirs, fronts):
        order = range(n_ch - 1, -1, -1) if rev else range(n_ch)
        for h in range(GLA_HEADS):
            bcum = bcum_all[:, h * dk:(h + 1) * dk]
            q = q_ref[0, :, h * dk:(h + 1) * dk].astype(F32)
            k = k_ref[0, :, h * dk:(h + 1) * dk].astype(F32)
            q_dec = (q * (dk ** -0.5) * jnp.exp(bcum)).astype(BF16)
            k_inv = (k * jnp.exp(-bcum)).astype(BF16)
            v = v_ref[0, :, h * dv:(h + 1) * dv]
            a = lax.dot_general(q_dec, k_inv, nt, preferred_element_type=F32)
            tots, incs = [], []
            for cidx in order:
                lo_r, hi_r = cidx * ch, (cidx + 1) * ch
                last = lo_r if rev else hi_r - 1
                tot = bcum[last:last + 1, :]
                k_end = k[lo_r:hi_r] * jnp.exp(tot - bcum[lo_r:hi_r])
                incs.append(jnp.dot(k_end.T.astype(BF16), v[lo_r:hi_r], preferred_element_type=F32))
                tots.append(tot)
            pad = jnp.zeros((8 - n_ch, dk), F32)
            g_cols = jnp.exp(jnp.concatenate(tots + [pad], axis=0)).T
            chains.append((o_ref, st_ref, h, order, mask, q_dec, v, a, incs, g_cols))

    stage2 = []
    for o_ref, st_ref, h, order, mask, q_dec, v, a, incs, g_cols in chains:
        a = jnp.where(mask, a, 0.0).astype(BF16)
        st = st_ref[h]
        entering = []
        for i, _ in enumerate(order):
            entering.append(st.astype(BF16))
            st = st * g_cols[:, i:i + 1] + incs[i]
        st_ref[h] = st
        stage2.append((o_ref, h, order, q_dec, v, a, entering))

    for o_ref, h, order, q_dec, v, a, entering in stage2:
        o_intra = jnp.dot(a, v, preferred_element_type=F32)
        for i, cidx in enumerate(order):
            lo_r, hi_r = cidx * ch, (cidx + 1) * ch
            o = o_intra[lo_r:hi_r] + jnp.dot(q_dec[lo_r:hi_r], entering[i], preferred_element_type=F32)
            o_ref[0, lo_r:hi_r, h * dv:(h + 1) * dv] = o.astype(BF16)


def _gla_call(pa, aw, ab, *, t_lat):
    n_batch, s_len, _ = pa.shape
    gb = GLA_BLOCK
    n_lat, n_ctx = t_lat // gb, (s_len - t_lat) // gb
    nblk = n_lat + n_ctx
    qk_w, v_w = GLA_HEADS * GLA_HEAD_K, GLA_HEADS * GLA_HEAD_V

    fwd = lambda n: jnp.where(n < n_ctx, n_lat + n, n - n_ctx)
    bwd = lambda n: jnp.where(n < n_ctx, n_lat + n_ctx - 1 - n, n_lat - 1 - (n - n_ctx))

    def specs(blk):
        return [
            pl.BlockSpec((1, gb, qk_w), lambda b, n: (b, blk(n), PA_GQ // qk_w)),
            pl.BlockSpec((1, gb, qk_w), lambda b, n: (b, blk(n), PA_GK // qk_w)),
            pl.BlockSpec((1, gb, v_w), lambda b, n: (b, blk(n), PA_GV // v_w)),
            pl.BlockSpec((1, gb, LANE), lambda b, n: (b, blk(n), PA_Z // LANE)),
        ]

    out = jax.ShapeDtypeStruct((n_batch, s_len, v_w), BF16)
    state = pltpu.VMEM((GLA_HEADS, GLA_HEAD_K, GLA_HEAD_V), F32)
    return pl.pallas_call(
        _gla_kernel,
        grid=(n_batch, nblk),
        in_specs=specs(fwd) + specs(bwd) + [
            pl.BlockSpec(aw.shape, lambda b, n: (0, 0, 0)),
            pl.BlockSpec(ab.shape, lambda b, n: (0, 0, 0)),
        ],
        out_specs=[pl.BlockSpec((1, gb, v_w), lambda b, n: (b, fwd(n), 0)),
                   pl.BlockSpec((1, gb, v_w), lambda b, n: (b, bwd(n), 0))],
        out_shape=[out, out],
        scratch_shapes=[state, state],
        compiler_params=_params(("parallel", "arbitrary")),
        name="gla",
    )(pa, pa, pa, pa, pa, pa, pa, pa, aw, ab)


def _with_one_hot(v):
    one_hot = (lax.broadcasted_iota(jnp.int32, v.shape, 1) == 0).astype(v.dtype)
    return jnp.concatenate([v, one_hot], axis=1)


def _swa_kernel(*refs, local, aliased):
    if aliased:
        refs = refs[1:]
    if local:
        q_ref, kp_ref, kc_ref, kn_ref, vp_ref, vc_ref, vn_ref, kx_ref, vx_ref, sink_ref, band_ref, o_ref = refs
    else:
        q_ref, kx_ref, vx_ref, sink_ref, o_ref = refs
    nt = (((1,), (1,)), ((), ()))
    kx = kx_ref[0]
    vx = _with_one_hot(vx_ref[0])
    if local:
        n = pl.program_id(2)
        blk = SWA_BLOCK
        kb = jnp.concatenate([kp_ref[0], kc_ref[0], kn_ref[0]], axis=0)
        vb = _with_one_hot(jnp.concatenate([vp_ref[0], vc_ref[0], vn_ref[0]], axis=0))
        n_loc = kb.shape[0]
        col = lax.broadcasted_iota(jnp.int32, (1, n_loc), 1)
        edge = jnp.where((col < blk) & (n == 0), NEG_INF, 0.0)
        edge = jnp.where((col >= n_loc - blk) & (n == pl.num_programs(2) - 1), NEG_INF, edge)
        bias = band_ref[...] + edge
    heads = range(SWA_GROUP)
    qs = [q_ref[0, :, g * LANE:(g + 1) * LANE] for g in heads]
    s_ctx = [lax.dot_general(q, kx, nt, preferred_element_type=F32) for q in qs]
    s_loc = [lax.dot_general(q, kb, nt, preferred_element_type=F32) + bias for q in qs] if local else None
    stage2 = []
    for g in heads:
        sink = sink_ref[0, g:g + 1, 0:1]
        m = jnp.maximum(jnp.max(s_ctx[g], axis=1, keepdims=True), sink)
        if local:
            m = jnp.maximum(m, jnp.max(s_loc[g], axis=1, keepdims=True))
        p_ctx = jnp.exp2(s_ctx[g] - m).astype(BF16)
        p_loc = jnp.exp2(s_loc[g] - m).astype(BF16) if local else None
        stage2.append((p_ctx, p_loc, jnp.exp2(sink - m)))
    for g, (p_ctx, p_loc, p_sink) in enumerate(stage2):
        o = jnp.dot(p_ctx, vx, preferred_element_type=F32)
        if local:
            o += jnp.dot(p_loc, vb, preferred_element_type=F32)
        den = o[:, SWA_HEAD_DIM:SWA_HEAD_DIM + 1] + p_sink
        o_ref[0, :, g * LANE:(g + 1) * LANE] = (o[:, :SWA_HEAD_DIM] / den).astype(BF16)


def _swa_band_table(bq):
    r = np.arange(bq)[:, None]
    c = np.arange(bq + 2 * SWA_BLOCK)[None, :] - SWA_BLOCK
    return jnp.asarray(np.where(np.abs(r - c) <= SWA_WINDOW, 0.0, NEG_INF), F32)


def _swa_call(pr, pa, sink2, *, t_lat):
    n_batch, s_len, _ = pr.shape
    blk = SWA_BLOCK
    l_ctx = s_len - t_lat
    bq = SWA_QTILE if t_lat % SWA_QTILE == 0 else blk
    per = bq // blk
    n_blk = t_lat // blk
    ctx_blk = t_lat // l_ctx
    gw = SWA_GROUP * LANE
    kcol = lambda kh: PR_SK // LANE + kh
    vcol = lambda kh: PA_SV // LANE + kh
    prev = lambda n: jnp.maximum(n * per - 1, 0)
    nxt = lambda n: jnp.minimum((n + 1) * per, n_blk - 1)
    edge = lambda col, pos: pl.BlockSpec((1, blk, LANE), lambda b, kh, n: (b, pos(n), col(kh)))
    body = lambda col: pl.BlockSpec((1, bq, LANE), lambda b, kh, n: (b, n, col(kh)))
    ctx = lambda col: pl.BlockSpec((1, l_ctx, LANE), lambda b, kh, n: (b, ctx_blk, col(kh)))
    sink_spec = pl.BlockSpec((1, SWA_GROUP, LANE), lambda b, kh, n: (kh, 0, 0))
    out_shape = jax.ShapeDtypeStruct((n_batch, s_len, SWA_HEADS * SWA_HEAD_DIM), BF16)
    band = _swa_band_table(bq)
    o_lat = pl.pallas_call(
        functools.partial(_swa_kernel, local=True, aliased=False),
        grid=(n_batch, SWA_KV_HEADS, t_lat // bq),
        in_specs=[
            pl.BlockSpec((1, bq, gw), lambda b, kh, n: (b, n, PR_SQ // gw + kh)),
            edge(kcol, prev), body(kcol), edge(kcol, nxt),
            edge(vcol, prev), body(vcol), edge(vcol, nxt),
            ctx(kcol), ctx(vcol), sink_spec,
            pl.BlockSpec(band.shape, lambda b, kh, n: (0, 0)),
        ],
        out_specs=pl.BlockSpec((1, bq, gw), lambda b, kh, n: (b, n, kh)),
        out_shape=out_shape,
        compiler_params=_params(("parallel", "parallel", "parallel")),
        name="swa",
    )(pr, pr, pr, pr, pa, pa, pa, pr, pa, sink2, band)
    return pl.pallas_call(
        functools.partial(_swa_kernel, local=False, aliased=True),
        grid=(n_batch, SWA_KV_HEADS, 1),
        in_specs=[
            pl.BlockSpec(memory_space=pl.ANY),
            pl.BlockSpec((1, l_ctx, gw), lambda b, kh, n: (b, ctx_blk, PR_SQ // gw + kh)),
            ctx(kcol), ctx(vcol), sink_spec,
        ],
        out_specs=pl.BlockSpec((1, l_ctx, gw), lambda b, kh, n: (b, ctx_blk, kh)),
        out_shape=out_shape,
        input_output_aliases={0: 0},
        compiler_params=_params(("parallel", "parallel", "parallel")),
        name="swa_ctx",
    )(o_lat, pr, pr, pa, sink2)


def _mla_proj_kernel(cq_ref, ckv_ref, kpe_ref, qnw_ref, kvnw_ref, wq_ref, wkv_ref, cm_ref, sm_ref,
                     q_ref, k_ref, v_ref):
    cq = cq_ref[0].astype(F32)
    ms = jnp.sum(cq * cq, axis=-1, keepdims=True) * (1.0 / MLA_Q_RANK)
    cqn = (cq * lax.rsqrt(ms + EPS) * qnw_ref[...]).astype(BF16)
    q = jnp.dot(cqn, wq_ref[...], preferred_element_type=F32)
    cos_m, sin_m = cm_ref[...], sm_ref[...]
    qs = MLA_SCALE * LOG2E
    hw = 2 * LANE
    for h in range(MLA_HEADS):
        q_ref[0, :, h * hw:h * hw + LANE] = (q[:, h * hw:h * hw + LANE] * qs).astype(BF16)
        pe = q[:, h * hw + LANE:(h + 1) * hw]
        pe = (pe * cos_m + _swap_halves(pe, MLA_ROPE // 4) * sin_m) * qs
        q_ref[0, :, h * hw + LANE:(h + 1) * hw] = pe.astype(BF16)
    ckv = _rms(ckv_ref[0].astype(F32), kvnw_ref[...]).astype(BF16)
    kv = jnp.dot(ckv, wkv_ref[...], preferred_element_type=F32)
    half = MLA_HEADS * MLA_NOPE
    kpe = kpe_ref[0]
    one_hot = (lax.broadcasted_iota(jnp.int32, kpe.shape, 1) == 0).astype(BF16)
    for h in range(MLA_HEADS):
        k_ref[0, :, h * hw:h * hw + LANE] = kv[:, h * LANE:(h + 1) * LANE].astype(BF16)
        k_ref[0, :, h * hw + LANE:(h + 1) * hw] = kpe
        v_ref[0, :, h * hw:h * hw + LANE] = kv[:, half + h * LANE:half + (h + 1) * LANE].astype(BF16)
        v_ref[0, :, h * hw + LANE:(h + 1) * hw] = one_hot


def _mla_proj_call(pa, pr, qnw, kvnw, wq, wkv, cos_m, sin_m):
    n_batch, s_len, _ = pa.shape
    tr = _row_tile(s_len)
    qw = MLA_HEADS * 2 * LANE
    kw = MLA_HEADS * MLA_NOPE
    const = lambda shape: pl.BlockSpec(shape, lambda b, i: (0,) * len(shape))
    wide = pl.BlockSpec((1, tr, qw), lambda b, i: (b, i, 0))
    return pl.pallas_call(
        _mla_proj_kernel,
        grid=(n_batch, s_len // tr),
        in_specs=[
            pl.BlockSpec((1, tr, MCQ_PAD), lambda b, i: (b, i, PA_MCQ // MCQ_PAD)),
            pl.BlockSpec((1, tr, MLA_KV_RANK), lambda b, i: (b, i, PA_MCKV // MLA_KV_RANK)),
            pl.BlockSpec((1, tr, LANE), lambda b, i: (b, i, PR_KPE // LANE)),
            const((1, MCQ_PAD)), const((1, MLA_KV_RANK)),
            const((MCQ_PAD, qw)), const((MLA_KV_RANK, 2 * kw)),
            pl.BlockSpec((tr, LANE), lambda b, i: (i, 0)),
            pl.BlockSpec((tr, LANE), lambda b, i: (i, 0)),
        ],
        out_specs=[wide, wide, wide],
        out_shape=[jax.ShapeDtypeStruct((n_batch, s_len, qw), BF16)] * 3,
        compiler_params=_params(("parallel", "parallel")),
        name="mla_proj",
    )(pa, pa, pr, qnw, kvnw, wq, wkv, cos_m, sin_m)


def _flash_kernel(*refs, aliased, sub):
    if aliased:
        refs = refs[1:]
    q_ref, k_ref, v_ref, o_ref, m_ref, acc_ref = refs
    j = pl.program_id(3)

    @pl.when(j == 0)
    def _():
        m_ref[...] = jnp.full_like(m_ref, NEG_INF)
        acc_ref[...] = jnp.zeros_like(acc_ref)

    k = k_ref[0]
    v = v_ref[0]
    bq = q_ref.shape[1]
    sub = min(sub, bq)
    chains = [slice(c * sub, (c + 1) * sub) for c in range(bq // sub)]
    scores = [lax.dot_general(q_ref[0, rows, :], k, (((1,), (1,)), ((), ())), preferred_element_type=F32)
              for rows in chains]
    probs, alphas = [], []
    for rows, s in zip(chains, scores):
        m_prev = m_ref[rows, :]
        m_new = jnp.maximum(m_prev, jnp.max(s, axis=1, keepdims=True))
        alphas.append(jnp.exp2(m_prev - m_new))
        probs.append(jnp.exp2(s - m_new).astype(BF16))
        m_ref[rows, :] = m_new
    for rows, p, alpha in zip(chains, probs, alphas):
        acc_ref[rows, :] = alpha * acc_ref[rows, :] + jnp.dot(p, v, preferred_element_type=F32)

    @pl.when(j == pl.num_programs(3) - 1)
    def _():
        acc = acc_ref[...]
        o_ref[0] = (acc[:, :MLA_V] / acc[:, MLA_V:MLA_V + 1]).astype(BF16)


def _flash_call(q, k, v, prev_out, *, bq, bk, q_blk0, n_q, kv_blk0, n_kv, name):
    n_batch, s_len, _ = q.shape
    aliased = prev_out is not None
    hw = 2 * LANE
    in_specs = [
        pl.BlockSpec((1, bq, hw), lambda b, h, i, j: (b, q_blk0 + i, h)),
        pl.BlockSpec((1, bk, hw), lambda b, h, i, j: (b, kv_blk0 + j, h)),
        pl.BlockSpec((1, bk, hw), lambda b, h, i, j: (b, kv_blk0 + j, h)),
    ]
    args = [q, k, v]
    if aliased:
        in_specs = [pl.BlockSpec(memory_space=pl.ANY)] + in_specs
        args = [prev_out] + args
    return pl.pallas_call(
        functools.partial(_flash_kernel, aliased=aliased, sub=FLASH_SUB),
        grid=(n_batch, MLA_HEADS, n_q, n_kv),
        in_specs=in_specs,
        out_specs=pl.BlockSpec((1, bq, LANE), lambda b, h, i, j: (b, q_blk0 + i, h)),
        out_shape=jax.ShapeDtypeStruct((n_batch, s_len, MLA_HEADS * MLA_V), BF16),
        scratch_shapes=[pltpu.VMEM((bq, 1), F32), pltpu.VMEM((bq, hw), F32)],
        input_output_aliases={0: 0} if aliased else {},
        compiler_params=_params(("parallel", "parallel", "parallel", "arbitrary")),
        name=name,
    )(*args)


def _kv_block(s_len):
    for bk in (3328, 1280, 640, 256, 128):
        if s_len % bk == 0:
            return bk
    raise ValueError(f"unsupported stream length {s_len}")


def _merge_kernel(x_ref, mod_ref, g_ref, gr_ref, of_ref, ob_ref, osw_ref, om_ref, gw_ref, wb_ref, wo_ref,
                  o_ref, *, n_batch, t_lat, tr):
    b = pl.program_id(0)
    row0 = pl.program_id(1) * tr
    og = of_ref[0].astype(F32) + ob_ref[0].astype(F32)
    gw = gw_ref[...]
    heads = []
    for h in range(GLA_HEADS):
        heads.append(_rms(og[:, h * GLA_HEAD_V:(h + 1) * GLA_HEAD_V], gw))
    gr = gr_ref[0].astype(F32)
    o_gla = (jnp.concatenate(heads, axis=1) * (gr * jax.nn.sigmoid(gr))).astype(BF16)
    branches = (o_gla, osw_ref[0], om_ref[0])
    merged = None
    for idx, ob in enumerate(branches):
        gate = jax.nn.sigmoid(g_ref[0, :, idx * D_MODEL:(idx + 1) * D_MODEL].astype(F32))
        term = gate * jnp.dot(ob, wb_ref[idx], preferred_element_type=F32)
        merged = term if merged is None else merged + term
    y = jnp.dot(merged.astype(BF16), wo_ref[...], preferred_element_type=F32)
    g_m = _mod_rows(mod_ref, b, n_batch, row0, tr, t_lat, 2)
    o_ref[0] = x_ref[0] + g_m * y


def _merge_call(xa, mod, pa, o_f, o_b, o_swa, o_mla, gw, wb, wo, *, t_lat):
    n_batch, s_len, _ = xa.shape
    tr = 320 if s_len % 320 == 0 else _row_tile(s_len)
    row = lambda width, col: pl.BlockSpec((1, tr, width), lambda b, i: (b, i, col))
    return pl.pallas_call(
        functools.partial(_merge_kernel, n_batch=n_batch, t_lat=t_lat, tr=tr),
        grid=(n_batch, s_len // tr),
        in_specs=[
            row(D_MODEL, 0),
            pl.BlockSpec(mod.shape, lambda b, i: (0, 0)),
            row(3 * D_MODEL, PA_GATES // (3 * D_MODEL)),
            row(D_MODEL, PA_GR // D_MODEL),
            row(D_MODEL, 0), row(D_MODEL, 0), row(D_MODEL, 0), row(D_MODEL, 0),
            pl.BlockSpec((1, GLA_HEAD_V), lambda b, i: (0, 0)),
            pl.BlockSpec((3, D_MODEL, D_MODEL), lambda b, i: (0, 0, 0)),
            pl.BlockSpec((D_MODEL, D_MODEL), lambda b, i: (0, 0)),
        ],
        out_specs=row(D_MODEL, 0),
        out_shape=jax.ShapeDtypeStruct(xa.shape, F32),
        compiler_params=_params(("parallel", "parallel")),
        name="merge",
    )(xa, mod, pa, pa, o_f, o_b, o_swa, o_mla, gw, wb, wo)


def _mlp_kernel(x_ref, mod_ref, nw_ref, w1_ref, w2_ref, fw_ref, o_ref, h_ref, acc_ref,
                *, n_batch, t_lat, tr, final):
    b = pl.program_id(0)
    row0 = pl.program_id(1) * tr
    c = pl.program_id(2)

    @pl.when(c == 0)
    def _():
        shift = _mod_rows(mod_ref, b, n_batch, row0, tr, t_lat, 3)
        scale = _mod_rows(mod_ref, b, n_batch, row0, tr, t_lat, 4)
        h_ref[...] = (_rms(x_ref[0], nw_ref[...]) * (1.0 + scale) + shift).astype(BF16)
        acc_ref[...] = jnp.zeros_like(acc_ref)

    u = jnp.maximum(jnp.dot(h_ref[...], w1_ref[...], preferred_element_type=F32), 0.0)
    acc_ref[...] += jnp.dot((u * u).astype(BF16), w2_ref[...], preferred_element_type=F32)

    @pl.when(c == pl.num_programs(2) - 1)
    def _():
        g_f = _mod_rows(mod_ref, b, n_batch, row0, tr, t_lat, 5)
        y = x_ref[0] + g_f * acc_ref[...]
        if final:
            y = _rms(y, fw_ref[...])
        o_ref[0] = y


def _mlp_call(xa, mod, nw, w1, w2, fw, *, t_lat, final):
    n_batch, s_len, _ = xa.shape
    tr = _row_tile(s_len)
    return pl.pallas_call(
        functools.partial(_mlp_kernel, n_batch=n_batch, t_lat=t_lat, tr=tr, final=final),
        grid=(n_batch, s_len // tr, FF_DIM // FF_CHUNK),
        in_specs=[
            pl.BlockSpec((1, tr, D_MODEL), lambda b, i, c: (b, i, 0)),
            pl.BlockSpec(mod.shape, lambda b, i, c: (0, 0)),
            pl.BlockSpec((1, D_MODEL), lambda b, i, c: (0, 0)),
            pl.BlockSpec((D_MODEL, FF_CHUNK), lambda b, i, c: (0, c)),
            pl.BlockSpec((FF_CHUNK, D_MODEL), lambda b, i, c: (c, 0)),
            pl.BlockSpec((1, D_MODEL), lambda b, i, c: (0, 0)),
        ],
        out_specs=pl.BlockSpec((1, tr, D_MODEL), lambda b, i, c: (b, i, 0)),
        out_shape=jax.ShapeDtypeStruct(xa.shape, F32),
        scratch_shapes=[pltpu.VMEM((tr, D_MODEL), BF16), pltpu.VMEM((tr, D_MODEL), F32)],
        compiler_params=_params(("parallel", "parallel", "arbitrary")),
        name="mlp",
    )(xa, mod, nw, w1, w2, fw)


def _rope_tables(t_lat, l_ctx, dim):
    pos = jnp.arange(t_lat)
    row, col = pos // GRID_W, pos % GRID_W
    d_axis = dim // 2
    inv = ROPE_BASE ** (-jnp.arange(0, d_axis, 2, dtype=F32) / d_axis)
    ang_r = row.astype(F32)[:, None] * inv
    ang_c = col.astype(F32)[:, None] * inv
    cos = jnp.concatenate([jnp.cos(ang_r)] * 2 + [jnp.cos(ang_c)] * 2, axis=1)
    sin = jnp.concatenate([-jnp.sin(ang_r), jnp.sin(ang_r), -jnp.sin(ang_c), jnp.sin(ang_c)], axis=1)
    if dim < LANE:
        cos = jnp.concatenate([cos, jnp.ones((t_lat, LANE - dim), F32)], axis=1)
        sin = jnp.concatenate([sin, jnp.zeros((t_lat, LANE - dim), F32)], axis=1)
    cos = jnp.concatenate([cos, jnp.ones((l_ctx, LANE), F32)], axis=0)
    sin = jnp.concatenate([sin, jnp.zeros((l_ctx, LANE), F32)], axis=0)
    return cos, sin


def _split_w_in(w):
    sizes = (512, 512, 1024, 1024, GLA_RANK, GLA_RANK, 1024, 256, 256, MLA_Q_RANK, MLA_KV_RANK, MLA_ROPE,
             3 * D_MODEL)
    offs = np.cumsum((0,) + sizes)
    return [w[:, offs[i]:offs[i + 1]] for i in range(len(sizes))]


def _layer_weights(w_in, gla_a_w, gla_a_b, mla_q_norm_w, mla_w_uq, mla_kv_norm_w, mla_w_ukv, swa_sink):
    gq, gk, gv, gr, gzf, gzb, sq, sk, sv, mcq, mckv, mkr, gates = _split_w_in(w_in)
    zpad = lambda n: jnp.zeros((D_MODEL, n), F32)
    w_plain = jnp.concatenate(
        [gates, gr, gv, gq, gk, mcq, zpad(MCQ_PAD - MLA_Q_RANK), sv, mckv, gzf, gzb, zpad(LANE - 2 * GLA_RANK)],
        axis=1).astype(BF16)
    w_rope = jnp.concatenate([sq, sk, mkr, zpad(LANE - MLA_ROPE)], axis=1).astype(BF16)
    aw = jnp.zeros((2, LANE, GLA_HEADS * GLA_HEAD_K), F32)
    aw = aw.at[0, :GLA_RANK].set(gla_a_w[0]).at[1, GLA_RANK:2 * GLA_RANK].set(gla_a_w[1])
    aw = aw.astype(BF16)
    ab = gla_a_b.reshape(2, 1, GLA_HEADS * GLA_HEAD_K)
    wq = mla_w_uq.reshape(MLA_Q_RANK, MLA_HEADS, MLA_NOPE + MLA_ROPE)
    wq = jnp.pad(wq, ((0, MCQ_PAD - MLA_Q_RANK), (0, 0), (0, 2 * LANE - MLA_NOPE - MLA_ROPE)))
    wq = wq.reshape(MCQ_PAD, MLA_HEADS * 2 * LANE).astype(BF16)
    qnw = jnp.pad(mla_q_norm_w, (0, MCQ_PAD - MLA_Q_RANK)).reshape(1, MCQ_PAD)
    wkv = mla_w_ukv.reshape(MLA_KV_RANK, MLA_HEADS, MLA_NOPE + MLA_V)
    wkv = jnp.concatenate([wkv[:, :, :MLA_NOPE].reshape(MLA_KV_RANK, -1),
                           wkv[:, :, MLA_NOPE:].reshape(MLA_KV_RANK, -1)], axis=1).astype(BF16)
    kvnw = mla_kv_norm_w.reshape(1, MLA_KV_RANK)
    sink2 = jnp.broadcast_to((swa_sink * LOG2E).reshape(SWA_KV_HEADS, SWA_GROUP, 1),
                             (SWA_KV_HEADS, SWA_GROUP, LANE))
    return w_plain, w_rope, aw, ab, wq, qnw, wkv, kvnw, sink2


def kernel(x, c, ctx, c_ctx, ada_w, ada_b, norm_mix_w, w_in, gla_a_w, gla_a_b, gla_norm_w, swa_sink,
           mla_q_norm_w, mla_w_uq, mla_kv_norm_w, mla_w_ukv, w_branch, w_out, norm_mlp_w, mlp_w1, mlp_w2,
           final_norm_w):
    n_batch, t_lat, d_model = x.shape
    l_ctx = ctx.shape[1]
    depth = ada_w.shape[0]
    s_len = t_lat + l_ctx
    assert d_model == D_MODEL and n_batch + 1 <= MOD_ROWS
    assert t_lat % GLA_BLOCK == 0 and l_ctx % GLA_BLOCK == 0 and t_lat % l_ctx == 0 and t_lat % GRID_W == 0

    xa = jnp.concatenate([x, ctx], axis=1)
    cin = jnp.zeros((MOD_ROWS, D_MODEL), F32).at[:n_batch].set(c).at[n_batch].set(c_ctx)
    mod_all = _ada_call(cin, ada_w, ada_b)

    cos_s, sin_s = _rope_tables(t_lat, l_ctx, SWA_HEAD_DIM)
    cos_m, sin_m = _rope_tables(t_lat, l_ctx, MLA_ROPE)
    tabs = (cos_s, sin_s, cos_m, sin_m)

    bq = min(1024, t_lat)
    bk = _kv_block(s_len)
    row_vec = lambda v: v.reshape(1, -1)

    for layer in range(depth):
        mod = mod_all[layer]
        (w_plain, w_rope, aw, ab, wq, qnw, wkv, kvnw, sink2) = _layer_weights(
            w_in[layer], gla_a_w[layer], gla_a_b[layer], mla_q_norm_w[layer], mla_w_uq[layer],
            mla_kv_norm_w[layer], mla_w_ukv[layer], swa_sink[layer])
        nw = row_vec(norm_mix_w[layer])
        pa = _proj_call(xa, mod, nw, w_plain, None, t_lat=t_lat, bn=PA_BN, rope=False, name="proj_plain")
        pr = _proj_call(xa, mod, nw, w_rope, tabs, t_lat=t_lat, bn=PR_WIDTH, rope=True, name="proj_rope")

        o_f, o_b = _gla_call(pa, aw, ab, t_lat=t_lat)
        o_swa = _swa_call(pr, pa, sink2, t_lat=t_lat)

        q, k, v = _mla_proj_call(pa, pr, qnw, kvnw, wq, wkv, cos_m, sin_m)
        o_mla = _flash_call(q, k, v, None, bq=bq, bk=bk, q_blk0=0, n_q=t_lat // bq, kv_blk0=0,
                            n_kv=s_len // bk, name="mla_flash")
        o_mla = _flash_call(q, k, v, o_mla, bq=l_ctx, bk=l_ctx, q_blk0=t_lat // l_ctx, n_q=1,
                            kv_blk0=t_lat // l_ctx, n_kv=1, name="mla_flash_ctx")

        xa = _merge_call(xa, mod, pa, o_f, o_b, o_swa, o_mla, row_vec(gla_norm_w[layer]),
                         w_branch[layer].astype(BF16), w_out[layer].astype(BF16), t_lat=t_lat)
        xa = _mlp_call(xa, mod, row_vec(norm_mlp_w[layer]), mlp_w1[layer].astype(BF16),
                       mlp_w2[layer].astype(BF16), row_vec(final_norm_w), t_lat=t_lat,
                       final=layer == depth - 1)
    return xa[:, :t_lat]
```

```python
import functools

import jax
import jax.numpy as jnp
import numpy as np
from jax import lax
from jax.experimental import pallas as pl
from jax.experimental.pallas import tpu as pltpu

F32 = jnp.float32
BF16 = jnp.bfloat16

D_MODEL = 1024
EPS = 1e-6
ROPE_BASE = 10000.0
NEG_INF = -1e30
LOG2E = 1.4426950408889634
GRID_W = 64

GLA_HEADS = 4
GLA_HEAD_K = 128
GLA_HEAD_V = 256
GLA_RANK = 16
GLA_GATE_NORM = 16.0
GLA_CHUNK = 64
GLA_BLOCK = 256

SWA_HEADS = 8
SWA_KV_HEADS = 2
SWA_GROUP = 4
SWA_HEAD_DIM = 128
SWA_WINDOW = 128
SWA_BLOCK = 128
SWA_SCALE = SWA_HEAD_DIM ** -0.5
SWA_QTILE = 512

MLA_HEADS = 8
MLA_Q_RANK = 384
MLA_KV_RANK = 256
MLA_NOPE = 128
MLA_ROPE = 64
MLA_V = 128
MLA_SCALE = (MLA_NOPE + MLA_ROPE) ** -0.5

FF_DIM = 4 * D_MODEL
FF_CHUNK = 2048
FLASH_SUB = 256

LANE = 128
MOD_ROWS = 8
VMEM_LIMIT = 56 * 1024 * 1024

PA_GATES, PA_GR, PA_GV, PA_GQ, PA_GK = 0, 3072, 4096, 5120, 5632
PA_MCQ, PA_SV, PA_MCKV, PA_Z = 6144, 6656, 6912, 7168
PA_WIDTH = 7296
PA_BN = 2432
MCQ_PAD = 512
PR_SQ, PR_SK, PR_KPE = 0, 1024, 1280
PR_WIDTH = 1408
PR_SLABS = PR_WIDTH // LANE


def _params(sem):
    return pltpu.CompilerParams(dimension_semantics=sem, vmem_limit_bytes=VMEM_LIMIT)


def _row_tile(s):
    for tr in (640, 512, 256, 128):
        if s % tr == 0:
            return tr
    raise ValueError(f"unsupported stream length {s}")


def _rms(x, w):
    return x * lax.rsqrt(jnp.mean(x * x, axis=-1, keepdims=True) + EPS) * w


def _mod_rows(mod_ref, b, n_batch, row0, rows, t_lat, idx):
    lo = idx * D_MODEL
    lat = mod_ref[pl.ds(b, 1), lo:lo + D_MODEL]
    ctx = mod_ref[n_batch:n_batch + 1, lo:lo + D_MODEL]
    rid = row0 + lax.broadcasted_iota(jnp.int32, (rows, 1), 0)
    return jnp.where(rid >= t_lat, ctx, lat)


def _swap_halves(x, half):
    lane = lax.broadcasted_iota(jnp.int32, x.shape, 1)
    return jnp.where((lane & half) == 0, pltpu.roll(x, LANE - half, 1), pltpu.roll(x, half, 1))


def _log_sigmoid(x):
    return jnp.minimum(x, 0.0) - jnp.log1p(jnp.exp(-jnp.abs(x)))


def _ada_kernel(c_ref, w_ref, b_ref, o_ref):
    c = c_ref[...]
    a = (c * jax.nn.sigmoid(c)).astype(BF16)
    o_ref[0] = jnp.dot(a, w_ref[0].astype(BF16), preferred_element_type=F32) + b_ref[0]


def _ada_call(cin, ada_w, ada_b):
    depth = ada_w.shape[0]
    nblk = ada_w.shape[2] // D_MODEL
    return pl.pallas_call(
        _ada_kernel,
        grid=(depth, nblk),
        in_specs=[
            pl.BlockSpec((MOD_ROWS, D_MODEL), lambda l, j: (0, 0)),
            pl.BlockSpec((1, D_MODEL, D_MODEL), lambda l, j: (l, 0, j)),
            pl.BlockSpec((1, 1, D_MODEL), lambda l, j: (l, 0, j)),
        ],
        out_specs=pl.BlockSpec((1, MOD_ROWS, D_MODEL), lambda l, j: (l, 0, j)),
        out_shape=jax.ShapeDtypeStruct((depth, MOD_ROWS, ada_w.shape[2]), F32),
        compiler_params=_params(("parallel", "parallel")),
        name="ada_mod",
    )(cin, ada_w, ada_b.reshape(depth, 1, -1))


def _proj_kernel(x_ref, mod_ref, nw_ref, w_ref, *rest, n_batch, t_lat, tr, rope):
    o_ref = rest[-1]
    b = pl.program_id(1)
    row0 = pl.program_id(2) * tr
    x = x_ref[0]
    shift = _mod_rows(mod_ref, b, n_batch, row0, tr, t_lat, 0)
    scale = _mod_rows(mod_ref, b, n_batch, row0, tr, t_lat, 1)
    h = _rms(x, nw_ref[...]) * (1.0 + scale) + shift
    acc = jnp.dot(h.astype(BF16), w_ref[...], preferred_element_type=F32)
    if not rope:
        o_ref[0] = acc.astype(BF16)
        return
    cs_ref, sn_ref, cm_ref, sm_ref = rest[:4]
    cos_s, sin_s = cs_ref[...], sn_ref[...]
    for s in range(PR_SLABS):
        a = acc[:, s * LANE:(s + 1) * LANE]
        if s * LANE < PR_KPE:
            r = a * cos_s + _swap_halves(a, SWA_HEAD_DIM // 4) * sin_s
            if s * LANE < PR_SK:
                r = r * (SWA_SCALE * LOG2E)
        else:
            r = a * cm_ref[...] + _swap_halves(a, MLA_ROPE // 4) * sm_ref[...]
        o_ref[0, :, s * LANE:(s + 1) * LANE] = r.astype(BF16)


def _proj_call(xa, mod, nw, w, tabs, *, t_lat, bn, rope, name):
    n_batch, s_len, _ = xa.shape
    tr = _row_tile(s_len)
    width = w.shape[1]
    grid = (width // bn, n_batch, s_len // tr)
    in_specs = [
        pl.BlockSpec((1, tr, D_MODEL), lambda j, b, i: (b, i, 0)),
        pl.BlockSpec(mod.shape, lambda j, b, i: (0, 0)),
        pl.BlockSpec((1, D_MODEL), lambda j, b, i: (0, 0)),
        pl.BlockSpec((D_MODEL, bn), lambda j, b, i: (0, j)),
    ]
    args = [xa, mod, nw, w]
    if rope:
        in_specs += [pl.BlockSpec((tr, LANE), lambda j, b, i: (i, 0))] * 4
        args += list(tabs)
    return pl.pallas_call(
        functools.partial(_proj_kernel, n_batch=n_batch, t_lat=t_lat, tr=tr, rope=rope),
        grid=grid,
        in_specs=in_specs,
        out_specs=pl.BlockSpec((1, tr, bn), lambda j, b, i: (b, i, j)),
        out_shape=jax.ShapeDtypeStruct((n_batch, s_len, width), BF16),
        compiler_params=_params(("parallel", "parallel", "parallel")),
        name=name,
    )(*args)


def _gla_decay(z_ref, aw, ab, reverse):
    gb, ch, dk = GLA_BLOCK, GLA_CHUNK, GLA_HEAD_K
    la = jnp.dot(z_ref[0], aw, preferred_element_type=F32) + ab
    la = _log_sigmoid(la) * (1.0 / GLA_GATE_NORM)

    r = lax.broadcasted_iota(jnp.int32, (gb, gb), 0)
    c = lax.broadcasted_iota(jnp.int32, (gb, gb), 1)
    same = (r // ch) == (c // ch)
    tri = (c >= r) if reverse else (c <= r)
    tmat = jnp.where(same & tri, 1.0, 0.0).astype(BF16)
    hi = la.astype(BF16)
    lo = (la - hi.astype(F32)).astype(BF16)
    hw = GLA_HEADS * dk
    res = jnp.dot(tmat, jnp.concatenate([hi, lo], axis=1), preferred_element_type=F32)
    bcum_all = res[:, :hw] + res[:, hw:]

    return bcum_all, same & tri


def _gla_kernel(qf_ref, kf_ref, vf_ref, zf_ref, qb_ref, kb_ref, vb_ref, zb_ref, aw_ref, ab_ref,
                of_ref, ob_ref, stf_ref, stb_ref):
    @pl.when(pl.program_id(1) == 0)
    def _():
        stf_ref[...] = jnp.zeros_like(stf_ref)
        stb_ref[...] = jnp.zeros_like(stb_ref)

    gb, ch, dk, dv = GLA_BLOCK, GLA_CHUNK, GLA_HEAD_K, GLA_HEAD_V
    n_ch = gb // ch
    nt = (((1,), (1,)), ((), ()))
    dirs = ((qf_ref, kf_ref, vf_ref, zf_ref, of_ref, stf_ref, False),
            (qb_ref, kb_ref, vb_ref, zb_ref, ob_ref, stb_ref, True))
    fronts = [_gla_decay(z_ref, aw_ref[d], ab_ref[d], rev) for d, (_, _, _, z_ref, _, _, rev) in enumerate(dirs)]

    chains = []
    for (q_ref, k_ref, v_ref, _, o_ref, st_ref, rev), (bcum_all, mask) in zip(dirs, fronts):
        order = range(n_ch - 1, -1, -1) if rev else range(n_ch)
        for h in range(GLA_HEADS):
            bcum = bcum_all[:, h * dk:(h + 1) * dk]
            q = q_ref[0, :, h * dk:(h + 1) * dk].astype(F32)
            k = k_ref[0, :, h * dk:(h + 1) * dk].astype(F32)
            q_dec = (q * (dk ** -0.5) * jnp.exp(bcum)).astype(BF16)
            k_inv = (k * jnp.exp(-bcum)).astype(BF16)
            v = v_ref[0, :, h * dv:(h + 1) * dv]
            a = lax.dot_general(q_dec, k_inv, nt, preferred_element_type=F32)
            tots, incs = [], []
            for cidx in order:
                lo_r, hi_r = cidx * ch, (cidx + 1) * ch
                last = lo_r if rev else hi_r - 1
                tot = bcum[last:last + 1, :]
                k_end = k[lo_r:hi_r] * jnp.exp(tot - bcum[lo_r:hi_r])
                incs.append(jnp.dot(k_end.T.astype(BF16), v[lo_r:hi_r], preferred_element_type=F32))
                tots.append(tot)
            pad = jnp.zeros((8 - n_ch, dk), F32)
            g_cols = jnp.exp(jnp.concatenate(tots + [pad], axis=0)).T
            chains.append((o_ref, st_ref, h, order, mask, q_dec, v, a, incs, g_cols))

    stage2 = []
    for o_ref, st_ref, h, order, mask, q_dec, v, a, incs, g_cols in chains:
        a = jnp.where(mask, a, 0.0).astype(BF16)
        st = st_ref[h]
        entering = []
        for i, _ in enumerate(order):
            entering.append(st.astype(BF16))
            st = st * g_cols[:, i:i + 1] + incs[i]
        st_ref[h] = st
        stage2.append((o_ref, h, order, q_dec, v, a, entering))

    for o_ref, h, order, q_dec, v, a, entering in stage2:
        o_intra = jnp.dot(a, v, preferred_element_type=F32)
        for i, cidx in enumerate(order):
            lo_r, hi_r = cidx * ch, (cidx + 1) * ch
            o = o_intra[lo_r:hi_r] + jnp.dot(q_dec[lo_r:hi_r], entering[i], preferred_element_type=F32)
            o_ref[0, lo_r:hi_r, h * dv:(h + 1) * dv] = o.astype(BF16)


def _gla_call(pa, aw, ab, *, t_lat):
    n_batch, s_len, _ = pa.shape
    gb = GLA_BLOCK
    n_lat, n_ctx = t_lat // gb, (s_len - t_lat) // gb
    nblk = n_lat + n_ctx
    qk_w, v_w = GLA_HEADS * GLA_HEAD_K, GLA_HEADS * GLA_HEAD_V

    fwd = lambda n: jnp.where(n < n_ctx, n_lat + n, n - n_ctx)
    bwd = lambda n: jnp.where(n < n_ctx, n_lat + n_ctx - 1 - n, n_lat - 1 - (n - n_ctx))

    def specs(blk):
        return [
            pl.BlockSpec((1, gb, qk_w), lambda b, n: (b, blk(n), PA_GQ // qk_w)),
            pl.BlockSpec((1, gb, qk_w), lambda b, n: (b, blk(n), PA_GK // qk_w)),
            pl.BlockSpec((1, gb, v_w), lambda b, n: (b, blk(n), PA_GV // v_w)),
            pl.BlockSpec((1, gb, LANE), lambda b, n: (b, blk(n), PA_Z // LANE)),
        ]

    out = jax.ShapeDtypeStruct((n_batch, s_len, v_w), BF16)
    state = pltpu.VMEM((GLA_HEADS, GLA_HEAD_K, GLA_HEAD_V), F32)
    return pl.pallas_call(
        _gla_kernel,
        grid=(n_batch, nblk),
        in_specs=specs(fwd) + specs(bwd) + [
            pl.BlockSpec(aw.shape, lambda b, n: (0, 0, 0)),
            pl.BlockSpec(ab.shape, lambda b, n: (0, 0, 0)),
        ],
        out_specs=[pl.BlockSpec((1, gb, v_w), lambda b, n: (b, fwd(n), 0)),
                   pl.BlockSpec((1, gb, v_w), lambda b, n: (b, bwd(n), 0))],
        out_shape=[out, out],
        scratch_shapes=[state, state],
        compiler_params=_params(("parallel", "arbitrary")),
        name="gla",
    )(pa, pa, pa, pa, pa, pa, pa, pa, aw, ab)


def _with_one_hot(v):
    one_hot = (lax.broadcasted_iota(jnp.int32, v.shape, 1) == 0).astype(v.dtype)
    return jnp.concatenate([v, one_hot], axis=1)


def _swa_kernel(*refs, local, aliased):
    if aliased:
        refs = refs[1:]
    if local:
        q_ref, kp_ref, kc_ref, kn_ref, vp_ref, vc_ref, vn_ref, kx_ref, vx_ref, sink_ref, band_ref, o_ref = refs
    else:
        q_ref, kx_ref, vx_ref, sink_ref, o_ref = refs
    nt = (((1,), (1,)), ((), ()))
    kx = kx_ref[0]
    vx = _with_one_hot(vx_ref[0])
    if local:
        n = pl.program_id(2)
        blk = SWA_BLOCK
        kb = jnp.concatenate([kp_ref[0], kc_ref[0], kn_ref[0]], axis=0)
        vb = _with_one_hot(jnp.concatenate([vp_ref[0], vc_ref[0], vn_ref[0]], axis=0))
        n_loc = kb.shape[0]
        col = lax.broadcasted_iota(jnp.int32, (1, n_loc), 1)
        edge = jnp.where((col < blk) & (n == 0), NEG_INF, 0.0)
        edge = jnp.where((col >= n_loc - blk) & (n == pl.num_programs(2) - 1), NEG_INF, edge)
        bias = band_ref[...] + edge
    heads = range(SWA_GROUP)
    qs = [q_ref[0, :, g * LANE:(g + 1) * LANE] for g in heads]
    s_ctx = [lax.dot_general(q, kx, nt, preferred_element_type=F32) for q in qs]
    s_loc = [lax.dot_general(q, kb, nt, preferred_element_type=F32) + bias for q in qs] if local else None
    stage2 = []
    for g in heads:
        sink = sink_ref[0, g:g + 1, 0:1]
        m = jnp.maximum(jnp.max(s_ctx[g], axis=1, keepdims=True), sink)
        if local:
            m = jnp.maximum(m, jnp.max(s_loc[g], axis=1, keepdims=True))
        p_ctx = jnp.exp2(s_ctx[g] - m).astype(BF16)
        p_loc = jnp.exp2(s_loc[g] - m).astype(BF16) if local else None
        stage2.append((p_ctx, p_loc, jnp.exp2(sink - m)))
    for g, (p_ctx, p_loc, p_sink) in enumerate(stage2):
        o = jnp.dot(p_ctx, vx, preferred_element_type=F32)
        if local:
            o += jnp.dot(p_loc, vb, preferred_element_type=F32)
        den = o[:, SWA_HEAD_DIM:SWA_HEAD_DIM + 1] + p_sink
        o_ref[0, :, g * LANE:(g + 1) * LANE] = (o[:, :SWA_HEAD_DIM] / den).astype(BF16)


def _swa_band_table(bq):
    r = np.arange(bq)[:, None]
    c = np.arange(bq + 2 * SWA_BLOCK)[None, :] - SWA_BLOCK
    return jnp.asarray(np.where(np.abs(r - c) <= SWA_WINDOW, 0.0, NEG_INF), F32)


def _swa_call(pr, pa, sink2, *, t_lat):
    n_batch, s_len, _ = pr.shape
    blk = SWA_BLOCK
    l_ctx = s_len - t_lat
    bq = SWA_QTILE if t_lat % SWA_QTILE == 0 else blk
    per = bq // blk
    n_blk = t_lat // blk
    ctx_blk = t_lat // l_ctx
    gw = SWA_GROUP * LANE
    kcol = lambda kh: PR_SK // LANE + kh
    vcol = lambda kh: PA_SV // LANE + kh
    prev = lambda n: jnp.maximum(n * per - 1, 0)
    nxt = lambda n: jnp.minimum((n + 1) * per, n_blk - 1)
    edge = lambda col, pos: pl.BlockSpec((1, blk, LANE), lambda b, kh, n: (b, pos(n), col(kh)))
    body = lambda col: pl.BlockSpec((1, bq, LANE), lambda b, kh, n: (b, n, col(kh)))
    ctx = lambda col: pl.BlockSpec((1, l_ctx, LANE), lambda b, kh, n: (b, ctx_blk, col(kh)))
    sink_spec = pl.BlockSpec((1, SWA_GROUP, LANE), lambda b, kh, n: (kh, 0, 0))
    out_shape = jax.ShapeDtypeStruct((n_batch, s_len, SWA_HEADS * SWA_HEAD_DIM), BF16)
    band = _swa_band_table(bq)
    o_lat = pl.pallas_call(
        functools.partial(_swa_kernel, local=True, aliased=False),
        grid=(n_batch, SWA_KV_HEADS, t_lat // bq),
        in_specs=[
            pl.BlockSpec((1, bq, gw), lambda b, kh, n: (b, n, PR_SQ // gw + kh)),
            edge(kcol, prev), body(kcol), edge(kcol, nxt),
            edge(vcol, prev), body(vcol), edge(vcol, nxt),
            ctx(kcol), ctx(vcol), sink_spec,
            pl.BlockSpec(band.shape, lambda b, kh, n: (0, 0)),
        ],
        out_specs=pl.BlockSpec((1, bq, gw), lambda b, kh, n: (b, n, kh)),
        out_shape=out_shape,
        compiler_params=_params(("parallel", "parallel", "parallel")),
        name="swa",
    )(pr, pr, pr, pr, pa, pa, pa, pr, pa, sink2, band)
    return pl.pallas_call(
        functools.partial(_swa_kernel, local=False, aliased=True),
        grid=(n_batch, SWA_KV_HEADS, 1),
        in_specs=[
            pl.BlockSpec(memory_space=pl.ANY),
            pl.BlockSpec((1, l_ctx, gw), lambda b, kh, n: (b, ctx_blk, PR_SQ // gw + kh)),
            ctx(kcol), ctx(vcol), sink_spec,
        ],
        out_specs=pl.BlockSpec((1, l_ctx, gw), lambda b, kh, n: (b, ctx_blk, kh)),
        out_shape=out_shape,
        input_output_aliases={0: 0},
        compiler_params=_params(("parallel", "parallel", "parallel")),
        name="swa_ctx",
    )(o_lat, pr, pr, pa, sink2)


def _mla_proj_kernel(cq_ref, ckv_ref, kpe_ref, qnw_ref, kvnw_ref, wq_ref, wkv_ref, cm_ref, sm_ref,
                     q_ref, k_ref, v_ref):
    cq = cq_ref[0].astype(F32)
    ms = jnp.sum(cq * cq, axis=-1, keepdims=True) * (1.0 / MLA_Q_RANK)
    cqn = (cq * lax.rsqrt(ms + EPS) * qnw_ref[...]).astype(BF16)
    q = jnp.dot(cqn, wq_ref[...], preferred_element_type=F32)
    cos_m, sin_m = cm_ref[...], sm_ref[...]
    qs = MLA_SCALE * LOG2E
    hw = 2 * LANE
    for h in range(MLA_HEADS):
        q_ref[0, :, h * hw:h * hw + LANE] = (q[:, h * hw:h * hw + LANE] * qs).astype(BF16)
        pe = q[:, h * hw + LANE:(h + 1) * hw]
        pe = (pe * cos_m + _swap_halves(pe, MLA_ROPE // 4) * sin_m) * qs
        q_ref[0, :, h * hw + LANE:(h + 1) * hw] = pe.astype(BF16)
    ckv = _rms(ckv_ref[0].astype(F32), kvnw_ref[...]).astype(BF16)
    kv = jnp.dot(ckv, wkv_ref[...], preferred_element_type=F32)
    half = MLA_HEADS * MLA_NOPE
    kpe = kpe_ref[0]
    one_hot = (lax.broadcasted_iota(jnp.int32, kpe.shape, 1) == 0).astype(BF16)
    for h in range(MLA_HEADS):
        k_ref[0, :, h * hw:h * hw + LANE] = kv[:, h * LANE:(h + 1) * LANE].astype(BF16)
        k_ref[0, :, h * hw + LANE:(h + 1) * hw] = kpe
        v_ref[0, :, h * hw:h * hw + LANE] = kv[:, half + h * LANE:half + (h + 1) * LANE].astype(BF16)
        v_ref[0, :, h * hw + LANE:(h + 1) * hw] = one_hot


def _mla_proj_call(pa, pr, qnw, kvnw, wq, wkv, cos_m, sin_m):
    n_batch, s_len, _ = pa.shape
    tr = _row_tile(s_len)
    qw = MLA_HEADS * 2 * LANE
    kw = MLA_HEADS * MLA_NOPE
    const = lambda shape: pl.BlockSpec(shape, lambda b, i: (0,) * len(shape))
    wide = pl.BlockSpec((1, tr, qw), lambda b, i: (b, i, 0))
    return pl.pallas_call(
        _mla_proj_kernel,
        grid=(n_batch, s_len // tr),
        in_specs=[
            pl.BlockSpec((1, tr, MCQ_PAD), lambda b, i: (b, i, PA_MCQ // MCQ_PAD)),
            pl.BlockSpec((1, tr, MLA_KV_RANK), lambda b, i: (b, i, PA_MCKV // MLA_KV_RANK)),
            pl.BlockSpec((1, tr, LANE), lambda b, i: (b, i, PR_KPE // LANE)),
            const((1, MCQ_PAD)), const((1, MLA_KV_RANK)),
            const((MCQ_PAD, qw)), const((MLA_KV_RANK, 2 * kw)),
            pl.BlockSpec((tr, LANE), lambda b, i: (i, 0)),
            pl.BlockSpec((tr, LANE), lambda b, i: (i, 0)),
        ],
        out_specs=[wide, wide, wide],
        out_shape=[jax.ShapeDtypeStruct((n_batch, s_len, qw), BF16)] * 3,
        compiler_params=_params(("parallel", "parallel")),
        name="mla_proj",
    )(pa, pa, pr, qnw, kvnw, wq, wkv, cos_m, sin_m)


def _flash_kernel(*refs, aliased, sub, kv_sub):
    if aliased:
        refs = refs[1:]
    q_ref, k_ref, v_ref, o_ref, m_ref, acc_ref = refs
    j = pl.program_id(3)

    @pl.when(j == 0)
    def _():
        m_ref[...] = jnp.full_like(m_ref, NEG_INF)
        acc_ref[...] = jnp.zeros_like(acc_ref)

    bq, bk = q_ref.shape[1], k_ref.shape[1]
    sub = min(sub, bq)
    kv_sub = min(kv_sub, bk)
    chains = [slice(c * sub, (c + 1) * sub) for c in range(bq // sub)]
    for t in range(bk // kv_sub):
        k = k_ref[0, t * kv_sub:(t + 1) * kv_sub, :]
        v = v_ref[0, t * kv_sub:(t + 1) * kv_sub, :]
        scores = [lax.dot_general(q_ref[0, rows, :], k, (((1,), (1,)), ((), ())), preferred_element_type=F32)
                  for rows in chains]
        probs, alphas = [], []
        for rows, s in zip(chains, scores):
            m_prev = m_ref[rows, :]
            m_new = jnp.maximum(m_prev, jnp.max(s, axis=1, keepdims=True))
            alphas.append(jnp.exp2(m_prev - m_new))
            probs.append(jnp.exp2(s - m_new).astype(BF16))
            m_ref[rows, :] = m_new
        for rows, p, alpha in zip(chains, probs, alphas):
            acc_ref[rows, :] = alpha * acc_ref[rows, :] + jnp.dot(p, v, preferred_element_type=F32)

    @pl.when(j == pl.num_programs(3) - 1)
    def _():
        acc = acc_ref[...]
        o_ref[0] = (acc[:, :MLA_V] / acc[:, MLA_V:MLA_V + 1]).astype(BF16)


def _flash_call(q, k, v, prev_out, *, bq, bk, kv_sub, q_blk0, n_q, kv_blk0, n_kv, name):
    n_batch, s_len, _ = q.shape
    aliased = prev_out is not None
    hw = 2 * LANE
    kv_mode = dict(pipeline_mode=pl.Buffered(1)) if n_kv == 1 else {}
    in_specs = [
        pl.BlockSpec((1, bq, hw), lambda b, h, i, j: (b, q_blk0 + i, h)),
        pl.BlockSpec((1, bk, hw), lambda b, h, i, j: (b, kv_blk0 + j, h), **kv_mode),
        pl.BlockSpec((1, bk, hw), lambda b, h, i, j: (b, kv_blk0 + j, h), **kv_mode),
    ]
    args = [q, k, v]
    if aliased:
        in_specs = [pl.BlockSpec(memory_space=pl.ANY)] + in_specs
        args = [prev_out] + args
    return pl.pallas_call(
        functools.partial(_flash_kernel, aliased=aliased, sub=FLASH_SUB, kv_sub=kv_sub),
        grid=(n_batch, MLA_HEADS, n_q, n_kv),
        in_specs=in_specs,
        out_specs=pl.BlockSpec((1, bq, LANE), lambda b, h, i, j: (b, q_blk0 + i, h)),
        out_shape=jax.ShapeDtypeStruct((n_batch, s_len, MLA_HEADS * MLA_V), BF16),
        scratch_shapes=[pltpu.VMEM((bq, 1), F32), pltpu.VMEM((bq, hw), F32)],
        input_output_aliases={0: 0} if aliased else {},
        compiler_params=_params(("parallel", "parallel", "parallel", "arbitrary")),
        name=name,
    )(*args)


def _kv_block(s_len):
    for bk in (3328, 1280, 640, 256, 128):
        if s_len % bk == 0:
            return bk
    raise ValueError(f"unsupported stream length {s_len}")


def _merge_kernel(x_ref, mod_ref, g_ref, gr_ref, of_ref, ob_ref, osw_ref, om_ref, gw_ref, wb_ref, wo_ref,
                  o_ref, *, n_batch, t_lat, tr):
    b = pl.program_id(0)
    row0 = pl.program_id(1) * tr
    og = of_ref[0].astype(F32) + ob_ref[0].astype(F32)
    gw = gw_ref[...]
    heads = []
    for h in range(GLA_HEADS):
        heads.append(_rms(og[:, h * GLA_HEAD_V:(h + 1) * GLA_HEAD_V], gw))
    gr = gr_ref[0].astype(F32)
    o_gla = (jnp.concatenate(heads, axis=1) * (gr * jax.nn.sigmoid(gr))).astype(BF16)
    branches = (o_gla, osw_ref[0], om_ref[0])
    merged = None
    for idx, ob in enumerate(branches):
        gate = jax.nn.sigmoid(g_ref[0, :, idx * D_MODEL:(idx + 1) * D_MODEL].astype(F32))
        term = gate * jnp.dot(ob, wb_ref[idx], preferred_element_type=F32)
        merged = term if merged is None else merged + term
    y = jnp.dot(merged.astype(BF16), wo_ref[...], preferred_element_type=F32)
    g_m = _mod_rows(mod_ref, b, n_batch, row0, tr, t_lat, 2)
    o_ref[0] = x_ref[0] + g_m * y


def _merge_call(xa, mod, pa, o_f, o_b, o_swa, o_mla, gw, wb, wo, *, t_lat):
    n_batch, s_len, _ = xa.shape
    tr = 320 if s_len % 320 == 0 else _row_tile(s_len)
    row = lambda width, col: pl.BlockSpec((1, tr, width), lambda b, i: (b, i, col))
    return pl.pallas_call(
        functools.partial(_merge_kernel, n_batch=n_batch, t_lat=t_lat, tr=tr),
        grid=(n_batch, s_len // tr),
        in_specs=[
            row(D_MODEL, 0),
            pl.BlockSpec(mod.shape, lambda b, i: (0, 0)),
            row(3 * D_MODEL, PA_GATES // (3 * D_MODEL)),
            row(D_MODEL, PA_GR // D_MODEL),
            row(D_MODEL, 0), row(D_MODEL, 0), row(D_MODEL, 0), row(D_MODEL, 0),
            pl.BlockSpec((1, GLA_HEAD_V), lambda b, i: (0, 0)),
            pl.BlockSpec((3, D_MODEL, D_MODEL), lambda b, i: (0, 0, 0)),
            pl.BlockSpec((D_MODEL, D_MODEL), lambda b, i: (0, 0)),
        ],
        out_specs=row(D_MODEL, 0),
        out_shape=jax.ShapeDtypeStruct(xa.shape, F32),
        compiler_params=_params(("parallel", "parallel")),
        name="merge",
    )(xa, mod, pa, pa, o_f, o_b, o_swa, o_mla, gw, wb, wo)


def _mlp_kernel(x_ref, mod_ref, nw_ref, w1_ref, w2_ref, fw_ref, o_ref, h_ref, acc_ref,
                *, n_batch, t_lat, tr, final):
    b = pl.program_id(0)
    row0 = pl.program_id(1) * tr
    c = pl.program_id(2)

    @pl.when(c == 0)
    def _():
        shift = _mod_rows(mod_ref, b, n_batch, row0, tr, t_lat, 3)
        scale = _mod_rows(mod_ref, b, n_batch, row0, tr, t_lat, 4)
        h_ref[...] = (_rms(x_ref[0], nw_ref[...]) * (1.0 + scale) + shift).astype(BF16)
        acc_ref[...] = jnp.zeros_like(acc_ref)

    halves = [slice(0, tr // 2), slice(tr // 2, tr)]
    ups = [jnp.maximum(jnp.dot(h_ref[rows, :], w1_ref[...], preferred_element_type=F32), 0.0) for rows in halves]
    for rows, u in zip(halves, ups):
        acc_ref[rows, :] += jnp.dot((u * u).astype(BF16), w2_ref[...], preferred_element_type=F32)

    @pl.when(c == pl.num_programs(2) - 1)
    def _():
        g_f = _mod_rows(mod_ref, b, n_batch, row0, tr, t_lat, 5)
        y = x_ref[0] + g_f * acc_ref[...]
        if final:
            y = _rms(y, fw_ref[...])
        o_ref[0] = y


def _mlp_call(xa, mod, nw, w1, w2, fw, *, t_lat, final):
    n_batch, s_len, _ = xa.shape
    rows = t_lat if final else s_len
    tr = _row_tile(rows)
    return pl.pallas_call(
        functools.partial(_mlp_kernel, n_batch=n_batch, t_lat=t_lat, tr=tr, final=final),
        grid=(n_batch, rows // tr, FF_DIM // FF_CHUNK),
        in_specs=[
            pl.BlockSpec((1, tr, D_MODEL), lambda b, i, c: (b, i, 0)),
            pl.BlockSpec(mod.shape, lambda b, i, c: (0, 0)),
            pl.BlockSpec((1, D_MODEL), lambda b, i, c: (0, 0)),
            pl.BlockSpec((D_MODEL, FF_CHUNK), lambda b, i, c: (0, c)),
            pl.BlockSpec((FF_CHUNK, D_MODEL), lambda b, i, c: (c, 0)),
            pl.BlockSpec((1, D_MODEL), lambda b, i, c: (0, 0)),
        ],
        out_specs=pl.BlockSpec((1, tr, D_MODEL), lambda b, i, c: (b, i, 0)),
        out_shape=jax.ShapeDtypeStruct((n_batch, rows, D_MODEL), F32),
        scratch_shapes=[pltpu.VMEM((tr, D_MODEL), BF16), pltpu.VMEM((tr, D_MODEL), F32)],
        compiler_params=_params(("parallel", "parallel", "arbitrary")),
        name="mlp",
    )(xa, mod, nw, w1, w2, fw)


def _rope_tables(t_lat, l_ctx, dim):
    f32 = np.float32
    pos = np.arange(t_lat)
    row, col = pos // GRID_W, pos % GRID_W
    d_axis = dim // 2
    inv = (f32(ROPE_BASE) ** (-np.arange(0, d_axis, 2, dtype=f32) / f32(d_axis))).astype(f32)
    ang_r = row.astype(f32)[:, None] * inv
    ang_c = col.astype(f32)[:, None] * inv
    cos = np.concatenate([np.cos(ang_r)] * 2 + [np.cos(ang_c)] * 2, axis=1)
    sin = np.concatenate([-np.sin(ang_r), np.sin(ang_r), -np.sin(ang_c), np.sin(ang_c)], axis=1)
    if dim < LANE:
        cos = np.concatenate([cos, np.ones((t_lat, LANE - dim), f32)], axis=1)
        sin = np.concatenate([sin, np.zeros((t_lat, LANE - dim), f32)], axis=1)
    cos = np.concatenate([cos, np.ones((l_ctx, LANE), f32)], axis=0)
    sin = np.concatenate([sin, np.zeros((l_ctx, LANE), f32)], axis=0)
    return jnp.asarray(cos, F32), jnp.asarray(sin, F32)


def _split_w_in(w):
    sizes = (512, 512, 1024, 1024, GLA_RANK, GLA_RANK, 1024, 256, 256, MLA_Q_RANK, MLA_KV_RANK, MLA_ROPE,
             3 * D_MODEL)
    offs = np.cumsum((0,) + sizes)
    return [w[:, offs[i]:offs[i + 1]] for i in range(len(sizes))]


def _layer_weights(w_in, gla_a_w, gla_a_b, mla_q_norm_w, mla_w_uq, mla_kv_norm_w, mla_w_ukv, swa_sink):
    gq, gk, gv, gr, gzf, gzb, sq, sk, sv, mcq, mckv, mkr, gates = _split_w_in(w_in)
    zpad = lambda n: jnp.zeros((D_MODEL, n), F32)
    w_plain = jnp.concatenate(
        [gates, gr, gv, gq, gk, mcq, zpad(MCQ_PAD - MLA_Q_RANK), sv, mckv, gzf, gzb, zpad(LANE - 2 * GLA_RANK)],
        axis=1).astype(BF16)
    w_rope = jnp.concatenate([sq, sk, mkr, zpad(LANE - MLA_ROPE)], axis=1).astype(BF16)
    aw = jnp.zeros((2, LANE, GLA_HEADS * GLA_HEAD_K), F32)
    aw = aw.at[0, :GLA_RANK].set(gla_a_w[0]).at[1, GLA_RANK:2 * GLA_RANK].set(gla_a_w[1])
    aw = aw.astype(BF16)
    ab = gla_a_b.reshape(2, 1, GLA_HEADS * GLA_HEAD_K)
    wq = mla_w_uq.reshape(MLA_Q_RANK, MLA_HEADS, MLA_NOPE + MLA_ROPE)
    wq = jnp.pad(wq, ((0, MCQ_PAD - MLA_Q_RANK), (0, 0), (0, 2 * LANE - MLA_NOPE - MLA_ROPE)))
    wq = wq.reshape(MCQ_PAD, MLA_HEADS * 2 * LANE).astype(BF16)
    qnw = jnp.pad(mla_q_norm_w, (0, MCQ_PAD - MLA_Q_RANK)).reshape(1, MCQ_PAD)
    wkv = mla_w_ukv.reshape(MLA_KV_RANK, MLA_HEADS, MLA_NOPE + MLA_V)
    wkv = jnp.concatenate([wkv[:, :, :MLA_NOPE].reshape(MLA_KV_RANK, -1),
                           wkv[:, :, MLA_NOPE:].reshape(MLA_KV_RANK, -1)], axis=1).astype(BF16)
    kvnw = mla_kv_norm_w.reshape(1, MLA_KV_RANK)
    sink2 = jnp.broadcast_to((swa_sink * LOG2E).reshape(SWA_KV_HEADS, SWA_GROUP, 1),
                             (SWA_KV_HEADS, SWA_GROUP, LANE))
    return w_plain, w_rope, aw, ab, wq, qnw, wkv, kvnw, sink2


def kernel(x, c, ctx, c_ctx, ada_w, ada_b, norm_mix_w, w_in, gla_a_w, gla_a_b, gla_norm_w, swa_sink,
           mla_q_norm_w, mla_w_uq, mla_kv_norm_w, mla_w_ukv, w_branch, w_out, norm_mlp_w, mlp_w1, mlp_w2,
           final_norm_w):
    n_batch, t_lat, d_model = x.shape
    l_ctx = ctx.shape[1]
    depth = ada_w.shape[0]
    s_len = t_lat + l_ctx
    assert d_model == D_MODEL and n_batch + 1 <= MOD_ROWS
    assert t_lat % GLA_BLOCK == 0 and l_ctx % GLA_BLOCK == 0 and t_lat % l_ctx == 0 and t_lat % GRID_W == 0

    xa = jnp.concatenate([x, ctx], axis=1)
    cin = jnp.zeros((MOD_ROWS, D_MODEL), F32).at[:n_batch].set(c).at[n_batch].set(c_ctx)
    mod_all = _ada_call(cin, ada_w, ada_b)

    cos_s, sin_s = _rope_tables(t_lat, l_ctx, SWA_HEAD_DIM)
    cos_m, sin_m = _rope_tables(t_lat, l_ctx, MLA_ROPE)
    tabs = (cos_s, sin_s, cos_m, sin_m)

    bq = min(1024, t_lat)
    bk = _kv_block(s_len)
    row_vec = lambda v: v.reshape(1, -1)

    for layer in range(depth):
        mod = mod_all[layer]
        (w_plain, w_rope, aw, ab, wq, qnw, wkv, kvnw, sink2) = _layer_weights(
            w_in[layer], gla_a_w[layer], gla_a_b[layer], mla_q_norm_w[layer], mla_w_uq[layer],
            mla_kv_norm_w[layer], mla_w_ukv[layer], swa_sink[layer])
        nw = row_vec(norm_mix_w[layer])
        pa = _proj_call(xa, mod, nw, w_plain, None, t_lat=t_lat, bn=PA_BN, rope=False, name="proj_plain")
        pr = _proj_call(xa, mod, nw, w_rope, tabs, t_lat=t_lat, bn=PR_WIDTH, rope=True, name="proj_rope")

        o_f, o_b = _gla_call(pa, aw, ab, t_lat=t_lat)
        o_swa = _swa_call(pr, pa, sink2, t_lat=t_lat)

        q, k, v = _mla_proj_call(pa, pr, qnw, kvnw, wq, wkv, cos_m, sin_m)
        o_mla = _flash_call(q, k, v, None, bq=bq, bk=s_len, kv_sub=bk, q_blk0=0, n_q=t_lat // bq, kv_blk0=0,
                            n_kv=1, name="mla_flash")
        o_mla = _flash_call(q, k, v, o_mla, bq=l_ctx, bk=l_ctx, kv_sub=l_ctx, q_blk0=t_lat // l_ctx, n_q=1,
                            kv_blk0=t_lat // l_ctx, n_kv=1, name="mla_flash_ctx")

        xa = _merge_call(xa, mod, pa, o_f, o_b, o_swa, o_mla, row_vec(gla_norm_w[layer]),
                         w_branch[layer].astype(BF16), w_out[layer].astype(BF16), t_lat=t_lat)
        xa = _mlp_call(xa, mod, row_vec(norm_mlp_w[layer]), mlp_w1[layer].astype(BF16),
                       mlp_w2[layer].astype(BF16), row_vec(final_norm_w), t_lat=t_lat,
                       final=layer == depth - 1)
    return xa
```

```python
import functools

import jax
import jax.numpy as jnp
import numpy as np
from jax import lax
from jax.experimental import pallas as pl
from jax.experimental.pallas import tpu as pltpu

F32 = jnp.float32
BF16 = jnp.bfloat16

D_MODEL = 1024
EPS = 1e-6
ROPE_BASE = 10000.0
NEG_INF = -1e30
LOG2E = 1.4426950408889634
GRID_W = 64

GLA_HEADS = 4
GLA_HEAD_K = 128
GLA_HEAD_V = 256
GLA_RANK = 16
GLA_GATE_NORM = 16.0
GLA_CHUNK = 64
GLA_BLOCK = 256

SWA_HEADS = 8
SWA_KV_HEADS = 2
SWA_GROUP = 4
SWA_HEAD_DIM = 128
SWA_WINDOW = 128
SWA_BLOCK = 128
SWA_SCALE = SWA_HEAD_DIM ** -0.5
SWA_QTILE = 512

MLA_HEADS = 8
MLA_Q_RANK = 384
MLA_KV_RANK = 256
MLA_NOPE = 128
MLA_ROPE = 64
MLA_V = 128
MLA_SCALE = (MLA_NOPE + MLA_ROPE) ** -0.5

FF_DIM = 4 * D_MODEL
FF_CHUNK = 2048
FLASH_SUB = 256

LANE = 128
MOD_ROWS = 8
VMEM_LIMIT = 56 * 1024 * 1024

PA_GATES, PA_GR, PA_GV, PA_GQ, PA_GK = 0, 3072, 4096, 5120, 5632
PA_MCQ, PA_SV, PA_MCKV, PA_Z = 6144, 6656, 6912, 7168
PA_WIDTH = 7296
PA_BN = 2432
MCQ_PAD = 512
PR_SQ, PR_SK, PR_KPE = 0, 1024, 1280
PR_WIDTH = 1408
PR_SLABS = PR_WIDTH // LANE


def _params(sem):
    return pltpu.CompilerParams(dimension_semantics=sem, vmem_limit_bytes=VMEM_LIMIT)


def _layer_spec(layer, block, index_map):
    return pl.BlockSpec((None,) + tuple(block), lambda *g: (layer,) + tuple(index_map(*g)))


def _row_tile(s):
    for tr in (640, 512, 256, 128):
        if s % tr == 0:
            return tr
    raise ValueError(f"unsupported stream length {s}")


def _rms(x, w):
    return x * lax.rsqrt(jnp.mean(x * x, axis=-1, keepdims=True) + EPS) * w


def _mod_rows(mod_ref, b, n_batch, row0, rows, t_lat, idx):
    lo = idx * D_MODEL
    lat = mod_ref[pl.ds(b, 1), lo:lo + D_MODEL]
    ctx = mod_ref[n_batch:n_batch + 1, lo:lo + D_MODEL]
    rid = row0 + lax.broadcasted_iota(jnp.int32, (rows, 1), 0)
    return jnp.where(rid >= t_lat, ctx, lat)


def _swap_halves(x, half):
    lane = lax.broadcasted_iota(jnp.int32, x.shape, 1)
    return jnp.where((lane & half) == 0, pltpu.roll(x, LANE - half, 1), pltpu.roll(x, half, 1))


def _log_sigmoid(x):
    return jnp.minimum(x, 0.0) - jnp.log1p(jnp.exp(-jnp.abs(x)))


def _ada_kernel(c_ref, w_ref, b_ref, o_ref):
    c = c_ref[...]
    a = (c * jax.nn.sigmoid(c)).astype(BF16)
    o_ref[0] = jnp.dot(a, w_ref[0].astype(BF16), preferred_element_type=F32) + b_ref[0]


def _ada_call(cin, ada_w, ada_b):
    depth = ada_w.shape[0]
    nblk = ada_w.shape[2] // D_MODEL
    return pl.pallas_call(
        _ada_kernel,
        grid=(depth, nblk),
        in_specs=[
            pl.BlockSpec((MOD_ROWS, D_MODEL), lambda l, j: (0, 0)),
            pl.BlockSpec((1, D_MODEL, D_MODEL), lambda l, j: (l, 0, j)),
            pl.BlockSpec((1, 1, D_MODEL), lambda l, j: (l, 0, j)),
        ],
        out_specs=pl.BlockSpec((1, MOD_ROWS, D_MODEL), lambda l, j: (l, 0, j)),
        out_shape=jax.ShapeDtypeStruct((depth, MOD_ROWS, ada_w.shape[2]), F32),
        compiler_params=_params(("parallel", "parallel")),
        name="ada_mod",
    )(cin, ada_w, ada_b.reshape(depth, 1, -1))


def _proj_kernel(x_ref, mod_ref, nw_ref, w_ref, *rest, n_batch, t_lat, tr, rope):
    o_ref = rest[-1]
    b = pl.program_id(1)
    row0 = pl.program_id(2) * tr
    x = x_ref[0]
    shift = _mod_rows(mod_ref, b, n_batch, row0, tr, t_lat, 0)
    scale = _mod_rows(mod_ref, b, n_batch, row0, tr, t_lat, 1)
    h = _rms(x, nw_ref[...]) * (1.0 + scale) + shift
    acc = jnp.dot(h.astype(BF16), w_ref[...], preferred_element_type=F32)
    if not rope:
        o_ref[0] = acc.astype(BF16)
        return
    cs_ref, sn_ref, cm_ref, sm_ref = rest[:4]
    cos_s, sin_s = cs_ref[...], sn_ref[...]
    for s in range(PR_SLABS):
        a = acc[:, s * LANE:(s + 1) * LANE]
        if s * LANE < PR_KPE:
            r = a * cos_s + _swap_halves(a, SWA_HEAD_DIM // 4) * sin_s
            if s * LANE < PR_SK:
                r = r * (SWA_SCALE * LOG2E)
        else:
            r = a * cm_ref[...] + _swap_halves(a, MLA_ROPE // 4) * sm_ref[...]
        o_ref[0, :, s * LANE:(s + 1) * LANE] = r.astype(BF16)


def _proj_call(xa, mod, nw, w, tabs, *, layer, t_lat, bn, rope, name):
    n_batch, s_len, _ = xa.shape
    tr = _row_tile(s_len)
    width = w.shape[-1]
    grid = (width // bn, n_batch, s_len // tr)
    in_specs = [
        pl.BlockSpec((1, tr, D_MODEL), lambda j, b, i: (b, i, 0)),
        _layer_spec(layer, mod.shape[1:], lambda j, b, i: (0, 0)),
        _layer_spec(layer, (1, D_MODEL), lambda j, b, i: (0, 0)),
        _layer_spec(layer, (D_MODEL, bn), lambda j, b, i: (0, j)),
    ]
    args = [xa, mod, nw, w]
    if rope:
        in_specs += [pl.BlockSpec((tr, LANE), lambda j, b, i: (i, 0))] * 4
        args += list(tabs)
    return pl.pallas_call(
        functools.partial(_proj_kernel, n_batch=n_batch, t_lat=t_lat, tr=tr, rope=rope),
        grid=grid,
        in_specs=in_specs,
        out_specs=pl.BlockSpec((1, tr, bn), lambda j, b, i: (b, i, j)),
        out_shape=jax.ShapeDtypeStruct((n_batch, s_len, width), BF16),
        compiler_params=_params(("parallel", "parallel", "parallel")),
        name=name,
    )(*args)


def _gla_mask(reverse):
    gb, ch = GLA_BLOCK, GLA_CHUNK
    r = lax.broadcasted_iota(jnp.int32, (gb, gb), 0)
    c = lax.broadcasted_iota(jnp.int32, (gb, gb), 1)
    return ((r // ch) == (c // ch)) & ((c >= r) if reverse else (c <= r))


def _gla_log_decay(z_ref, aw, ab):
    la = jnp.dot(z_ref[0], aw, preferred_element_type=F32) + ab
    la = _log_sigmoid(la) * (1.0 / GLA_GATE_NORM)
    hi = la.astype(BF16)
    lo = (la - hi.astype(F32)).astype(BF16)
    return jnp.concatenate([hi, lo], axis=1)


def _gla_cumulate(hi_lo, reverse):
    tmat = jnp.where(_gla_mask(reverse), 1.0, 0.0).astype(BF16)
    hw = GLA_HEADS * GLA_HEAD_K
    res = jnp.dot(tmat, hi_lo, preferred_element_type=F32)
    return res[:, :hw] + res[:, hw:]


def _gla_decay(z_ref, aw, ab, reverse):
    return _gla_cumulate(_gla_log_decay(z_ref, aw, ab), reverse)


def _gla_kernel(qf_ref, kf_ref, vf_ref, zf_ref, zfn_ref, qb_ref, kb_ref, vb_ref, zb_ref, zbn_ref, aw_ref, ab_ref,
                of_ref, ob_ref, stf_ref, stb_ref, bc_ref):
    n = pl.program_id(1)

    @pl.when(n == 0)
    def _():
        stf_ref[...] = jnp.zeros_like(stf_ref)
        stb_ref[...] = jnp.zeros_like(stb_ref)
        bc_ref[0] = _gla_decay(zf_ref, aw_ref[0], ab_ref[0], False)
        bc_ref[1] = _gla_decay(zb_ref, aw_ref[1], ab_ref[1], True)

    gb, ch, dk, dv = GLA_BLOCK, GLA_CHUNK, GLA_HEAD_K, GLA_HEAD_V
    n_ch = gb // ch
    nt = (((1,), (1,)), ((), ()))
    dirs = ((qf_ref, kf_ref, vf_ref, zfn_ref, of_ref, stf_ref, False),
            (qb_ref, kb_ref, vb_ref, zbn_ref, ob_ref, stb_ref, True))
    fronts = [(bc_ref[d], _gla_mask(rev)) for d, (_, _, _, _, _, _, rev) in enumerate(dirs)]
    upcoming = [_gla_log_decay(z_next_ref, aw_ref[d], ab_ref[d]) for d, (_, _, _, z_next_ref, _, _, _) in enumerate(dirs)]

    chains = []
    for (q_ref, k_ref, v_ref, _, o_ref, st_ref, rev), (bcum_all, mask) in zip(dirs, fronts):
        order = range(n_ch - 1, -1, -1) if rev else range(n_ch)
        for h in range(GLA_HEADS):
            bcum = bcum_all[:, h * dk:(h + 1) * dk]
            q = q_ref[0, :, h * dk:(h + 1) * dk].astype(F32)
            k = k_ref[0, :, h * dk:(h + 1) * dk].astype(F32)
            q_dec = (q * (dk ** -0.5) * jnp.exp(bcum)).astype(BF16)
            k_inv = (k * jnp.exp(-bcum)).astype(BF16)
            v = v_ref[0, :, h * dv:(h + 1) * dv]
            a = lax.dot_general(q_dec, k_inv, nt, preferred_element_type=F32)
            tots, incs = [], []
            for cidx in order:
                lo_r, hi_r = cidx * ch, (cidx + 1) * ch
                last = lo_r if rev else hi_r - 1
                tot = bcum[last:last + 1, :]
                k_end = k[lo_r:hi_r] * jnp.exp(tot - bcum[lo_r:hi_r])
                incs.append(jnp.dot(k_end.T.astype(BF16), v[lo_r:hi_r], preferred_element_type=F32))
                tots.append(tot)
            pad = jnp.zeros((8 - n_ch, dk), F32)
            g_cols = jnp.exp(jnp.concatenate(tots + [pad], axis=0)).T
            chains.append((o_ref, st_ref, h, order, mask, q_dec, v, a, incs, g_cols))

    upcoming = [_gla_cumulate(hi_lo, rev) for hi_lo, (_, _, _, _, _, _, rev) in zip(upcoming, dirs)]

    stage2 = []
    for o_ref, st_ref, h, order, mask, q_dec, v, a, incs, g_cols in chains:
        a = jnp.where(mask, a, 0.0).astype(BF16)
        st = st_ref[h]
        entering = []
        for i, _ in enumerate(order):
            entering.append(st.astype(BF16))
            st = st * g_cols[:, i:i + 1] + incs[i]
        st_ref[h] = st
        stage2.append((o_ref, h, order, q_dec, v, a, entering))

    for o_ref, h, order, q_dec, v, a, entering in stage2:
        o_intra = jnp.dot(a, v, preferred_element_type=F32)
        for i, cidx in enumerate(order):
            lo_r, hi_r = cidx * ch, (cidx + 1) * ch
            o = o_intra[lo_r:hi_r] + jnp.dot(q_dec[lo_r:hi_r], entering[i], preferred_element_type=F32)
            o_ref[0, lo_r:hi_r, h * dv:(h + 1) * dv] = o.astype(BF16)

    for d, nxt in enumerate(upcoming):
        bc_ref[d] = nxt


def _gla_call(pa, aw, ab, *, layer, t_lat):
    n_batch, s_len, _ = pa.shape
    gb = GLA_BLOCK
    n_lat, n_ctx = t_lat // gb, (s_len - t_lat) // gb
    nblk = n_lat + n_ctx
    qk_w, v_w = GLA_HEADS * GLA_HEAD_K, GLA_HEADS * GLA_HEAD_V

    fwd = lambda n: jnp.where(n < n_ctx, n_lat + n, n - n_ctx)
    bwd = lambda n: jnp.where(n < n_ctx, n_lat + n_ctx - 1 - n, n_lat - 1 - (n - n_ctx))

    def specs(blk):
        return [
            pl.BlockSpec((1, gb, qk_w), lambda b, n: (b, blk(n), PA_GQ // qk_w)),
            pl.BlockSpec((1, gb, qk_w), lambda b, n: (b, blk(n), PA_GK // qk_w)),
            pl.BlockSpec((1, gb, v_w), lambda b, n: (b, blk(n), PA_GV // v_w)),
            pl.BlockSpec((1, gb, LANE), lambda b, n: (b, blk(n), PA_Z // LANE)),
            pl.BlockSpec((1, gb, LANE), lambda b, n: (b, blk(jnp.minimum(n + 1, nblk - 1)), PA_Z // LANE)),
        ]

    out = jax.ShapeDtypeStruct((n_batch, s_len, v_w), BF16)
    state = pltpu.VMEM((GLA_HEADS, GLA_HEAD_K, GLA_HEAD_V), F32)
    decay = pltpu.VMEM((2, gb, qk_w), F32)
    return pl.pallas_call(
        _gla_kernel,
        grid=(n_batch, nblk),
        in_specs=specs(fwd) + specs(bwd) + [
            _layer_spec(layer, aw.shape[1:], lambda b, n: (0, 0, 0)),
            _layer_spec(layer, ab.shape[1:], lambda b, n: (0, 0, 0)),
        ],
        out_specs=[pl.BlockSpec((1, gb, v_w), lambda b, n: (b, fwd(n), 0)),
                   pl.BlockSpec((1, gb, v_w), lambda b, n: (b, bwd(n), 0))],
        out_shape=[out, out],
        scratch_shapes=[state, state, decay],
        compiler_params=_params(("parallel", "arbitrary")),
        name="gla",
    )(pa, pa, pa, pa, pa, pa, pa, pa, pa, pa, aw, ab)


def _with_one_hot(v):
    one_hot = (lax.broadcasted_iota(jnp.int32, v.shape, 1) == 0).astype(v.dtype)
    return jnp.concatenate([v, one_hot], axis=1)


def _swa_kernel(*refs, local, aliased):
    if aliased:
        refs = refs[1:]
    if local:
        q_ref, kp_ref, kc_ref, kn_ref, vp_ref, vc_ref, vn_ref, kx_ref, vx_ref, sink_ref, band_ref, o_ref = refs
    else:
        q_ref, kx_ref, vx_ref, sink_ref, o_ref = refs
    nt = (((1,), (1,)), ((), ()))
    kx = kx_ref[0]
    vx = _with_one_hot(vx_ref[0])
    if local:
        n = pl.program_id(2)
        blk = SWA_BLOCK
        kb = jnp.concatenate([kp_ref[0], kc_ref[0], kn_ref[0]], axis=0)
        vb = _with_one_hot(jnp.concatenate([vp_ref[0], vc_ref[0], vn_ref[0]], axis=0))
        n_loc = kb.shape[0]
        col = lax.broadcasted_iota(jnp.int32, (1, n_loc), 1)
        edge = jnp.where((col < blk) & (n == 0), NEG_INF, 0.0)
        edge = jnp.where((col >= n_loc - blk) & (n == pl.num_programs(2) - 1), NEG_INF, edge)
        bias = band_ref[...] + edge
    heads = range(SWA_GROUP)
    qs = [q_ref[0, :, g * LANE:(g + 1) * LANE] for g in heads]
    s_ctx = [lax.dot_general(q, kx, nt, preferred_element_type=F32) for q in qs]
    s_loc = [lax.dot_general(q, kb, nt, preferred_element_type=F32) + bias for q in qs] if local else None
    stage2 = []
    for g in heads:
        sink = sink_ref[0, g:g + 1, 0:1]
        m = jnp.maximum(jnp.max(s_ctx[g], axis=1, keepdims=True), sink)
        if local:
            m = jnp.maximum(m, jnp.max(s_loc[g], axis=1, keepdims=True))
        p_ctx = jnp.exp2(s_ctx[g] - m).astype(BF16)
        p_loc = jnp.exp2(s_loc[g] - m).astype(BF16) if local else None
        stage2.append((p_ctx, p_loc, jnp.exp2(sink - m)))
    for g, (p_ctx, p_loc, p_sink) in enumerate(stage2):
        o = jnp.dot(p_ctx, vx, preferred_element_type=F32)
        if local:
            o += jnp.dot(p_loc, vb, preferred_element_type=F32)
        den = o[:, SWA_HEAD_DIM:SWA_HEAD_DIM + 1] + p_sink
        o_ref[0, :, g * LANE:(g + 1) * LANE] = (o[:, :SWA_HEAD_DIM] / den).astype(BF16)


def _swa_band_table(bq):
    r = np.arange(bq)[:, None]
    c = np.arange(bq + 2 * SWA_BLOCK)[None, :] - SWA_BLOCK
    return jnp.asarray(np.where(np.abs(r - c) <= SWA_WINDOW, 0.0, NEG_INF), F32)


def _swa_call(pr, pa, sink2, *, layer, t_lat):
    n_batch, s_len, _ = pr.shape
    blk = SWA_BLOCK
    l_ctx = s_len - t_lat
    bq = SWA_QTILE if t_lat % SWA_QTILE == 0 else blk
    per = bq // blk
    n_blk = t_lat // blk
    ctx_blk = t_lat // l_ctx
    gw = SWA_GROUP * LANE
    kcol = lambda kh: PR_SK // LANE + kh
    vcol = lambda kh: PA_SV // LANE + kh
    prev = lambda n: jnp.maximum(n * per - 1, 0)
    nxt = lambda n: jnp.minimum((n + 1) * per, n_blk - 1)
    edge = lambda col, pos: pl.BlockSpec((1, blk, LANE), lambda b, kh, n: (b, pos(n), col(kh)))
    body = lambda col: pl.BlockSpec((1, bq, LANE), lambda b, kh, n: (b, n, col(kh)))
    ctx = lambda col: pl.BlockSpec((1, l_ctx, LANE), lambda b, kh, n: (b, ctx_blk, col(kh)))
    sink_spec = _layer_spec(layer, (1, SWA_GROUP, LANE), lambda b, kh, n: (kh, 0, 0))
    out_shape = jax.ShapeDtypeStruct((n_batch, s_len, SWA_HEADS * SWA_HEAD_DIM), BF16)
    band = _swa_band_table(bq)
    o_lat = pl.pallas_call(
        functools.partial(_swa_kernel, local=True, aliased=False),
        grid=(n_batch, SWA_KV_HEADS, t_lat // bq),
        in_specs=[
            pl.BlockSpec((1, bq, gw), lambda b, kh, n: (b, n, PR_SQ // gw + kh)),
            edge(kcol, prev), body(kcol), edge(kcol, nxt),
            edge(vcol, prev), body(vcol), edge(vcol, nxt),
            ctx(kcol), ctx(vcol), sink_spec,
            pl.BlockSpec(band.shape, lambda b, kh, n: (0, 0)),
        ],
        out_specs=pl.BlockSpec((1, bq, gw), lambda b, kh, n: (b, n, kh)),
        out_shape=out_shape,
        compiler_params=_params(("parallel", "parallel", "parallel")),
        name="swa",
    )(pr, pr, pr, pr, pa, pa, pa, pr, pa, sink2, band)
    return pl.pallas_call(
        functools.partial(_swa_kernel, local=False, aliased=True),
        grid=(n_batch, SWA_KV_HEADS, 1),
        in_specs=[
            pl.BlockSpec(memory_space=pl.ANY),
            pl.BlockSpec((1, l_ctx, gw), lambda b, kh, n: (b, ctx_blk, PR_SQ // gw + kh)),
            ctx(kcol), ctx(vcol), sink_spec,
        ],
        out_specs=pl.BlockSpec((1, l_ctx, gw), lambda b, kh, n: (b, ctx_blk, kh)),
        out_shape=out_shape,
        input_output_aliases={0: 0},
        compiler_params=_params(("parallel", "parallel", "parallel")),
        name="swa_ctx",
    )(o_lat, pr, pr, pa, sink2)


def _mla_proj_kernel(cq_ref, ckv_ref, kpe_ref, qnw_ref, kvnw_ref, wq_ref, wkv_ref, cm_ref, sm_ref,
                     q_ref, k_ref, v_ref):
    cq = cq_ref[0].astype(F32)
    ms = jnp.sum(cq * cq, axis=-1, keepdims=True) * (1.0 / MLA_Q_RANK)
    cqn = (cq * lax.rsqrt(ms + EPS) * qnw_ref[...]).astype(BF16)
    q = jnp.dot(cqn, wq_ref[...], preferred_element_type=F32)
    cos_m, sin_m = cm_ref[...], sm_ref[...]
    qs = MLA_SCALE * LOG2E
    hw = 2 * LANE
    for h in range(MLA_HEADS):
        q_ref[0, :, h * hw:h * hw + LANE] = (q[:, h * hw:h * hw + LANE] * qs).astype(BF16)
        pe = q[:, h * hw + LANE:(h + 1) * hw]
        pe = (pe * cos_m + _swap_halves(pe, MLA_ROPE // 4) * sin_m) * qs
        q_ref[0, :, h * hw + LANE:(h + 1) * hw] = pe.astype(BF16)
    ckv = _rms(ckv_ref[0].astype(F32), kvnw_ref[...]).astype(BF16)
    kv = jnp.dot(ckv, wkv_ref[...], preferred_element_type=F32)
    half = MLA_HEADS * MLA_NOPE
    kpe = kpe_ref[0]
    one_hot = (lax.broadcasted_iota(jnp.int32, kpe.shape, 1) == 0).astype(BF16)
    for h in range(MLA_HEADS):
        k_ref[0, :, h * hw:h * hw + LANE] = kv[:, h * LANE:(h + 1) * LANE].astype(BF16)
        k_ref[0, :, h * hw + LANE:(h + 1) * hw] = kpe
        v_ref[0, :, h * hw:h * hw + LANE] = kv[:, half + h * LANE:half + (h + 1) * LANE].astype(BF16)
        v_ref[0, :, h * hw + LANE:(h + 1) * hw] = one_hot


def _mla_proj_call(pa, pr, qnw, kvnw, wq, wkv, cos_m, sin_m, *, layer):
    n_batch, s_len, _ = pa.shape
    tr = _row_tile(s_len)
    qw = MLA_HEADS * 2 * LANE
    kw = MLA_HEADS * MLA_NOPE
    const = lambda shape: _layer_spec(layer, shape, lambda b, i: (0,) * len(shape))
    wide = pl.BlockSpec((1, tr, qw), lambda b, i: (b, i, 0))
    return pl.pallas_call(
        _mla_proj_kernel,
        grid=(n_batch, s_len // tr),
        in_specs=[
            pl.BlockSpec((1, tr, MCQ_PAD), lambda b, i: (b, i, PA_MCQ // MCQ_PAD)),
            pl.BlockSpec((1, tr, MLA_KV_RANK), lambda b, i: (b, i, PA_MCKV // MLA_KV_RANK)),
            pl.BlockSpec((1, tr, LANE), lambda b, i: (b, i, PR_KPE // LANE)),
            const((1, MCQ_PAD)), const((1, MLA_KV_RANK)),
            const((MCQ_PAD, qw)), const((MLA_KV_RANK, 2 * kw)),
            pl.BlockSpec((tr, LANE), lambda b, i: (i, 0)),
            pl.BlockSpec((tr, LANE), lambda b, i: (i, 0)),
        ],
        out_specs=[wide, wide, wide],
        out_shape=[jax.ShapeDtypeStruct((n_batch, s_len, qw), BF16)] * 3,
        compiler_params=_params(("parallel", "parallel")),
        name="mla_proj",
    )(pa, pa, pr, qnw, kvnw, wq, wkv, cos_m, sin_m)


def _flash_kernel(*refs, aliased, sub, kv_sub):
    if aliased:
        refs = refs[1:]
    q_ref, k_ref, v_ref, o_ref, m_ref, acc_ref = refs
    j = pl.program_id(3)

    @pl.when(j == 0)
    def _():
        m_ref[...] = jnp.full_like(m_ref, NEG_INF)
        acc_ref[...] = jnp.zeros_like(acc_ref)

    bq, bk = q_ref.shape[1], k_ref.shape[1]
    sub = min(sub, bq)
    kv_sub = min(kv_sub, bk)
    chains = [slice(c * sub, (c + 1) * sub) for c in range(bq // sub)]
    for t in range(bk // kv_sub):
        k = k_ref[0, t * kv_sub:(t + 1) * kv_sub, :]
        v = v_ref[0, t * kv_sub:(t + 1) * kv_sub, :]
        scores = [lax.dot_general(q_ref[0, rows, :], k, (((1,), (1,)), ((), ())), preferred_element_type=F32)
                  for rows in chains]
        probs, alphas = [], []
        for rows, s in zip(chains, scores):
            m_prev = m_ref[rows, :]
            m_new = jnp.maximum(m_prev, jnp.max(s, axis=1, keepdims=True))
            alphas.append(jnp.exp2(m_prev - m_new))
            probs.append(jnp.exp2(s - m_new).astype(BF16))
            m_ref[rows, :] = m_new
        for rows, p, alpha in zip(chains, probs, alphas):
            acc_ref[rows, :] = alpha * acc_ref[rows, :] + jnp.dot(p, v, preferred_element_type=F32)

    @pl.when(j == pl.num_programs(3) - 1)
    def _():
        acc = acc_ref[...]
        o_ref[0] = (acc[:, :MLA_V] / acc[:, MLA_V:MLA_V + 1]).astype(BF16)


def _flash_call(q, k, v, prev_out, *, bq, bk, kv_sub, q_blk0, n_q, kv_blk0, n_kv, name):
    n_batch, s_len, _ = q.shape
    aliased = prev_out is not None
    hw = 2 * LANE
    kv_mode = dict(pipeline_mode=pl.Buffered(1)) if n_kv == 1 else {}
    in_specs = [
        pl.BlockSpec((1, bq, hw), lambda b, h, i, j: (b, q_blk0 + i, h)),
        pl.BlockSpec((1, bk, hw), lambda b, h, i, j: (b, kv_blk0 + j, h), **kv_mode),
        pl.BlockSpec((1, bk, hw), lambda b, h, i, j: (b, kv_blk0 + j, h), **kv_mode),
    ]
    args = [q, k, v]
    if aliased:
        in_specs = [pl.BlockSpec(memory_space=pl.ANY)] + in_specs
        args = [prev_out] + args
    return pl.pallas_call(
        functools.partial(_flash_kernel, aliased=aliased, sub=FLASH_SUB, kv_sub=kv_sub),
        grid=(n_batch, MLA_HEADS, n_q, n_kv),
        in_specs=in_specs,
        out_specs=pl.BlockSpec((1, bq, LANE), lambda b, h, i, j: (b, q_blk0 + i, h)),
        out_shape=jax.ShapeDtypeStruct((n_batch, s_len, MLA_HEADS * MLA_V), BF16),
        scratch_shapes=[pltpu.VMEM((bq, 1), F32), pltpu.VMEM((bq, hw), F32)],
        input_output_aliases={0: 0} if aliased else {},
        compiler_params=_params(("parallel", "parallel", "parallel", "arbitrary")),
        name=name,
    )(*args)


def _kv_block(s_len):
    for bk in (3328, 1280, 640, 256, 128):
        if s_len % bk == 0:
            return bk
    raise ValueError(f"unsupported stream length {s_len}")


def _merge_kernel(x_ref, mod_ref, g_ref, gr_ref, of_ref, ob_ref, osw_ref, om_ref, gw_ref, wb_ref, wo_ref,
                  o_ref, *, n_batch, t_lat, tr):
    b = pl.program_id(0)
    row0 = pl.program_id(1) * tr
    og = of_ref[0].astype(F32) + ob_ref[0].astype(F32)
    gw = gw_ref[...]
    heads = []
    for h in range(GLA_HEADS):
        heads.append(_rms(og[:, h * GLA_HEAD_V:(h + 1) * GLA_HEAD_V], gw))
    gr = gr_ref[0].astype(F32)
    o_gla = (jnp.concatenate(heads, axis=1) * (gr * jax.nn.sigmoid(gr))).astype(BF16)
    branches = (o_gla, osw_ref[0], om_ref[0])
    merged = None
    for idx, ob in enumerate(branches):
        gate = jax.nn.sigmoid(g_ref[0, :, idx * D_MODEL:(idx + 1) * D_MODEL].astype(F32))
        term = gate * jnp.dot(ob, wb_ref[idx], preferred_element_type=F32)
        merged = term if merged is None else merged + term
    y = jnp.dot(merged.astype(BF16), wo_ref[...], preferred_element_type=F32)
    g_m = _mod_rows(mod_ref, b, n_batch, row0, tr, t_lat, 2)
    o_ref[0] = x_ref[0] + g_m * y


def _merge_call(xa, mod, pa, o_f, o_b, o_swa, o_mla, gw, wb, wo, *, layer, t_lat):
    n_batch, s_len, _ = xa.shape
    tr = 320 if s_len % 320 == 0 else _row_tile(s_len)
    row = lambda width, col: pl.BlockSpec((1, tr, width), lambda b, i: (b, i, col))
    return pl.pallas_call(
        functools.partial(_merge_kernel, n_batch=n_batch, t_lat=t_lat, tr=tr),
        grid=(n_batch, s_len // tr),
        in_specs=[
            row(D_MODEL, 0),
            _layer_spec(layer, mod.shape[1:], lambda b, i: (0, 0)),
            row(3 * D_MODEL, PA_GATES // (3 * D_MODEL)),
            row(D_MODEL, PA_GR // D_MODEL),
            row(D_MODEL, 0), row(D_MODEL, 0), row(D_MODEL, 0), row(D_MODEL, 0),
            _layer_spec(layer, (1, GLA_HEAD_V), lambda b, i: (0, 0)),
            _layer_spec(layer, (3, D_MODEL, D_MODEL), lambda b, i: (0, 0, 0)),
            _layer_spec(layer, (D_MODEL, D_MODEL), lambda b, i: (0, 0)),
        ],
        out_specs=row(D_MODEL, 0),
        out_shape=jax.ShapeDtypeStruct(xa.shape, F32),
        compiler_params=_params(("parallel", "parallel")),
        name="merge",
    )(xa, mod, pa, pa, o_f, o_b, o_swa, o_mla, gw, wb, wo)


def _mlp_kernel(x_ref, mod_ref, nw_ref, w1_ref, w2_ref, fw_ref, o_ref, h_ref, acc_ref,
                *, n_batch, t_lat, tr, final):
    b = pl.program_id(0)
    row0 = pl.program_id(1) * tr
    c = pl.program_id(2)

    @pl.when(c == 0)
    def _():
        shift = _mod_rows(mod_ref, b, n_batch, row0, tr, t_lat, 3)
        scale = _mod_rows(mod_ref, b, n_batch, row0, tr, t_lat, 4)
        h_ref[...] = (_rms(x_ref[0], nw_ref[...]) * (1.0 + scale) + shift).astype(BF16)
        acc_ref[...] = jnp.zeros_like(acc_ref)

    halves = [slice(0, tr // 2), slice(tr // 2, tr)]
    ups = [jnp.maximum(jnp.dot(h_ref[rows, :], w1_ref[...], preferred_element_type=F32), 0.0) for rows in halves]
    for rows, u in zip(halves, ups):
        acc_ref[rows, :] += jnp.dot((u * u).astype(BF16), w2_ref[...], preferred_element_type=F32)

    @pl.when(c == pl.num_programs(2) - 1)
    def _():
        g_f = _mod_rows(mod_ref, b, n_batch, row0, tr, t_lat, 5)
        y = x_ref[0] + g_f * acc_ref[...]
        if final:
            y = _rms(y, fw_ref[...])
        o_ref[0] = y


def _mlp_call(xa, mod, nw, w1, w2, fw, *, layer, t_lat, final):
    n_batch, s_len, _ = xa.shape
    rows = t_lat if final else s_len
    tr = _row_tile(rows)
    return pl.pallas_call(
        functools.partial(_mlp_kernel, n_batch=n_batch, t_lat=t_lat, tr=tr, final=final),
        grid=(n_batch, rows // tr, FF_DIM // FF_CHUNK),
        in_specs=[
            pl.BlockSpec((1, tr, D_MODEL), lambda b, i, c: (b, i, 0)),
            _layer_spec(layer, mod.shape[1:], lambda b, i, c: (0, 0)),
            _layer_spec(layer, (1, D_MODEL), lambda b, i, c: (0, 0)),
            _layer_spec(layer, (D_MODEL, FF_CHUNK), lambda b, i, c: (0, c)),
            _layer_spec(layer, (FF_CHUNK, D_MODEL), lambda b, i, c: (c, 0)),
            pl.BlockSpec((1, D_MODEL), lambda b, i, c: (0, 0)),
        ],
        out_specs=pl.BlockSpec((1, tr, D_MODEL), lambda b, i, c: (b, i, 0)),
        out_shape=jax.ShapeDtypeStruct((n_batch, rows, D_MODEL), F32),
        scratch_shapes=[pltpu.VMEM((tr, D_MODEL), BF16), pltpu.VMEM((tr, D_MODEL), F32)],
        compiler_params=_params(("parallel", "parallel", "arbitrary")),
        name="mlp",
    )(xa, mod, nw, w1, w2, fw)


def _rope_tables(t_lat, l_ctx, dim):
    f32 = np.float32
    pos = np.arange(t_lat)
    row, col = pos // GRID_W, pos % GRID_W
    d_axis = dim // 2
    inv = (f32(ROPE_BASE) ** (-np.arange(0, d_axis, 2, dtype=f32) / f32(d_axis))).astype(f32)
    ang_r = row.astype(f32)[:, None] * inv
    ang_c = col.astype(f32)[:, None] * inv
    cos = np.concatenate([np.cos(ang_r)] * 2 + [np.cos(ang_c)] * 2, axis=1)
    sin = np.concatenate([-np.sin(ang_r), np.sin(ang_r), -np.sin(ang_c), np.sin(ang_c)], axis=1)
    if dim < LANE:
        cos = np.concatenate([cos, np.ones((t_lat, LANE - dim), f32)], axis=1)
        sin = np.concatenate([sin, np.zeros((t_lat, LANE - dim), f32)], axis=1)
    cos = np.concatenate([cos, np.ones((l_ctx, LANE), f32)], axis=0)
    sin = np.concatenate([sin, np.zeros((l_ctx, LANE), f32)], axis=0)
    return jnp.asarray(cos, F32), jnp.asarray(sin, F32)


def _split_w_in(w):
    sizes = (512, 512, 1024, 1024, GLA_RANK, GLA_RANK, 1024, 256, 256, MLA_Q_RANK, MLA_KV_RANK, MLA_ROPE,
             3 * D_MODEL)
    offs = np.cumsum((0,) + sizes)
    return [w[..., offs[i]:offs[i + 1]] for i in range(len(sizes))]


def _layer_weights(w_in, gla_a_w, gla_a_b, mla_q_norm_w, mla_w_uq, mla_kv_norm_w, mla_w_ukv, swa_sink):
    depth = w_in.shape[0]
    gq, gk, gv, gr, gzf, gzb, sq, sk, sv, mcq, mckv, mkr, gates = _split_w_in(w_in)
    zpad = lambda n: jnp.zeros((depth, D_MODEL, n), F32)
    w_plain = jnp.concatenate(
        [gates, gr, gv, gq, gk, mcq, zpad(MCQ_PAD - MLA_Q_RANK), sv, mckv, gzf, gzb, zpad(LANE - 2 * GLA_RANK)],
        axis=-1).astype(BF16)
    w_rope = jnp.concatenate([sq, sk, mkr, zpad(LANE - MLA_ROPE)], axis=-1).astype(BF16)
    aw = jnp.zeros((depth, 2, LANE, GLA_HEADS * GLA_HEAD_K), F32)
    aw = aw.at[:, 0, :GLA_RANK].set(gla_a_w[:, 0]).at[:, 1, GLA_RANK:2 * GLA_RANK].set(gla_a_w[:, 1])
    aw = aw.astype(BF16)
    ab = gla_a_b.reshape(depth, 2, 1, GLA_HEADS * GLA_HEAD_K)
    wq = mla_w_uq.reshape(depth, MLA_Q_RANK, MLA_HEADS, MLA_NOPE + MLA_ROPE)
    wq = jnp.pad(wq, ((0, 0), (0, MCQ_PAD - MLA_Q_RANK), (0, 0), (0, 2 * LANE - MLA_NOPE - MLA_ROPE)))
    wq = wq.reshape(depth, MCQ_PAD, MLA_HEADS * 2 * LANE).astype(BF16)
    qnw = jnp.pad(mla_q_norm_w, ((0, 0), (0, MCQ_PAD - MLA_Q_RANK))).reshape(depth, 1, MCQ_PAD)
    wkv = mla_w_ukv.reshape(depth, MLA_KV_RANK, MLA_HEADS, MLA_NOPE + MLA_V)
    wkv = jnp.concatenate([wkv[..., :MLA_NOPE].reshape(depth, MLA_KV_RANK, -1),
                           wkv[..., MLA_NOPE:].reshape(depth, MLA_KV_RANK, -1)], axis=-1).astype(BF16)
    kvnw = mla_kv_norm_w.reshape(depth, 1, MLA_KV_RANK)
    sink2 = jnp.broadcast_to((swa_sink * LOG2E).reshape(depth, SWA_KV_HEADS, SWA_GROUP, 1),
                             (depth, SWA_KV_HEADS, SWA_GROUP, LANE))
    return w_plain, w_rope, aw, ab, wq, qnw, wkv, kvnw, sink2


def kernel(x, c, ctx, c_ctx, ada_w, ada_b, norm_mix_w, w_in, gla_a_w, gla_a_b, gla_norm_w, swa_sink,
           mla_q_norm_w, mla_w_uq, mla_kv_norm_w, mla_w_ukv, w_branch, w_out, norm_mlp_w, mlp_w1, mlp_w2,
           final_norm_w):
    n_batch, t_lat, d_model = x.shape
    l_ctx = ctx.shape[1]
    depth = ada_w.shape[0]
    s_len = t_lat + l_ctx
    assert d_model == D_MODEL and n_batch + 1 <= MOD_ROWS
    assert t_lat % GLA_BLOCK == 0 and l_ctx % GLA_BLOCK == 0 and t_lat % l_ctx == 0 and t_lat % GRID_W == 0

    xa = jnp.concatenate([x, ctx], axis=1)
    cin = jnp.zeros((MOD_ROWS, D_MODEL), F32).at[:n_batch].set(c).at[n_batch].set(c_ctx)
    mod_all = _ada_call(cin, ada_w, ada_b)

    cos_s, sin_s = _rope_tables(t_lat, l_ctx, SWA_HEAD_DIM)
    cos_m, sin_m = _rope_tables(t_lat, l_ctx, MLA_ROPE)
    tabs = (cos_s, sin_s, cos_m, sin_m)

    bq = min(1024, t_lat)
    bk = _kv_block(s_len)

    (w_plain, w_rope, aw, ab, wq, qnw, wkv, kvnw, sink2) = _layer_weights(
        w_in, gla_a_w, gla_a_b, mla_q_norm_w, mla_w_uq, mla_kv_norm_w, mla_w_ukv, swa_sink)
    row_vec = lambda v: v.reshape(depth, 1, -1)
    nw_mix, nw_mlp, gw = row_vec(norm_mix_w), row_vec(norm_mlp_w), row_vec(gla_norm_w)
    wb, wo, w1, w2 = (w.astype(BF16) for w in (w_branch, w_out, mlp_w1, mlp_w2))
    mod = mod_all

    for layer in range(depth):
        pa = _proj_call(xa, mod, nw_mix, w_plain, None, layer=layer, t_lat=t_lat, bn=PA_BN, rope=False,
                        name="proj_plain")
        pr = _proj_call(xa, mod, nw_mix, w_rope, tabs, layer=layer, t_lat=t_lat, bn=PR_WIDTH, rope=True,
                        name="proj_rope")

        o_f, o_b = _gla_call(pa, aw, ab, layer=layer, t_lat=t_lat)
        o_swa = _swa_call(pr, pa, sink2, layer=layer, t_lat=t_lat)

        q, k, v = _mla_proj_call(pa, pr, qnw, kvnw, wq, wkv, cos_m, sin_m, layer=layer)
        o_mla = _flash_call(q, k, v, None, bq=bq, bk=s_len, kv_sub=bk, q_blk0=0, n_q=t_lat // bq, kv_blk0=0,
                            n_kv=1, name="mla_flash")
        o_mla = _flash_call(q, k, v, o_mla, bq=l_ctx, bk=l_ctx, kv_sub=l_ctx, q_blk0=t_lat // l_ctx, n_q=1,
                            kv_blk0=t_lat // l_ctx, n_kv=1, name="mla_flash_ctx")

        xa = _merge_call(xa, mod, pa, o_f, o_b, o_swa, o_mla, gw, wb, wo, layer=layer, t_lat=t_lat)
        xa = _mlp_call(xa, mod, nw_mlp, w1, w2, final_norm_w.reshape(1, -1), layer=layer, t_lat=t_lat,
                       final=layer == depth - 1)
    return xa
```

```python
import functools

import jax
import jax.numpy as jnp
import numpy as np
from jax import lax
from jax.experimental import pallas as pl
from jax.experimental.pallas import tpu as pltpu

F32 = jnp.float32
BF16 = jnp.bfloat16

D_MODEL = 1024
EPS = 1e-6
ROPE_BASE = 10000.0
NEG_INF = -1e30
LOG2E = 1.4426950408889634
GRID_W = 64

GLA_HEADS = 4
GLA_HEAD_K = 128
GLA_HEAD_V = 256
GLA_RANK = 16
GLA_GATE_NORM = 16.0
GLA_CHUNK = 64
GLA_BLOCK = 256

SWA_HEADS = 8
SWA_KV_HEADS = 2
SWA_GROUP = 4
SWA_HEAD_DIM = 128
SWA_WINDOW = 128
SWA_BLOCK = 128
SWA_SCALE = SWA_HEAD_DIM ** -0.5
SWA_QTILE = 512

MLA_HEADS = 8
MLA_Q_RANK = 384
MLA_KV_RANK = 256
MLA_NOPE = 128
MLA_ROPE = 64
MLA_V = 128
MLA_SCALE = (MLA_NOPE + MLA_ROPE) ** -0.5

FF_DIM = 4 * D_MODEL
FF_CHUNK = 2048
FLASH_SUB = 256

LANE = 128
MOD_ROWS = 8
VMEM_LIMIT = 56 * 1024 * 1024

PA_GATES, PA_GR, PA_GV, PA_GQ, PA_GK = 0, 3072, 4096, 5120, 5632
PA_MCQ, PA_SV, PA_MCKV, PA_Z = 6144, 6656, 6912, 7168
PA_WIDTH = 7296
PA_BN = 2432
MCQ_PAD = 512
PR_SQ, PR_SK, PR_KPE = 0, 1024, 1280
PR_WIDTH = 1408
PR_SLABS = PR_WIDTH // LANE


def _params(sem):
    return pltpu.CompilerParams(dimension_semantics=sem, vmem_limit_bytes=VMEM_LIMIT)


def _layer_spec(layer, block, index_map):
    return pl.BlockSpec((None,) + tuple(block), lambda *g: (layer,) + tuple(index_map(*g)))


def _row_tile(s):
    for tr in (640, 512, 256, 128):
        if s % tr == 0:
            return tr
    raise ValueError(f"unsupported stream length {s}")


def _rms(x, w):
    return x * lax.rsqrt(jnp.mean(x * x, axis=-1, keepdims=True) + EPS) * w


def _mod_rows(mod_ref, b, n_batch, row0, rows, t_lat, idx):
    lo = idx * D_MODEL
    lat = mod_ref[pl.ds(b, 1), lo:lo + D_MODEL]
    ctx = mod_ref[n_batch:n_batch + 1, lo:lo + D_MODEL]
    rid = row0 + lax.broadcasted_iota(jnp.int32, (rows, 1), 0)
    return jnp.where(rid >= t_lat, ctx, lat)


def _swap_halves(x, half):
    lane = lax.broadcasted_iota(jnp.int32, x.shape, 1)
    return jnp.where((lane & half) == 0, pltpu.roll(x, LANE - half, 1), pltpu.roll(x, half, 1))


def _log_sigmoid(x):
    return jnp.minimum(x, 0.0) - jnp.log1p(jnp.exp(-jnp.abs(x)))


def _ada_kernel(c_ref, w_ref, b_ref, o_ref):
    c = c_ref[...]
    a = (c * jax.nn.sigmoid(c)).astype(BF16)
    o_ref[0] = jnp.dot(a, w_ref[0].astype(BF16), preferred_element_type=F32) + b_ref[0]


def _ada_call(cin, ada_w, ada_b):
    depth = ada_w.shape[0]
    nblk = ada_w.shape[2] // D_MODEL
    return pl.pallas_call(
        _ada_kernel,
        grid=(depth, nblk),
        in_specs=[
            pl.BlockSpec((MOD_ROWS, D_MODEL), lambda l, j: (0, 0)),
            pl.BlockSpec((1, D_MODEL, D_MODEL), lambda l, j: (l, 0, j)),
            pl.BlockSpec((1, 1, D_MODEL), lambda l, j: (l, 0, j)),
        ],
        out_specs=pl.BlockSpec((1, MOD_ROWS, D_MODEL), lambda l, j: (l, 0, j)),
        out_shape=jax.ShapeDtypeStruct((depth, MOD_ROWS, ada_w.shape[2]), F32),
        compiler_params=_params(("parallel", "parallel")),
        name="ada_mod",
    )(cin, ada_w, ada_b.reshape(depth, 1, -1))


def _proj_kernel(x_ref, mod_ref, nw_ref, w_ref, *rest, n_batch, t_lat, tr, rope):
    o_ref = rest[-1]
    b = pl.program_id(1)
    row0 = pl.program_id(2) * tr
    x = x_ref[0]
    shift = _mod_rows(mod_ref, b, n_batch, row0, tr, t_lat, 0)
    scale = _mod_rows(mod_ref, b, n_batch, row0, tr, t_lat, 1)
    h = _rms(x, nw_ref[...]) * (1.0 + scale) + shift
    acc = jnp.dot(h.astype(BF16), w_ref[...], preferred_element_type=F32)
    if not rope:
        o_ref[0] = acc.astype(BF16)
        return
    cs_ref, sn_ref, cm_ref, sm_ref = rest[:4]
    cos_s, sin_s = cs_ref[...], sn_ref[...]
    for s in range(PR_SLABS):
        a = acc[:, s * LANE:(s + 1) * LANE]
        if s * LANE < PR_KPE:
            r = a * cos_s + _swap_halves(a, SWA_HEAD_DIM // 4) * sin_s
            if s * LANE < PR_SK:
                r = r * (SWA_SCALE * LOG2E)
        else:
            r = a * cm_ref[...] + _swap_halves(a, MLA_ROPE // 4) * sm_ref[...]
        o_ref[0, :, s * LANE:(s + 1) * LANE] = r.astype(BF16)


def _proj_call(xa, mod, nw, w, tabs, *, layer, t_lat, bn, rope, name):
    n_batch, s_len, _ = xa.shape
    tr = _row_tile(s_len)
    width = w.shape[-1]
    grid = (width // bn, n_batch, s_len // tr)
    in_specs = [
        pl.BlockSpec((1, tr, D_MODEL), lambda j, b, i: (b, i, 0)),
        _layer_spec(layer, mod.shape[1:], lambda j, b, i: (0, 0)),
        _layer_spec(layer, (1, D_MODEL), lambda j, b, i: (0, 0)),
        _layer_spec(layer, (D_MODEL, bn), lambda j, b, i: (0, j)),
    ]
    args = [xa, mod, nw, w]
    if rope:
        in_specs += [pl.BlockSpec((tr, LANE), lambda j, b, i: (i, 0))] * 4
        args += list(tabs)
    return pl.pallas_call(
        functools.partial(_proj_kernel, n_batch=n_batch, t_lat=t_lat, tr=tr, rope=rope),
        grid=grid,
        in_specs=in_specs,
        out_specs=pl.BlockSpec((1, tr, bn), lambda j, b, i: (b, i, j)),
        out_shape=jax.ShapeDtypeStruct((n_batch, s_len, width), BF16),
        compiler_params=_params(("parallel", "parallel", "parallel")),
        name=name,
    )(*args)


def _gla_mask(reverse):
    gb, ch = GLA_BLOCK, GLA_CHUNK
    r = lax.broadcasted_iota(jnp.int32, (gb, gb), 0)
    c = lax.broadcasted_iota(jnp.int32, (gb, gb), 1)
    return ((r // ch) == (c // ch)) & ((c >= r) if reverse else (c <= r))


def _gla_log_decay(z_ref, aw, ab):
    la = jnp.dot(z_ref[0], aw, preferred_element_type=F32) + ab
    la = _log_sigmoid(la) * (1.0 / GLA_GATE_NORM)
    hi = la.astype(BF16)
    lo = (la - hi.astype(F32)).astype(BF16)
    return jnp.concatenate([hi, lo], axis=1)


def _gla_cumulate(hi_lo, reverse):
    tmat = jnp.where(_gla_mask(reverse), 1.0, 0.0).astype(BF16)
    hw = GLA_HEADS * GLA_HEAD_K
    res = jnp.dot(tmat, hi_lo, preferred_element_type=F32)
    return res[:, :hw] + res[:, hw:]


def _gla_decay(z_ref, aw, ab, reverse):
    return _gla_cumulate(_gla_log_decay(z_ref, aw, ab), reverse)


def _gla_kernel(qf_ref, kf_ref, vf_ref, zf_ref, zfn_ref, qb_ref, kb_ref, vb_ref, zb_ref, zbn_ref, aw_ref, ab_ref,
                of_ref, ob_ref, stf_ref, stb_ref, bc_ref):
    n = pl.program_id(1)

    @pl.when(n == 0)
    def _():
        stf_ref[...] = jnp.zeros_like(stf_ref)
        stb_ref[...] = jnp.zeros_like(stb_ref)
        bc_ref[0] = _gla_decay(zf_ref, aw_ref[0], ab_ref[0], False)
        bc_ref[1] = _gla_decay(zb_ref, aw_ref[1], ab_ref[1], True)

    gb, ch, dk, dv = GLA_BLOCK, GLA_CHUNK, GLA_HEAD_K, GLA_HEAD_V
    n_ch = gb // ch
    nt = (((1,), (1,)), ((), ()))
    dirs = ((qf_ref, kf_ref, vf_ref, zfn_ref, of_ref, stf_ref, False),
            (qb_ref, kb_ref, vb_ref, zbn_ref, ob_ref, stb_ref, True))
    fronts = [(bc_ref[d], _gla_mask(rev)) for d, (_, _, _, _, _, _, rev) in enumerate(dirs)]
    upcoming = [_gla_log_decay(z_next_ref, aw_ref[d], ab_ref[d]) for d, (_, _, _, z_next_ref, _, _, _) in enumerate(dirs)]

    chains = []
    for (q_ref, k_ref, v_ref, _, o_ref, st_ref, rev), (bcum_all, mask) in zip(dirs, fronts):
        order = range(n_ch - 1, -1, -1) if rev else range(n_ch)
        for h in range(GLA_HEADS):
            bcum = bcum_all[:, h * dk:(h + 1) * dk]
            q = q_ref[0, :, h * dk:(h + 1) * dk].astype(F32)
            k = k_ref[0, :, h * dk:(h + 1) * dk].astype(F32)
            q_dec = (q * (dk ** -0.5) * jnp.exp(bcum)).astype(BF16)
            k_inv = (k * jnp.exp(-bcum)).astype(BF16)
            v = v_ref[0, :, h * dv:(h + 1) * dv]
            a = lax.dot_general(q_dec, k_inv, nt, preferred_element_type=F32)
            tots, incs = [], []
            for cidx in order:
                lo_r, hi_r = cidx * ch, (cidx + 1) * ch
                last = lo_r if rev else hi_r - 1
                tot = bcum[last:last + 1, :]
                k_end = k[lo_r:hi_r] * jnp.exp(tot - bcum[lo_r:hi_r])
                incs.append(jnp.dot(k_end.T.astype(BF16), v[lo_r:hi_r], preferred_element_type=F32))
                tots.append(tot)
            pad = jnp.zeros((8 - n_ch, dk), F32)
            g_cols = jnp.exp(jnp.concatenate(tots + [pad], axis=0)).T
            chains.append((o_ref, st_ref, h, order, mask, q_dec, v, a, incs, g_cols))

    upcoming = [_gla_cumulate(hi_lo, rev) for hi_lo, (_, _, _, _, _, _, rev) in zip(upcoming, dirs)]

    stage2 = []
    for o_ref, st_ref, h, order, mask, q_dec, v, a, incs, g_cols in chains:
        a = jnp.where(mask, a, 0.0).astype(BF16)
        st = st_ref[h]
        entering = []
        for i, _ in enumerate(order):
            entering.append(st.astype(BF16))
            st = st * g_cols[:, i:i + 1] + incs[i]
        st_ref[h] = st
        stage2.append((o_ref, h, order, q_dec, v, a, entering))

    for o_ref, h, order, q_dec, v, a, entering in stage2:
        o_intra = jnp.dot(a, v, preferred_element_type=F32)
        for i, cidx in enumerate(order):
            lo_r, hi_r = cidx * ch, (cidx + 1) * ch
            o = o_intra[lo_r:hi_r] + jnp.dot(q_dec[lo_r:hi_r], entering[i], preferred_element_type=F32)
            o_ref[0, lo_r:hi_r, h * dv:(h + 1) * dv] = o.astype(BF16)

    for d, nxt in enumerate(upcoming):
        bc_ref[d] = nxt


def _gla_call(pa, aw, ab, *, layer, t_lat):
    n_batch, s_len, _ = pa.shape
    gb = GLA_BLOCK
    n_lat, n_ctx = t_lat // gb, (s_len - t_lat) // gb
    nblk = n_lat + n_ctx
    qk_w, v_w = GLA_HEADS * GLA_HEAD_K, GLA_HEADS * GLA_HEAD_V

    fwd = lambda n: jnp.where(n < n_ctx, n_lat + n, n - n_ctx)
    bwd = lambda n: jnp.where(n < n_ctx, n_lat + n_ctx - 1 - n, n_lat - 1 - (n - n_ctx))

    def specs(blk):
        return [
            pl.BlockSpec((1, gb, qk_w), lambda b, n: (b, blk(n), PA_GQ // qk_w)),
            pl.BlockSpec((1, gb, qk_w), lambda b, n: (b, blk(n), PA_GK // qk_w)),
            pl.BlockSpec((1, gb, v_w), lambda b, n: (b, blk(n), PA_GV // v_w)),
            pl.BlockSpec((1, gb, LANE), lambda b, n: (b, blk(n), PA_Z // LANE)),
            pl.BlockSpec((1, gb, LANE), lambda b, n: (b, blk(jnp.minimum(n + 1, nblk - 1)), PA_Z // LANE)),
        ]

    out = jax.ShapeDtypeStruct((n_batch, s_len, v_w), BF16)
    state = pltpu.VMEM((GLA_HEADS, GLA_HEAD_K, GLA_HEAD_V), F32)
    decay = pltpu.VMEM((2, gb, qk_w), F32)
    return pl.pallas_call(
        _gla_kernel,
        grid=(n_batch, nblk),
        in_specs=specs(fwd) + specs(bwd) + [
            _layer_spec(layer, aw.shape[1:], lambda b, n: (0, 0, 0)),
            _layer_spec(layer, ab.shape[1:], lambda b, n: (0, 0, 0)),
        ],
        out_specs=[pl.BlockSpec((1, gb, v_w), lambda b, n: (b, fwd(n), 0)),
                   pl.BlockSpec((1, gb, v_w), lambda b, n: (b, bwd(n), 0))],
        out_shape=[out, out],
        scratch_shapes=[state, state, decay],
        compiler_params=_params(("parallel", "arbitrary")),
        name="gla",
    )(pa, pa, pa, pa, pa, pa, pa, pa, pa, pa, aw, ab)


def _with_one_hot(v):
    one_hot = (lax.broadcasted_iota(jnp.int32, v.shape, 1) == 0).astype(v.dtype)
    return jnp.concatenate([v, one_hot], axis=1)


def _swa_kernel(*refs, local, aliased):
    if aliased:
        refs = refs[1:]
    if local:
        q_ref, kp_ref, kc_ref, kn_ref, vp_ref, vc_ref, vn_ref, kx_ref, vx_ref, sink_ref, band_ref, o_ref = refs
    else:
        q_ref, kx_ref, vx_ref, sink_ref, o_ref = refs
    nt = (((1,), (1,)), ((), ()))
    kx = kx_ref[0]
    vx = _with_one_hot(vx_ref[0])
    if local:
        n = pl.program_id(2)
        blk = SWA_BLOCK
        kb = jnp.concatenate([kp_ref[0], kc_ref[0], kn_ref[0]], axis=0)
        vb = _with_one_hot(jnp.concatenate([vp_ref[0], vc_ref[0], vn_ref[0]], axis=0))
        n_loc = kb.shape[0]
        col = lax.broadcasted_iota(jnp.int32, (1, n_loc), 1)
        edge = jnp.where((col < blk) & (n == 0), NEG_INF, 0.0)
        edge = jnp.where((col >= n_loc - blk) & (n == pl.num_programs(2) - 1), NEG_INF, edge)
        bias = band_ref[...] + edge
    heads = range(SWA_GROUP)
    qs = [q_ref[0, :, g * LANE:(g + 1) * LANE] for g in heads]
    s_ctx = [lax.dot_general(q, kx, nt, preferred_element_type=F32) for q in qs]
    s_loc = [lax.dot_general(q, kb, nt, preferred_element_type=F32) + bias for q in qs] if local else None
    stage2 = []
    for g in heads:
        sink = sink_ref[0, g:g + 1, 0:1]
        m = jnp.maximum(jnp.max(s_ctx[g], axis=1, keepdims=True), sink)
        if local:
            m = jnp.maximum(m, jnp.max(s_loc[g], axis=1, keepdims=True))
        p_ctx = jnp.exp2(s_ctx[g] - m).astype(BF16)
        p_loc = jnp.exp2(s_loc[g] - m).astype(BF16) if local else None
        stage2.append((p_ctx, p_loc, jnp.exp2(sink - m)))
    for g, (p_ctx, p_loc, p_sink) in enumerate(stage2):
        o = jnp.dot(p_ctx, vx, preferred_element_type=F32)
        if local:
            o += jnp.dot(p_loc, vb, preferred_element_type=F32)
        den = o[:, SWA_HEAD_DIM:SWA_HEAD_DIM + 1] + p_sink
        o_ref[0, :, g * LANE:(g + 1) * LANE] = (o[:, :SWA_HEAD_DIM] / den).astype(BF16)


def _swa_band_table(bq):
    r = np.arange(bq)[:, None]
    c = np.arange(bq + 2 * SWA_BLOCK)[None, :] - SWA_BLOCK
    return jnp.asarray(np.where(np.abs(r - c) <= SWA_WINDOW, 0.0, NEG_INF), F32)


def _swa_call(pr, pa, sink2, *, layer, t_lat):
    n_batch, s_len, _ = pr.shape
    blk = SWA_BLOCK
    l_ctx = s_len - t_lat
    bq = SWA_QTILE if t_lat % SWA_QTILE == 0 else blk
    per = bq // blk
    n_blk = t_lat // blk
    ctx_blk = t_lat // l_ctx
    gw = SWA_GROUP * LANE
    kcol = lambda kh: PR_SK // LANE + kh
    vcol = lambda kh: PA_SV // LANE + kh
    prev = lambda n: jnp.maximum(n * per - 1, 0)
    nxt = lambda n: jnp.minimum((n + 1) * per, n_blk - 1)
    edge = lambda col, pos: pl.BlockSpec((1, blk, LANE), lambda b, kh, n: (b, pos(n), col(kh)))
    body = lambda col: pl.BlockSpec((1, bq, LANE), lambda b, kh, n: (b, n, col(kh)))
    ctx = lambda col: pl.BlockSpec((1, l_ctx, LANE), lambda b, kh, n: (b, ctx_blk, col(kh)))
    sink_spec = _layer_spec(layer, (1, SWA_GROUP, LANE), lambda b, kh, n: (kh, 0, 0))
    out_shape = jax.ShapeDtypeStruct((n_batch, s_len, SWA_HEADS * SWA_HEAD_DIM), BF16)
    band = _swa_band_table(bq)
    o_lat = pl.pallas_call(
        functools.partial(_swa_kernel, local=True, aliased=False),
        grid=(n_batch, SWA_KV_HEADS, t_lat // bq),
        in_specs=[
            pl.BlockSpec((1, bq, gw), lambda b, kh, n: (b, n, PR_SQ // gw + kh)),
            edge(kcol, prev), body(kcol), edge(kcol, nxt),
            edge(vcol, prev), body(vcol), edge(vcol, nxt),
            ctx(kcol), ctx(vcol), sink_spec,
            pl.BlockSpec(band.shape, lambda b, kh, n: (0, 0)),
        ],
        out_specs=pl.BlockSpec((1, bq, gw), lambda b, kh, n: (b, n, kh)),
        out_shape=out_shape,
        compiler_params=_params(("parallel", "parallel", "parallel")),
        name="swa",
    )(pr, pr, pr, pr, pa, pa, pa, pr, pa, sink2, band)
    return pl.pallas_call(
        functools.partial(_swa_kernel, local=False, aliased=True),
        grid=(n_batch, SWA_KV_HEADS, 1),
        in_specs=[
            pl.BlockSpec(memory_space=pl.ANY),
            pl.BlockSpec((1, l_ctx, gw), lambda b, kh, n: (b, ctx_blk, PR_SQ // gw + kh)),
            ctx(kcol), ctx(vcol), sink_spec,
        ],
        out_specs=pl.BlockSpec((1, l_ctx, gw), lambda b, kh, n: (b, ctx_blk, kh)),
        out_shape=out_shape,
        input_output_aliases={0: 0},
        compiler_params=_params(("parallel", "parallel", "parallel")),
        name="swa_ctx",
    )(o_lat, pr, pr, pa, sink2)


def _mla_proj_kernel(cq_ref, ckv_ref, kpe_ref, qnw_ref, kvnw_ref, wq_ref, wkv_ref, cm_ref, sm_ref,
                     q_ref, k_ref, v_ref):
    cq = cq_ref[0].astype(F32)
    ms = jnp.sum(cq * cq, axis=-1, keepdims=True) * (1.0 / MLA_Q_RANK)
    cqn = (cq * lax.rsqrt(ms + EPS) * qnw_ref[...]).astype(BF16)
    q = jnp.dot(cqn, wq_ref[...], preferred_element_type=F32)
    cos_m, sin_m = cm_ref[...], sm_ref[...]
    qs = MLA_SCALE * LOG2E
    hw = 2 * LANE
    for h in range(MLA_HEADS):
        q_ref[0, :, h * hw:h * hw + LANE] = (q[:, h * hw:h * hw + LANE] * qs).astype(BF16)
        pe = q[:, h * hw + LANE:(h + 1) * hw]
        pe = (pe * cos_m + _swap_halves(pe, MLA_ROPE // 4) * sin_m) * qs
        q_ref[0, :, h * hw + LANE:(h + 1) * hw] = pe.astype(BF16)
    ckv = _rms(ckv_ref[0].astype(F32), kvnw_ref[...]).astype(BF16)
    kv = jnp.dot(ckv, wkv_ref[...], preferred_element_type=F32)
    half = MLA_HEADS * MLA_NOPE
    kpe = kpe_ref[0]
    one_hot = (lax.broadcasted_iota(jnp.int32, kpe.shape, 1) == 0).astype(BF16)
    for h in range(MLA_HEADS):
        k_ref[0, :, h * hw:h * hw + LANE] = kv[:, h * LANE:(h + 1) * LANE].astype(BF16)
        k_ref[0, :, h * hw + LANE:(h + 1) * hw] = kpe
        v_ref[0, :, h * hw:h * hw + LANE] = kv[:, half + h * LANE:half + (h + 1) * LANE].astype(BF16)
        v_ref[0, :, h * hw + LANE:(h + 1) * hw] = one_hot


def _mla_proj_call(pa, pr, qnw, kvnw, wq, wkv, cos_m, sin_m, *, layer):
    n_batch, s_len, _ = pa.shape
    tr = _row_tile(s_len)
    qw = MLA_HEADS * 2 * LANE
    kw = MLA_HEADS * MLA_NOPE
    const = lambda shape: _layer_spec(layer, shape, lambda b, i: (0,) * len(shape))
    wide = pl.BlockSpec((1, tr, qw), lambda b, i: (b, i, 0))
    return pl.pallas_call(
        _mla_proj_kernel,
        grid=(n_batch, s_len // tr),
        in_specs=[
            pl.BlockSpec((1, tr, MCQ_PAD), lambda b, i: (b, i, PA_MCQ // MCQ_PAD)),
            pl.BlockSpec((1, tr, MLA_KV_RANK), lambda b, i: (b, i, PA_MCKV // MLA_KV_RANK)),
            pl.BlockSpec((1, tr, LANE), lambda b, i: (b, i, PR_KPE // LANE)),
            const((1, MCQ_PAD)), const((1, MLA_KV_RANK)),
            const((MCQ_PAD, qw)), const((MLA_KV_RANK, 2 * kw)),
            pl.BlockSpec((tr, LANE), lambda b, i: (i, 0)),
            pl.BlockSpec((tr, LANE), lambda b, i: (i, 0)),
        ],
        out_specs=[wide, wide, wide],
        out_shape=[jax.ShapeDtypeStruct((n_batch, s_len, qw), BF16)] * 3,
        compiler_params=_params(("parallel", "parallel")),
        name="mla_proj",
    )(pa, pa, pr, qnw, kvnw, wq, wkv, cos_m, sin_m)


def _flash_kernel(*refs, aliased, sub, kv_sub):
    if aliased:
        refs = refs[1:]
    q_ref, k_ref, v_ref, o_ref, m_ref, acc_ref = refs
    j = pl.program_id(3)

    @pl.when(j == 0)
    def _():
        m_ref[...] = jnp.full_like(m_ref, NEG_INF)
        acc_ref[...] = jnp.zeros_like(acc_ref)

    bq, bk = q_ref.shape[1], k_ref.shape[1]
    sub = min(sub, bq)
    kv_sub = min(kv_sub, bk)
    chains = [slice(c * sub, (c + 1) * sub) for c in range(bq // sub)]
    for t in range(bk // kv_sub):
        k = k_ref[0, t * kv_sub:(t + 1) * kv_sub, :]
        v = v_ref[0, t * kv_sub:(t + 1) * kv_sub, :]
        scores = [lax.dot_general(q_ref[0, rows, :], k, (((1,), (1,)), ((), ())), preferred_element_type=F32)
                  for rows in chains]
        probs, alphas = [], []
        for rows, s in zip(chains, scores):
            m_prev = m_ref[rows, :]
            m_new = jnp.maximum(m_prev, jnp.max(s, axis=1, keepdims=True))
            alphas.append(jnp.exp2(m_prev - m_new))
            probs.append(jnp.exp2(s - m_new).astype(BF16))
            m_ref[rows, :] = m_new
        for rows, p, alpha in zip(chains, probs, alphas):
            acc_ref[rows, :] = alpha * acc_ref[rows, :] + jnp.dot(p, v, preferred_element_type=F32)

    @pl.when(j == pl.num_programs(3) - 1)
    def _():
        acc = acc_ref[...]
        o_ref[0] = (acc[:, :MLA_V] / acc[:, MLA_V:MLA_V + 1]).astype(BF16)


def _flash_call(q, k, v, prev_out, *, bq, bk, kv_sub, q_blk0, n_q, kv_blk0, n_kv, name):
    n_batch, s_len, _ = q.shape
    aliased = prev_out is not None
    hw = 2 * LANE
    in_specs = [
        pl.BlockSpec((1, bq, hw), lambda b, h, i, j: (b, q_blk0 + i, h)),
        pl.BlockSpec((1, bk, hw), lambda b, h, i, j: (b, kv_blk0 + j, h)),
        pl.BlockSpec((1, bk, hw), lambda b, h, i, j: (b, kv_blk0 + j, h)),
    ]
    args = [q, k, v]
    if aliased:
        in_specs = [pl.BlockSpec(memory_space=pl.ANY)] + in_specs
        args = [prev_out] + args
    return pl.pallas_call(
        functools.partial(_flash_kernel, aliased=aliased, sub=FLASH_SUB, kv_sub=kv_sub),
        grid=(n_batch, MLA_HEADS, n_q, n_kv),
        in_specs=in_specs,
        out_specs=pl.BlockSpec((1, bq, LANE), lambda b, h, i, j: (b, q_blk0 + i, h)),
        out_shape=jax.ShapeDtypeStruct((n_batch, s_len, MLA_HEADS * MLA_V), BF16),
        scratch_shapes=[pltpu.VMEM((bq, 1), F32), pltpu.VMEM((bq, hw), F32)],
        input_output_aliases={0: 0} if aliased else {},
        compiler_params=_params(("parallel", "parallel", "parallel", "arbitrary")),
        name=name,
    )(*args)


def _kv_block(s_len):
    for bk in (3328, 1280, 640, 256, 128):
        if s_len % bk == 0:
            return bk
    raise ValueError(f"unsupported stream length {s_len}")


def _merge_kernel(x_ref, mod_ref, g_ref, gr_ref, of_ref, ob_ref, osw_ref, om_ref, gw_ref, wb_ref, wo_ref,
                  o_ref, *, n_batch, t_lat, tr):
    b = pl.program_id(0)
    row0 = pl.program_id(1) * tr
    og = of_ref[0].astype(F32) + ob_ref[0].astype(F32)
    gw = gw_ref[...]
    heads = []
    for h in range(GLA_HEADS):
        heads.append(_rms(og[:, h * GLA_HEAD_V:(h + 1) * GLA_HEAD_V], gw))
    gr = gr_ref[0].astype(F32)
    o_gla = (jnp.concatenate(heads, axis=1) * (gr * jax.nn.sigmoid(gr))).astype(BF16)
    branches = (o_gla, osw_ref[0], om_ref[0])
    merged = None
    for idx, ob in enumerate(branches):
        gate = jax.nn.sigmoid(g_ref[0, :, idx * D_MODEL:(idx + 1) * D_MODEL].astype(F32))
        term = gate * jnp.dot(ob, wb_ref[idx], preferred_element_type=F32)
        merged = term if merged is None else merged + term
    y = jnp.dot(merged.astype(BF16), wo_ref[...], preferred_element_type=F32)
    g_m = _mod_rows(mod_ref, b, n_batch, row0, tr, t_lat, 2)
    o_ref[0] = x_ref[0] + g_m * y


def _merge_call(xa, mod, pa, o_f, o_b, o_swa, o_mla, gw, wb, wo, *, layer, t_lat):
    n_batch, s_len, _ = xa.shape
    tr = 320 if s_len % 320 == 0 else _row_tile(s_len)
    row = lambda width, col: pl.BlockSpec((1, tr, width), lambda b, i: (b, i, col))
    return pl.pallas_call(
        functools.partial(_merge_kernel, n_batch=n_batch, t_lat=t_lat, tr=tr),
        grid=(n_batch, s_len // tr),
        in_specs=[
            row(D_MODEL, 0),
            _layer_spec(layer, mod.shape[1:], lambda b, i: (0, 0)),
            row(3 * D_MODEL, PA_GATES // (3 * D_MODEL)),
            row(D_MODEL, PA_GR // D_MODEL),
            row(D_MODEL, 0), row(D_MODEL, 0), row(D_MODEL, 0), row(D_MODEL, 0),
            _layer_spec(layer, (1, GLA_HEAD_V), lambda b, i: (0, 0)),
            _layer_spec(layer, (3, D_MODEL, D_MODEL), lambda b, i: (0, 0, 0)),
            _layer_spec(layer, (D_MODEL, D_MODEL), lambda b, i: (0, 0)),
        ],
        out_specs=row(D_MODEL, 0),
        out_shape=jax.ShapeDtypeStruct(xa.shape, F32),
        compiler_params=_params(("parallel", "parallel")),
        name="merge",
    )(xa, mod, pa, pa, o_f, o_b, o_swa, o_mla, gw, wb, wo)


def _mlp_kernel(x_ref, mod_ref, nw_ref, w1_ref, w2_ref, fw_ref, o_ref, h_ref, acc_ref,
                *, n_batch, t_lat, tr, final):
    b = pl.program_id(0)
    row0 = pl.program_id(1) * tr
    c = pl.program_id(2)

    @pl.when(c == 0)
    def _():
        shift = _mod_rows(mod_ref, b, n_batch, row0, tr, t_lat, 3)
        scale = _mod_rows(mod_ref, b, n_batch, row0, tr, t_lat, 4)
        h_ref[...] = (_rms(x_ref[0], nw_ref[...]) * (1.0 + scale) + shift).astype(BF16)
        acc_ref[...] = jnp.zeros_like(acc_ref)

    halves = [slice(0, tr // 2), slice(tr // 2, tr)]
    ups = [jnp.maximum(jnp.dot(h_ref[rows, :], w1_ref[...], preferred_element_type=F32), 0.0) for rows in halves]
    for rows, u in zip(halves, ups):
        acc_ref[rows, :] += jnp.dot((u * u).astype(BF16), w2_ref[...], preferred_element_type=F32)

    @pl.when(c == pl.num_programs(2) - 1)
    def _():
        g_f = _mod_rows(mod_ref, b, n_batch, row0, tr, t_lat, 5)
        y = x_ref[0] + g_f * acc_ref[...]
        if final:
            y = _rms(y, fw_ref[...])
        o_ref[0] = y


def _mlp_call(xa, mod, nw, w1, w2, fw, *, layer, t_lat, final):
    n_batch, s_len, _ = xa.shape
    rows = t_lat if final else s_len
    tr = _row_tile(rows)
    return pl.pallas_call(
        functools.partial(_mlp_kernel, n_batch=n_batch, t_lat=t_lat, tr=tr, final=final),
        grid=(n_batch, rows // tr, FF_DIM // FF_CHUNK),
        in_specs=[
            pl.BlockSpec((1, tr, D_MODEL), lambda b, i, c: (b, i, 0)),
            _layer_spec(layer, mod.shape[1:], lambda b, i, c: (0, 0)),
            _layer_spec(layer, (1, D_MODEL), lambda b, i, c: (0, 0)),
            _layer_spec(layer, (D_MODEL, FF_CHUNK), lambda b, i, c: (0, c)),
            _layer_spec(layer, (FF_CHUNK, D_MODEL), lambda b, i, c: (c, 0)),
            pl.BlockSpec((1, D_MODEL), lambda b, i, c: (0, 0)),
        ],
        out_specs=pl.BlockSpec((1, tr, D_MODEL), lambda b, i, c: (b, i, 0)),
        out_shape=jax.ShapeDtypeStruct((n_batch, rows, D_MODEL), F32),
        scratch_shapes=[pltpu.VMEM((tr, D_MODEL), BF16), pltpu.VMEM((tr, D_MODEL), F32)],
        compiler_params=_params(("parallel", "parallel", "arbitrary")),
        name="mlp",
    )(xa, mod, nw, w1, w2, fw)


def _rope_tables(t_lat, l_ctx, dim):
    f32 = np.float32
    pos = np.arange(t_lat)
    row, col = pos // GRID_W, pos % GRID_W
    d_axis = dim // 2
    inv = (f32(ROPE_BASE) ** (-np.arange(0, d_axis, 2, dtype=f32) / f32(d_axis))).astype(f32)
    ang_r = row.astype(f32)[:, None] * inv
    ang_c = col.astype(f32)[:, None] * inv
    cos = np.concatenate([np.cos(ang_r)] * 2 + [np.cos(ang_c)] * 2, axis=1)
    sin = np.concatenate([-np.sin(ang_r), np.sin(ang_r), -np.sin(ang_c), np.sin(ang_c)], axis=1)
    if dim < LANE:
        cos = np.concatenate([cos, np.ones((t_lat, LANE - dim), f32)], axis=1)
        sin = np.concatenate([sin, np.zeros((t_lat, LANE - dim), f32)], axis=1)
    cos = np.concatenate([cos, np.ones((l_ctx, LANE), f32)], axis=0)
    sin = np.concatenate([sin, np.zeros((l_ctx, LANE), f32)], axis=0)
    return jnp.asarray(cos, F32), jnp.asarray(sin, F32)


def _split_w_in(w):
    sizes = (512, 512, 1024, 1024, GLA_RANK, GLA_RANK, 1024, 256, 256, MLA_Q_RANK, MLA_KV_RANK, MLA_ROPE,
             3 * D_MODEL)
    offs = np.cumsum((0,) + sizes)
    return [w[..., offs[i]:offs[i + 1]] for i in range(len(sizes))]


def _layer_weights(w_in, gla_a_w, gla_a_b, mla_q_norm_w, mla_w_uq, mla_kv_norm_w, mla_w_ukv, swa_sink):
    depth = w_in.shape[0]
    gq, gk, gv, gr, gzf, gzb, sq, sk, sv, mcq, mckv, mkr, gates = _split_w_in(w_in)
    zpad = lambda n: jnp.zeros((depth, D_MODEL, n), F32)
    w_plain = jnp.concatenate(
        [gates, gr, gv, gq, gk, mcq, zpad(MCQ_PAD - MLA_Q_RANK), sv, mckv, gzf, gzb, zpad(LANE - 2 * GLA_RANK)],
        axis=-1).astype(BF16)
    w_rope = jnp.concatenate([sq, sk, mkr, zpad(LANE - MLA_ROPE)], axis=-1).astype(BF16)
    aw = jnp.zeros((depth, 2, LANE, GLA_HEADS * GLA_HEAD_K), F32)
    aw = aw.at[:, 0, :GLA_RANK].set(gla_a_w[:, 0]).at[:, 1, GLA_RANK:2 * GLA_RANK].set(gla_a_w[:, 1])
    aw = aw.astype(BF16)
    ab = gla_a_b.reshape(depth, 2, 1, GLA_HEADS * GLA_HEAD_K)
    wq = mla_w_uq.reshape(depth, MLA_Q_RANK, MLA_HEADS, MLA_NOPE + MLA_ROPE)
    wq = jnp.pad(wq, ((0, 0), (0, MCQ_PAD - MLA_Q_RANK), (0, 0), (0, 2 * LANE - MLA_NOPE - MLA_ROPE)))
    wq = wq.reshape(depth, MCQ_PAD, MLA_HEADS * 2 * LANE).astype(BF16)
    qnw = jnp.pad(mla_q_norm_w, ((0, 0), (0, MCQ_PAD - MLA_Q_RANK))).reshape(depth, 1, MCQ_PAD)
    wkv = mla_w_ukv.reshape(depth, MLA_KV_RANK, MLA_HEADS, MLA_NOPE + MLA_V)
    wkv = jnp.concatenate([wkv[..., :MLA_NOPE].reshape(depth, MLA_KV_RANK, -1),
                           wkv[..., MLA_NOPE:].reshape(depth, MLA_KV_RANK, -1)], axis=-1).astype(BF16)
    kvnw = mla_kv_norm_w.reshape(depth, 1, MLA_KV_RANK)
    sink2 = jnp.broadcast_to((swa_sink * LOG2E).reshape(depth, SWA_KV_HEADS, SWA_GROUP, 1),
                             (depth, SWA_KV_HEADS, SWA_GROUP, LANE))
    return w_plain, w_rope, aw, ab, wq, qnw, wkv, kvnw, sink2


def kernel(x, c, ctx, c_ctx, ada_w, ada_b, norm_mix_w, w_in, gla_a_w, gla_a_b, gla_norm_w, swa_sink,
           mla_q_norm_w, mla_w_uq, mla_kv_norm_w, mla_w_ukv, w_branch, w_out, norm_mlp_w, mlp_w1, mlp_w2,
           final_norm_w):
    n_batch, t_lat, d_model = x.shape
    l_ctx = ctx.shape[1]
    depth = ada_w.shape[0]
    s_len = t_lat + l_ctx
    assert d_model == D_MODEL and n_batch + 1 <= MOD_ROWS
    assert t_lat % GLA_BLOCK == 0 and l_ctx % GLA_BLOCK == 0 and t_lat % l_ctx == 0 and t_lat % GRID_W == 0

    xa = jnp.concatenate([x, ctx], axis=1)
    cin = jnp.zeros((MOD_ROWS, D_MODEL), F32).at[:n_batch].set(c).at[n_batch].set(c_ctx)
    mod_all = _ada_call(cin, ada_w, ada_b)

    cos_s, sin_s = _rope_tables(t_lat, l_ctx, SWA_HEAD_DIM)
    cos_m, sin_m = _rope_tables(t_lat, l_ctx, MLA_ROPE)
    tabs = (cos_s, sin_s, cos_m, sin_m)

    bq = min(1024, t_lat)
    bk = _kv_block(s_len)

    (w_plain, w_rope, aw, ab, wq, qnw, wkv, kvnw, sink2) = _layer_weights(
        w_in, gla_a_w, gla_a_b, mla_q_norm_w, mla_w_uq, mla_kv_norm_w, mla_w_ukv, swa_sink)
    row_vec = lambda v: v.reshape(depth, 1, -1)
    nw_mix, nw_mlp, gw = row_vec(norm_mix_w), row_vec(norm_mlp_w), row_vec(gla_norm_w)
    wb, wo, w1, w2 = (w.astype(BF16) for w in (w_branch, w_out, mlp_w1, mlp_w2))
    mod = mod_all

    for layer in range(depth):
        pa = _proj_call(xa, mod, nw_mix, w_plain, None, layer=layer, t_lat=t_lat, bn=PA_BN, rope=False,
                        name="proj_plain")
        pr = _proj_call(xa, mod, nw_mix, w_rope, tabs, layer=layer, t_lat=t_lat, bn=PR_WIDTH, rope=True,
                        name="proj_rope")

        o_f, o_b = _gla_call(pa, aw, ab, layer=layer, t_lat=t_lat)
        o_swa = _swa_call(pr, pa, sink2, layer=layer, t_lat=t_lat)

        q, k, v = _mla_proj_call(pa, pr, qnw, kvnw, wq, wkv, cos_m, sin_m, layer=layer)
        o_mla = _flash_call(q, k, v, None, bq=bq, bk=s_len, kv_sub=bk, q_blk0=0, n_q=t_lat // bq, kv_blk0=0,
                            n_kv=1, name="mla_flash")
        o_mla = _flash_call(q, k, v, o_mla, bq=l_ctx, bk=l_ctx, kv_sub=l_ctx, q_blk0=t_lat // l_ctx, n_q=1,
                            kv_blk0=t_lat // l_ctx, n_kv=1, name="mla_flash_ctx")

        xa = _merge_call(xa, mod, pa, o_f, o_b, o_swa, o_mla, gw, wb, wo, layer=layer, t_lat=t_lat)
        xa = _mlp_call(xa, mod, nw_mlp, w1, w2, final_norm_w.reshape(1, -1), layer=layer, t_lat=t_lat,
                       final=layer == depth - 1)
    return xa
```

```python
import functools

import jax
import jax.numpy as jnp
import numpy as np
from jax import lax
from jax.experimental import pallas as pl
from jax.experimental.pallas import tpu as pltpu

F32 = jnp.float32
BF16 = jnp.bfloat16

D_MODEL = 1024
EPS = 1e-6
ROPE_BASE = 10000.0
NEG_INF = -1e30
LOG2E = 1.4426950408889634
GRID_W = 64

GLA_HEADS = 4
GLA_HEAD_K = 128
GLA_HEAD_V = 256
GLA_RANK = 16
GLA_GATE_NORM = 16.0
GLA_CHUNK = 64
GLA_BLOCK = 256

SWA_HEADS = 8
SWA_KV_HEADS = 2
SWA_GROUP = 4
SWA_HEAD_DIM = 128
SWA_WINDOW = 128
SWA_BLOCK = 128
SWA_SCALE = SWA_HEAD_DIM ** -0.5
SWA_QTILE = 512

MLA_HEADS = 8
MLA_Q_RANK = 384
MLA_KV_RANK = 256
MLA_NOPE = 128
MLA_ROPE = 64
MLA_V = 128
MLA_SCALE = (MLA_NOPE + MLA_ROPE) ** -0.5

FF_DIM = 4 * D_MODEL
FF_CHUNK = 2048
FLASH_SUB = 256

LANE = 128
MOD_ROWS = 8
VMEM_LIMIT = 56 * 1024 * 1024

PA_GATES, PA_GR, PA_GV, PA_GQ, PA_GK = 0, 3072, 4096, 5120, 5632
PA_MCQ, PA_SV, PA_MCKV, PA_Z = 6144, 6656, 6912, 7168
PA_WIDTH = 7296
PA_BN = 2432
MCQ_PAD = 512
PR_SQ, PR_SK, PR_KPE = 0, 1024, 1280
PR_WIDTH = 1408
PR_SLABS = PR_WIDTH // LANE


def _params(sem):
    return pltpu.CompilerParams(dimension_semantics=sem, vmem_limit_bytes=VMEM_LIMIT)


def _layer_spec(layer, block, index_map):
    return pl.BlockSpec((None,) + tuple(block), lambda *g: (layer,) + tuple(index_map(*g)))


def _row_tile(s):
    for tr in (640, 512, 256, 128):
        if s % tr == 0:
            return tr
    raise ValueError(f"unsupported stream length {s}")


def _rms(x, w):
    return x * lax.rsqrt(jnp.mean(x * x, axis=-1, keepdims=True) + EPS) * w


def _mod_rows(mod_ref, b, n_batch, row0, rows, t_lat, idx):
    lo = idx * D_MODEL
    lat = mod_ref[pl.ds(b, 1), lo:lo + D_MODEL]
    ctx = mod_ref[n_batch:n_batch + 1, lo:lo + D_MODEL]
    rid = row0 + lax.broadcasted_iota(jnp.int32, (rows, 1), 0)
    return jnp.where(rid >= t_lat, ctx, lat)


def _swap_halves(x, half):
    lane = lax.broadcasted_iota(jnp.int32, x.shape, 1)
    return jnp.where((lane & half) == 0, pltpu.roll(x, LANE - half, 1), pltpu.roll(x, half, 1))


def _log_sigmoid(x):
    return jnp.minimum(x, 0.0) - jnp.log1p(jnp.exp(-jnp.abs(x)))


def _ada_kernel(c_ref, w_ref, b_ref, o_ref):
    c = c_ref[...]
    a = (c * jax.nn.sigmoid(c)).astype(BF16)
    o_ref[0] = jnp.dot(a, w_ref[0].astype(BF16), preferred_element_type=F32) + b_ref[0]


def _ada_call(cin, ada_w, ada_b):
    depth = ada_w.shape[0]
    nblk = ada_w.shape[2] // D_MODEL
    return pl.pallas_call(
        _ada_kernel,
        grid=(depth, nblk),
        in_specs=[
            pl.BlockSpec((MOD_ROWS, D_MODEL), lambda l, j: (0, 0)),
            pl.BlockSpec((1, D_MODEL, D_MODEL), lambda l, j: (l, 0, j)),
            pl.BlockSpec((1, 1, D_MODEL), lambda l, j: (l, 0, j)),
        ],
        out_specs=pl.BlockSpec((1, MOD_ROWS, D_MODEL), lambda l, j: (l, 0, j)),
        out_shape=jax.ShapeDtypeStruct((depth, MOD_ROWS, ada_w.shape[2]), F32),
        compiler_params=_params(("parallel", "parallel")),
        name="ada_mod",
    )(cin, ada_w, ada_b.reshape(depth, 1, -1))


def _proj_kernel(x_ref, mod_ref, nw_ref, w_ref, *rest, n_batch, t_lat, tr, rope):
    o_ref = rest[-1]
    b = pl.program_id(1)
    row0 = pl.program_id(2) * tr
    x = x_ref[0]
    shift = _mod_rows(mod_ref, b, n_batch, row0, tr, t_lat, 0)
    scale = _mod_rows(mod_ref, b, n_batch, row0, tr, t_lat, 1)
    h = _rms(x, nw_ref[...]) * (1.0 + scale) + shift
    acc = jnp.dot(h.astype(BF16), w_ref[...], preferred_element_type=F32)
    if not rope:
        o_ref[0] = acc.astype(BF16)
        return
    cs_ref, sn_ref, cm_ref, sm_ref = rest[:4]
    cos_s, sin_s = cs_ref[...], sn_ref[...]
    for s in range(PR_SLABS):
        a = acc[:, s * LANE:(s + 1) * LANE]
        if s * LANE < PR_KPE:
            r = a * cos_s + _swap_halves(a, SWA_HEAD_DIM // 4) * sin_s
            if s * LANE < PR_SK:
                r = r * (SWA_SCALE * LOG2E)
        else:
            r = a * cm_ref[...] + _swap_halves(a, MLA_ROPE // 4) * sm_ref[...]
        o_ref[0, :, s * LANE:(s + 1) * LANE] = r.astype(BF16)


def _proj_call(xa, mod, nw, w, tabs, *, layer, t_lat, bn, rope, name):
    n_batch, s_len, _ = xa.shape
    tr = _row_tile(s_len)
    width = w.shape[-1]
    grid = (width // bn, n_batch, s_len // tr)
    in_specs = [
        pl.BlockSpec((1, tr, D_MODEL), lambda j, b, i: (b, i, 0)),
        _layer_spec(layer, mod.shape[1:], lambda j, b, i: (0, 0)),
        _layer_spec(layer, (1, D_MODEL), lambda j, b, i: (0, 0)),
        _layer_spec(layer, (D_MODEL, bn), lambda j, b, i: (0, j)),
    ]
    args = [xa, mod, nw, w]
    if rope:
        in_specs += [pl.BlockSpec((tr, LANE), lambda j, b, i: (i, 0))] * 4
        args += list(tabs)
    return pl.pallas_call(
        functools.partial(_proj_kernel, n_batch=n_batch, t_lat=t_lat, tr=tr, rope=rope),
        grid=grid,
        in_specs=in_specs,
        out_specs=pl.BlockSpec((1, tr, bn), lambda j, b, i: (b, i, j)),
        out_shape=jax.ShapeDtypeStruct((n_batch, s_len, width), BF16),
        compiler_params=_params(("parallel", "parallel", "parallel")),
        name=name,
    )(*args)


def _gla_mask(reverse):
    gb, ch = GLA_BLOCK, GLA_CHUNK
    r = lax.broadcasted_iota(jnp.int32, (gb, gb), 0)
    c = lax.broadcasted_iota(jnp.int32, (gb, gb), 1)
    return ((r // ch) == (c // ch)) & ((c >= r) if reverse else (c <= r))


def _gla_log_decay(z_ref, aw, ab):
    la = jnp.dot(z_ref[0], aw, preferred_element_type=F32) + ab
    la = _log_sigmoid(la) * (1.0 / GLA_GATE_NORM)
    hi = la.astype(BF16)
    lo = (la - hi.astype(F32)).astype(BF16)
    return jnp.concatenate([hi, lo], axis=1)


def _gla_cumulate(hi_lo, reverse):
    tmat = jnp.where(_gla_mask(reverse), 1.0, 0.0).astype(BF16)
    hw = GLA_HEADS * GLA_HEAD_K
    res = jnp.dot(tmat, hi_lo, preferred_element_type=F32)
    return res[:, :hw] + res[:, hw:]


def _gla_decay(z_ref, aw, ab, reverse):
    return _gla_cumulate(_gla_log_decay(z_ref, aw, ab), reverse)


def _gla_kernel(qf_ref, kf_ref, vf_ref, zf_ref, zfn_ref, qb_ref, kb_ref, vb_ref, zb_ref, zbn_ref, aw_ref, ab_ref,
                of_ref, ob_ref, stf_ref, stb_ref, bc_ref):
    n = pl.program_id(1)

    @pl.when(n == 0)
    def _():
        stf_ref[...] = jnp.zeros_like(stf_ref)
        stb_ref[...] = jnp.zeros_like(stb_ref)
        bc_ref[0] = _gla_decay(zf_ref, aw_ref[0], ab_ref[0], False)
        bc_ref[1] = _gla_decay(zb_ref, aw_ref[1], ab_ref[1], True)

    gb, ch, dk, dv = GLA_BLOCK, GLA_CHUNK, GLA_HEAD_K, GLA_HEAD_V
    n_ch = gb // ch
    nt = (((1,), (1,)), ((), ()))
    dirs = ((qf_ref, kf_ref, vf_ref, zfn_ref, of_ref, stf_ref, False),
            (qb_ref, kb_ref, vb_ref, zbn_ref, ob_ref, stb_ref, True))
    fronts = [(bc_ref[d], _gla_mask(rev)) for d, (_, _, _, _, _, _, rev) in enumerate(dirs)]
    upcoming = [_gla_log_decay(z_next_ref, aw_ref[d], ab_ref[d]) for d, (_, _, _, z_next_ref, _, _, _) in enumerate(dirs)]

    chains = []
    for (q_ref, k_ref, v_ref, _, o_ref, st_ref, rev), (bcum_all, mask) in zip(dirs, fronts):
        order = range(n_ch - 1, -1, -1) if rev else range(n_ch)
        for h in range(GLA_HEADS):
            bcum = bcum_all[:, h * dk:(h + 1) * dk]
            q = q_ref[0, :, h * dk:(h + 1) * dk].astype(F32)
            k = k_ref[0, :, h * dk:(h + 1) * dk].astype(F32)
            q_dec = (q * (dk ** -0.5) * jnp.exp(bcum)).astype(BF16)
            k_inv = (k * jnp.exp(-bcum)).astype(BF16)
            v = v_ref[0, :, h * dv:(h + 1) * dv]
            a = lax.dot_general(q_dec, k_inv, nt, preferred_element_type=F32)
            tots, incs = [], []
            for cidx in order:
                lo_r, hi_r = cidx * ch, (cidx + 1) * ch
                last = lo_r if rev else hi_r - 1
                tot = bcum[last:last + 1, :]
                k_end = k[lo_r:hi_r] * jnp.exp(tot - bcum[lo_r:hi_r])
                incs.append(jnp.dot(k_end.T.astype(BF16), v[lo_r:hi_r], preferred_element_type=F32))
                tots.append(tot)
            pad = jnp.zeros((8 - n_ch, dk), F32)
            g_cols = jnp.exp(jnp.concatenate(tots + [pad], axis=0)).T
            chains.append((o_ref, st_ref, h, order, mask, q_dec, v, a, incs, g_cols))

    upcoming = [_gla_cumulate(hi_lo, rev) for hi_lo, (_, _, _, _, _, _, rev) in zip(upcoming, dirs)]

    stage2 = []
    for o_ref, st_ref, h, order, mask, q_dec, v, a, incs, g_cols in chains:
        a = jnp.where(mask, a, 0.0).astype(BF16)
        st = st_ref[h]
        entering = []
        for i, _ in enumerate(order):
            entering.append(st.astype(BF16))
            st = st * g_cols[:, i:i + 1] + incs[i]
        st_ref[h] = st
        stage2.append((o_ref, h, order, q_dec, v, a, entering))

    for o_ref, h, order, q_dec, v, a, entering in stage2:
        o_intra = jnp.dot(a, v, preferred_element_type=F32)
        for i, cidx in enumerate(order):
            lo_r, hi_r = cidx * ch, (cidx + 1) * ch
            o = o_intra[lo_r:hi_r] + jnp.dot(q_dec[lo_r:hi_r], entering[i], preferred_element_type=F32)
            o_ref[0, lo_r:hi_r, h * dv:(h + 1) * dv] = o.astype(BF16)

    for d, nxt in enumerate(upcoming):
        bc_ref[d] = nxt


def _gla_call(pa, aw, ab, *, layer, t_lat):
    n_batch, s_len, _ = pa.shape
    gb = GLA_BLOCK
    n_lat, n_ctx = t_lat // gb, (s_len - t_lat) // gb
    nblk = n_lat + n_ctx
    qk_w, v_w = GLA_HEADS * GLA_HEAD_K, GLA_HEADS * GLA_HEAD_V

    fwd = lambda n: jnp.where(n < n_ctx, n_lat + n, n - n_ctx)
    bwd = lambda n: jnp.where(n < n_ctx, n_lat + n_ctx - 1 - n, n_lat - 1 - (n - n_ctx))

    def specs(blk):
        return [
            pl.BlockSpec((1, gb, qk_w), lambda b, n: (b, blk(n), PA_GQ // qk_w)),
            pl.BlockSpec((1, gb, qk_w), lambda b, n: (b, blk(n), PA_GK // qk_w)),
            pl.BlockSpec((1, gb, v_w), lambda b, n: (b, blk(n), PA_GV // v_w)),
            pl.BlockSpec((1, gb, LANE), lambda b, n: (b, blk(n), PA_Z // LANE)),
            pl.BlockSpec((1, gb, LANE), lambda b, n: (b, blk(jnp.minimum(n + 1, nblk - 1)), PA_Z // LANE)),
        ]

    out = jax.ShapeDtypeStruct((n_batch, s_len, v_w), BF16)
    state = pltpu.VMEM((GLA_HEADS, GLA_HEAD_K, GLA_HEAD_V), F32)
    decay = pltpu.VMEM((2, gb, qk_w), F32)
    return pl.pallas_call(
        _gla_kernel,
        grid=(n_batch, nblk),
        in_specs=specs(fwd) + specs(bwd) + [
            _layer_spec(layer, aw.shape[1:], lambda b, n: (0, 0, 0)),
            _layer_spec(layer, ab.shape[1:], lambda b, n: (0, 0, 0)),
        ],
        out_specs=[pl.BlockSpec((1, gb, v_w), lambda b, n: (b, fwd(n), 0)),
                   pl.BlockSpec((1, gb, v_w), lambda b, n: (b, bwd(n), 0))],
        out_shape=[out, out],
        scratch_shapes=[state, state, decay],
        compiler_params=_params(("parallel", "arbitrary")),
        name="gla",
    )(pa, pa, pa, pa, pa, pa, pa, pa, pa, pa, aw, ab)


def _with_one_hot(v):
    one_hot = (lax.broadcasted_iota(jnp.int32, v.shape, 1) == 0).astype(v.dtype)
    return jnp.concatenate([v, one_hot], axis=1)


def _swa_kernel(*refs, local, aliased):
    if aliased:
        refs = refs[1:]
    if local:
        q_ref, kp_ref, kc_ref, kn_ref, vp_ref, vc_ref, vn_ref, kx_ref, vx_ref, sink_ref, band_ref, o_ref = refs
    else:
        q_ref, kx_ref, vx_ref, sink_ref, o_ref = refs
    nt = (((1,), (1,)), ((), ()))
    if local:
        n = pl.program_id(2)
        blk = SWA_BLOCK
        keys = jnp.concatenate([kp_ref[0], kc_ref[0], kn_ref[0], kx_ref[0]], axis=0)
        vals = _with_one_hot(jnp.concatenate([vp_ref[0], vc_ref[0], vn_ref[0], vx_ref[0]], axis=0))
        n_loc = kp_ref.shape[1] + kc_ref.shape[1] + kn_ref.shape[1]
        col = lax.broadcasted_iota(jnp.int32, (1, keys.shape[0]), 1)
        edge = jnp.where((col < blk) & (n == 0), NEG_INF, 0.0)
        edge = jnp.where((col >= n_loc - blk) & (col < n_loc) & (n == pl.num_programs(2) - 1), NEG_INF, edge)
        bias = band_ref[...] + edge
    else:
        keys = kx_ref[0]
        vals = _with_one_hot(vx_ref[0])
        bias = None
    heads = range(SWA_GROUP)
    qs = [q_ref[0, :, g * LANE:(g + 1) * LANE] for g in heads]
    scores = [lax.dot_general(q, keys, nt, preferred_element_type=F32) for q in qs]
    if local:
        scores = [s + bias for s in scores]
    stage2 = []
    for g in heads:
        sink = sink_ref[0, g:g + 1, 0:1]
        m = jnp.maximum(jnp.max(scores[g], axis=1, keepdims=True), sink)
        stage2.append((jnp.exp2(scores[g] - m).astype(BF16), jnp.exp2(sink - m)))
    for g, (p, p_sink) in enumerate(stage2):
        o = jnp.dot(p, vals, preferred_element_type=F32)
        den = o[:, SWA_HEAD_DIM:SWA_HEAD_DIM + 1] + p_sink
        o_ref[0, :, g * LANE:(g + 1) * LANE] = (o[:, :SWA_HEAD_DIM] / den).astype(BF16)


def _swa_band_table(bq, l_ctx):
    r = np.arange(bq)[:, None]
    c = np.arange(bq + 2 * SWA_BLOCK)[None, :] - SWA_BLOCK
    band = np.where(np.abs(r - c) <= SWA_WINDOW, 0.0, NEG_INF)
    return jnp.asarray(np.concatenate([band, np.zeros((bq, l_ctx))], axis=1), F32)


def _swa_call(pr, pa, sink2, *, layer, t_lat):
    n_batch, s_len, _ = pr.shape
    blk = SWA_BLOCK
    l_ctx = s_len - t_lat
    bq = SWA_QTILE if t_lat % SWA_QTILE == 0 else blk
    per = bq // blk
    n_blk = t_lat // blk
    ctx_blk = t_lat // l_ctx
    gw = SWA_GROUP * LANE
    kcol = lambda kh: PR_SK // LANE + kh
    vcol = lambda kh: PA_SV // LANE + kh
    prev = lambda n: jnp.maximum(n * per - 1, 0)
    nxt = lambda n: jnp.minimum((n + 1) * per, n_blk - 1)
    edge = lambda col, pos: pl.BlockSpec((1, blk, LANE), lambda b, kh, n: (b, pos(n), col(kh)))
    body = lambda col: pl.BlockSpec((1, bq, LANE), lambda b, kh, n: (b, n, col(kh)))
    ctx = lambda col: pl.BlockSpec((1, l_ctx, LANE), lambda b, kh, n: (b, ctx_blk, col(kh)))
    sink_spec = _layer_spec(layer, (1, SWA_GROUP, LANE), lambda b, kh, n: (kh, 0, 0))
    out_shape = jax.ShapeDtypeStruct((n_batch, s_len, SWA_HEADS * SWA_HEAD_DIM), BF16)
    band = _swa_band_table(bq, l_ctx)
    o_lat = pl.pallas_call(
        functools.partial(_swa_kernel, local=True, aliased=False),
        grid=(n_batch, SWA_KV_HEADS, t_lat // bq),
        in_specs=[
            pl.BlockSpec((1, bq, gw), lambda b, kh, n: (b, n, PR_SQ // gw + kh)),
            edge(kcol, prev), body(kcol), edge(kcol, nxt),
            edge(vcol, prev), body(vcol), edge(vcol, nxt),
            ctx(kcol), ctx(vcol), sink_spec,
            pl.BlockSpec(band.shape, lambda b, kh, n: (0, 0)),
        ],
        out_specs=pl.BlockSpec((1, bq, gw), lambda b, kh, n: (b, n, kh)),
        out_shape=out_shape,
        compiler_params=_params(("parallel", "parallel", "parallel")),
        name="swa",
    )(pr, pr, pr, pr, pa, pa, pa, pr, pa, sink2, band)
    return pl.pallas_call(
        functools.partial(_swa_kernel, local=False, aliased=True),
        grid=(n_batch, SWA_KV_HEADS, 1),
        in_specs=[
            pl.BlockSpec(memory_space=pl.ANY),
            pl.BlockSpec((1, l_ctx, gw), lambda b, kh, n: (b, ctx_blk, PR_SQ // gw + kh)),
            ctx(kcol), ctx(vcol), sink_spec,
        ],
        out_specs=pl.BlockSpec((1, l_ctx, gw), lambda b, kh, n: (b, ctx_blk, kh)),
        out_shape=out_shape,
        input_output_aliases={0: 0},
        compiler_params=_params(("parallel", "parallel", "parallel")),
        name="swa_ctx",
    )(o_lat, pr, pr, pa, sink2)


def _mla_proj_kernel(cq_ref, ckv_ref, kpe_ref, qnw_ref, kvnw_ref, wq_ref, wkv_ref, cm_ref, sm_ref,
                     q_ref, k_ref, v_ref):
    cq = cq_ref[0].astype(F32)
    ms = jnp.sum(cq * cq, axis=-1, keepdims=True) * (1.0 / MLA_Q_RANK)
    cqn = (cq * lax.rsqrt(ms + EPS) * qnw_ref[...]).astype(BF16)
    q = jnp.dot(cqn, wq_ref[...], preferred_element_type=F32)
    cos_m, sin_m = cm_ref[...], sm_ref[...]
    qs = MLA_SCALE * LOG2E
    hw = 2 * LANE
    for h in range(MLA_HEADS):
        q_ref[0, :, h * hw:h * hw + LANE] = (q[:, h * hw:h * hw + LANE] * qs).astype(BF16)
        pe = q[:, h * hw + LANE:(h + 1) * hw]
        pe = (pe * cos_m + _swap_halves(pe, MLA_ROPE // 4) * sin_m) * qs
        q_ref[0, :, h * hw + LANE:(h + 1) * hw] = pe.astype(BF16)
    ckv = _rms(ckv_ref[0].astype(F32), kvnw_ref[...]).astype(BF16)
    kv = jnp.dot(ckv, wkv_ref[...], preferred_element_type=F32)
    half = MLA_HEADS * MLA_NOPE
    kpe = kpe_ref[0]
    one_hot = (lax.broadcasted_iota(jnp.int32, kpe.shape, 1) == 0).astype(BF16)
    for h in range(MLA_HEADS):
        k_ref[0, :, h * hw:h * hw + LANE] = kv[:, h * LANE:(h + 1) * LANE].astype(BF16)
        k_ref[0, :, h * hw + LANE:(h + 1) * hw] = kpe
        v_ref[0, :, h * hw:h * hw + LANE] = kv[:, half + h * LANE:half + (h + 1) * LANE].astype(BF16)
        v_ref[0, :, h * hw + LANE:(h + 1) * hw] = one_hot


def _mla_proj_call(pa, pr, qnw, kvnw, wq, wkv, cos_m, sin_m, *, layer):
    n_batch, s_len, _ = pa.shape
    tr = _row_tile(s_len)
    qw = MLA_HEADS * 2 * LANE
    kw = MLA_HEADS * MLA_NOPE
    const = lambda shape: _layer_spec(layer, shape, lambda b, i: (0,) * len(shape))
    wide = pl.BlockSpec((1, tr, qw), lambda b, i: (b, i, 0))
    return pl.pallas_call(
        _mla_proj_kernel,
        grid=(n_batch, s_len // tr),
        in_specs=[
            pl.BlockSpec((1, tr, MCQ_PAD), lambda b, i: (b, i, PA_MCQ // MCQ_PAD)),
            pl.BlockSpec((1, tr, MLA_KV_RANK), lambda b, i: (b, i, PA_MCKV // MLA_KV_RANK)),
            pl.BlockSpec((1, tr, LANE), lambda b, i: (b, i, PR_KPE // LANE)),
            const((1, MCQ_PAD)), const((1, MLA_KV_RANK)),
            const((MCQ_PAD, qw)), const((MLA_KV_RANK, 2 * kw)),
            pl.BlockSpec((tr, LANE), lambda b, i: (i, 0)),
            pl.BlockSpec((tr, LANE), lambda b, i: (i, 0)),
        ],
        out_specs=[wide, wide, wide],
        out_shape=[jax.ShapeDtypeStruct((n_batch, s_len, qw), BF16)] * 3,
        compiler_params=_params(("parallel", "parallel")),
        name="mla_proj",
    )(pa, pa, pr, qnw, kvnw, wq, wkv, cos_m, sin_m)


def _flash_kernel(*refs, aliased, sub, kv_sub):
    if aliased:
        refs = refs[1:]
    q_ref, k_ref, v_ref, o_ref, m_ref, acc_ref = refs
    j = pl.program_id(3)

    @pl.when(j == 0)
    def _():
        m_ref[...] = jnp.full_like(m_ref, NEG_INF)
        acc_ref[...] = jnp.zeros_like(acc_ref)

    bq, bk = q_ref.shape[1], k_ref.shape[1]
    sub = min(sub, bq)
    kv_sub = min(kv_sub, bk)
    chains = [slice(c * sub, (c + 1) * sub) for c in range(bq // sub)]
    for t in range(bk // kv_sub):
        k = k_ref[0, t * kv_sub:(t + 1) * kv_sub, :]
        v = v_ref[0, t * kv_sub:(t + 1) * kv_sub, :]
        scores = [lax.dot_general(q_ref[0, rows, :], k, (((1,), (1,)), ((), ())), preferred_element_type=F32)
                  for rows in chains]
        probs, alphas = [], []
        for rows, s in zip(chains, scores):
            m_prev = m_ref[rows, :]
            m_new = jnp.maximum(m_prev, jnp.max(s, axis=1, keepdims=True))
            alphas.append(jnp.exp2(m_prev - m_new))
            probs.append(jnp.exp2(s - m_new).astype(BF16))
            m_ref[rows, :] = m_new
        for rows, p, alpha in zip(chains, probs, alphas):
            acc_ref[rows, :] = alpha * acc_ref[rows, :] + jnp.dot(p, v, preferred_element_type=F32)

    @pl.when(j == pl.num_programs(3) - 1)
    def _():
        acc = acc_ref[...]
        o_ref[0] = (acc[:, :MLA_V] / acc[:, MLA_V:MLA_V + 1]).astype(BF16)


def _flash_call(q, k, v, prev_out, *, bq, bk, kv_sub, q_blk0, n_q, kv_blk0, n_kv, name):
    n_batch, s_len, _ = q.shape
    aliased = prev_out is not None
    hw = 2 * LANE
    in_specs = [
        pl.BlockSpec((1, bq, hw), lambda b, h, i, j: (b, q_blk0 + i, h)),
        pl.BlockSpec((1, bk, hw), lambda b, h, i, j: (b, kv_blk0 + j, h)),
        pl.BlockSpec((1, bk, hw), lambda b, h, i, j: (b, kv_blk0 + j, h)),
    ]
    args = [q, k, v]
    if aliased:
        in_specs = [pl.BlockSpec(memory_space=pl.ANY)] + in_specs
        args = [prev_out] + args
    return pl.pallas_call(
        functools.partial(_flash_kernel, aliased=aliased, sub=FLASH_SUB, kv_sub=kv_sub),
        grid=(n_batch, MLA_HEADS, n_q, n_kv),
        in_specs=in_specs,
        out_specs=pl.BlockSpec((1, bq, LANE), lambda b, h, i, j: (b, q_blk0 + i, h)),
        out_shape=jax.ShapeDtypeStruct((n_batch, s_len, MLA_HEADS * MLA_V), BF16),
        scratch_shapes=[pltpu.VMEM((bq, 1), F32), pltpu.VMEM((bq, hw), F32)],
        input_output_aliases={0: 0} if aliased else {},
        compiler_params=_params(("parallel", "parallel", "parallel", "arbitrary")),
        name=name,
    )(*args)


def _kv_block(s_len):
    for bk in (3328, 1280, 640, 256, 128):
        if s_len % bk == 0:
            return bk
    raise ValueError(f"unsupported stream length {s_len}")


def _merge_kernel(x_ref, mod_ref, g_ref, gr_ref, of_ref, ob_ref, osw_ref, om_ref, gw_ref, wb_ref, wo_ref,
                  o_ref, *, n_batch, t_lat, tr):
    b = pl.program_id(0)
    row0 = pl.program_id(1) * tr
    og = of_ref[0].astype(F32) + ob_ref[0].astype(F32)
    gw = gw_ref[...]
    heads = []
    for h in range(GLA_HEADS):
        heads.append(_rms(og[:, h * GLA_HEAD_V:(h + 1) * GLA_HEAD_V], gw))
    gr = gr_ref[0].astype(F32)
    o_gla = (jnp.concatenate(heads, axis=1) * (gr * jax.nn.sigmoid(gr))).astype(BF16)
    branches = (o_gla, osw_ref[0], om_ref[0])
    merged = None
    for idx, ob in enumerate(branches):
        gate = jax.nn.sigmoid(g_ref[0, :, idx * D_MODEL:(idx + 1) * D_MODEL].astype(F32))
        term = gate * jnp.dot(ob, wb_ref[idx], preferred_element_type=F32)
        merged = term if merged is None else merged + term
    y = jnp.dot(merged.astype(BF16), wo_ref[...], preferred_element_type=F32)
    g_m = _mod_rows(mod_ref, b, n_batch, row0, tr, t_lat, 2)
    o_ref[0] = x_ref[0] + g_m * y


def _merge_call(xa, mod, pa, o_f, o_b, o_swa, o_mla, gw, wb, wo, *, layer, t_lat):
    n_batch, s_len, _ = xa.shape
    tr = 320 if s_len % 320 == 0 else _row_tile(s_len)
    row = lambda width, col: pl.BlockSpec((1, tr, width), lambda b, i: (b, i, col))
    return pl.pallas_call(
        functools.partial(_merge_kernel, n_batch=n_batch, t_lat=t_lat, tr=tr),
        grid=(n_batch, s_len // tr),
        in_specs=[
            row(D_MODEL, 0),
            _layer_spec(layer, mod.shape[1:], lambda b, i: (0, 0)),
            row(3 * D_MODEL, PA_GATES // (3 * D_MODEL)),
            row(D_MODEL, PA_GR // D_MODEL),
            row(D_MODEL, 0), row(D_MODEL, 0), row(D_MODEL, 0), row(D_MODEL, 0),
            _layer_spec(layer, (1, GLA_HEAD_V), lambda b, i: (0, 0)),
            _layer_spec(layer, (3, D_MODEL, D_MODEL), lambda b, i: (0, 0, 0)),
            _layer_spec(layer, (D_MODEL, D_MODEL), lambda b, i: (0, 0)),
        ],
        out_specs=row(D_MODEL, 0),
        out_shape=jax.ShapeDtypeStruct(xa.shape, F32),
        compiler_params=_params(("parallel", "parallel")),
        name="merge",
    )(xa, mod, pa, pa, o_f, o_b, o_swa, o_mla, gw, wb, wo)


def _mlp_kernel(x_ref, mod_ref, nw_ref, w1_ref, w2_ref, fw_ref, o_ref, h_ref, acc_ref,
                *, n_batch, t_lat, tr, final):
    b = pl.program_id(0)
    row0 = pl.program_id(1) * tr
    c = pl.program_id(2)

    @pl.when(c == 0)
    def _():
        shift = _mod_rows(mod_ref, b, n_batch, row0, tr, t_lat, 3)
        scale = _mod_rows(mod_ref, b, n_batch, row0, tr, t_lat, 4)
        h_ref[...] = (_rms(x_ref[0], nw_ref[...]) * (1.0 + scale) + shift).astype(BF16)
        acc_ref[...] = jnp.zeros_like(acc_ref)

    halves = [slice(0, tr // 2), slice(tr // 2, tr)]
    ups = [jnp.maximum(jnp.dot(h_ref[rows, :], w1_ref[...], preferred_element_type=F32), 0.0) for rows in halves]
    for rows, u in zip(halves, ups):
        acc_ref[rows, :] += jnp.dot((u * u).astype(BF16), w2_ref[...], preferred_element_type=F32)

    @pl.when(c == pl.num_programs(2) - 1)
    def _():
        g_f = _mod_rows(mod_ref, b, n_batch, row0, tr, t_lat, 5)
        y = x_ref[0] + g_f * acc_ref[...]
        if final:
            y = _rms(y, fw_ref[...])
        o_ref[0] = y


def _mlp_call(xa, mod, nw, w1, w2, fw, *, layer, t_lat, final):
    n_batch, s_len, _ = xa.shape
    rows = t_lat if final else s_len
    tr = _row_tile(rows)
    return pl.pallas_call(
        functools.partial(_mlp_kernel, n_batch=n_batch, t_lat=t_lat, tr=tr, final=final),
        grid=(n_batch, rows // tr, FF_DIM // FF_CHUNK),
        in_specs=[
            pl.BlockSpec((1, tr, D_MODEL), lambda b, i, c: (b, i, 0)),
            _layer_spec(layer, mod.shape[1:], lambda b, i, c: (0, 0)),
            _layer_spec(layer, (1, D_MODEL), lambda b, i, c: (0, 0)),
            _layer_spec(layer, (D_MODEL, FF_CHUNK), lambda b, i, c: (0, c)),
            _layer_spec(layer, (FF_CHUNK, D_MODEL), lambda b, i, c: (c, 0)),
            pl.BlockSpec((1, D_MODEL), lambda b, i, c: (0, 0)),
        ],
        out_specs=pl.BlockSpec((1, tr, D_MODEL), lambda b, i, c: (b, i, 0)),
        out_shape=jax.ShapeDtypeStruct((n_batch, rows, D_MODEL), F32),
        scratch_shapes=[pltpu.VMEM((tr, D_MODEL), BF16), pltpu.VMEM((tr, D_MODEL), F32)],
        compiler_params=_params(("parallel", "parallel", "arbitrary")),
        name="mlp",
    )(xa, mod, nw, w1, w2, fw)


def _rope_tables(t_lat, l_ctx, dim):
    f32 = np.float32
    pos = np.arange(t_lat)
    row, col = pos // GRID_W, pos % GRID_W
    d_axis = dim // 2
    inv = (f32(ROPE_BASE) ** (-np.arange(0, d_axis, 2, dtype=f32) / f32(d_axis))).astype(f32)
    ang_r = row.astype(f32)[:, None] * inv
    ang_c = col.astype(f32)[:, None] * inv
    cos = np.concatenate([np.cos(ang_r)] * 2 + [np.cos(ang_c)] * 2, axis=1)
    sin = np.concatenate([-np.sin(ang_r), np.sin(ang_r), -np.sin(ang_c), np.sin(ang_c)], axis=1)
    if dim < LANE:
        cos = np.concatenate([cos, np.ones((t_lat, LANE - dim), f32)], axis=1)
        sin = np.concatenate([sin, np.zeros((t_lat, LANE - dim), f32)], axis=1)
    cos = np.concatenate([cos, np.ones((l_ctx, LANE), f32)], axis=0)
    sin = np.concatenate([sin, np.zeros((l_ctx, LANE), f32)], axis=0)
    return jnp.asarray(cos, F32), jnp.asarray(sin, F32)


def _split_w_in(w):
    sizes = (512, 512, 1024, 1024, GLA_RANK, GLA_RANK, 1024, 256, 256, MLA_Q_RANK, MLA_KV_RANK, MLA_ROPE,
             3 * D_MODEL)
    offs = np.cumsum((0,) + sizes)
    return [w[..., offs[i]:offs[i + 1]] for i in range(len(sizes))]


def _layer_weights(w_in, gla_a_w, gla_a_b, mla_q_norm_w, mla_w_uq, mla_kv_norm_w, mla_w_ukv, swa_sink):
    depth = w_in.shape[0]
    gq, gk, gv, gr, gzf, gzb, sq, sk, sv, mcq, mckv, mkr, gates = _split_w_in(w_in)
    zpad = lambda n: jnp.zeros((depth, D_MODEL, n), F32)
    w_plain = jnp.concatenate(
        [gates, gr, gv, gq, gk, mcq, zpad(MCQ_PAD - MLA_Q_RANK), sv, mckv, gzf, gzb, zpad(LANE - 2 * GLA_RANK)],
        axis=-1).astype(BF16)
    w_rope = jnp.concatenate([sq, sk, mkr, zpad(LANE - MLA_ROPE)], axis=-1).astype(BF16)
    aw = jnp.zeros((depth, 2, LANE, GLA_HEADS * GLA_HEAD_K), F32)
    aw = aw.at[:, 0, :GLA_RANK].set(gla_a_w[:, 0]).at[:, 1, GLA_RANK:2 * GLA_RANK].set(gla_a_w[:, 1])
    aw = aw.astype(BF16)
    ab = gla_a_b.reshape(depth, 2, 1, GLA_HEADS * GLA_HEAD_K)
    wq = mla_w_uq.reshape(depth, MLA_Q_RANK, MLA_HEADS, MLA_NOPE + MLA_ROPE)
    wq = jnp.pad(wq, ((0, 0), (0, MCQ_PAD - MLA_Q_RANK), (0, 0), (0, 2 * LANE - MLA_NOPE - MLA_ROPE)))
    wq = wq.reshape(depth, MCQ_PAD, MLA_HEADS * 2 * LANE).astype(BF16)
    qnw = jnp.pad(mla_q_norm_w, ((0, 0), (0, MCQ_PAD - MLA_Q_RANK))).reshape(depth, 1, MCQ_PAD)
    wkv = mla_w_ukv.reshape(depth, MLA_KV_RANK, MLA_HEADS, MLA_NOPE + MLA_V)
    wkv = jnp.concatenate([wkv[..., :MLA_NOPE].reshape(depth, MLA_KV_RANK, -1),
                           wkv[..., MLA_NOPE:].reshape(depth, MLA_KV_RANK, -1)], axis=-1).astype(BF16)
    kvnw = mla_kv_norm_w.reshape(depth, 1, MLA_KV_RANK)
    sink2 = jnp.broadcast_to((swa_sink * LOG2E).reshape(depth, SWA_KV_HEADS, SWA_GROUP, 1),
                             (depth, SWA_KV_HEADS, SWA_GROUP, LANE))
    return w_plain, w_rope, aw, ab, wq, qnw, wkv, kvnw, sink2


def kernel(x, c, ctx, c_ctx, ada_w, ada_b, norm_mix_w, w_in, gla_a_w, gla_a_b, gla_norm_w, swa_sink,
           mla_q_norm_w, mla_w_uq, mla_kv_norm_w, mla_w_ukv, w_branch, w_out, norm_mlp_w, mlp_w1, mlp_w2,
           final_norm_w):
    n_batch, t_lat, d_model = x.shape
    l_ctx = ctx.shape[1]
    depth = ada_w.shape[0]
    s_len = t_lat + l_ctx
    assert d_model == D_MODEL and n_batch + 1 <= MOD_ROWS
    assert t_lat % GLA_BLOCK == 0 and l_ctx % GLA_BLOCK == 0 and t_lat % l_ctx == 0 and t_lat % GRID_W == 0

    xa = jnp.concatenate([x, ctx], axis=1)
    cin = jnp.zeros((MOD_ROWS, D_MODEL), F32).at[:n_batch].set(c).at[n_batch].set(c_ctx)
    mod_all = _ada_call(cin, ada_w, ada_b)

    cos_s, sin_s = _rope_tables(t_lat, l_ctx, SWA_HEAD_DIM)
    cos_m, sin_m = _rope_tables(t_lat, l_ctx, MLA_ROPE)
    tabs = (cos_s, sin_s, cos_m, sin_m)

    bq = min(1024, t_lat)
    bk = _kv_block(s_len)

    (w_plain, w_rope, aw, ab, wq, qnw, wkv, kvnw, sink2) = _layer_weights(
        w_in, gla_a_w, gla_a_b, mla_q_norm_w, mla_w_uq, mla_kv_norm_w, mla_w_ukv, swa_sink)
    row_vec = lambda v: v.reshape(depth, 1, -1)
    nw_mix, nw_mlp, gw = row_vec(norm_mix_w), row_vec(norm_mlp_w), row_vec(gla_norm_w)
    wb, wo, w1, w2 = (w.astype(BF16) for w in (w_branch, w_out, mlp_w1, mlp_w2))
    mod = mod_all

    for layer in range(depth):
        pa = _proj_call(xa, mod, nw_mix, w_plain, None, layer=layer, t_lat=t_lat, bn=PA_BN, rope=False,
                        name="proj_plain")
        pr = _proj_call(xa, mod, nw_mix, w_rope, tabs, layer=layer, t_lat=t_lat, bn=PR_WIDTH, rope=True,
                        name="proj_rope")

        o_f, o_b = _gla_call(pa, aw, ab, layer=layer, t_lat=t_lat)
        o_swa = _swa_call(pr, pa, sink2, layer=layer, t_lat=t_lat)

        q, k, v = _mla_proj_call(pa, pr, qnw, kvnw, wq, wkv, cos_m, sin_m, layer=layer)
        o_mla = _flash_call(q, k, v, None, bq=bq, bk=s_len, kv_sub=bk, q_blk0=0, n_q=t_lat // bq, kv_blk0=0,
                            n_kv=1, name="mla_flash")
        o_mla = _flash_call(q, k, v, o_mla, bq=l_ctx, bk=l_ctx, kv_sub=l_ctx, q_blk0=t_lat // l_ctx, n_q=1,
                            kv_blk0=t_lat // l_ctx, n_kv=1, name="mla_flash_ctx")

        xa = _merge_call(xa, mod, pa, o_f, o_b, o_swa, o_mla, gw, wb, wo, layer=layer, t_lat=t_lat)
        xa = _mlp_call(xa, mod, nw_mlp, w1, w2, final_norm_w.reshape(1, -1), layer=layer, t_lat=t_lat,
                       final=layer == depth - 1)
    return xa
```

```python
import functools

import jax
import jax.numpy as jnp
import numpy as np
from jax import lax
from jax.experimental import pallas as pl
from jax.experimental.pallas import tpu as pltpu

F32 = jnp.float32
BF16 = jnp.bfloat16

D_MODEL = 1024
EPS = 1e-6
ROPE_BASE = 10000.0
NEG_INF = -1e30
LOG2E = 1.4426950408889634
GRID_W = 64

GLA_HEADS = 4
GLA_HEAD_K = 128
GLA_HEAD_V = 256
GLA_RANK = 16
GLA_GATE_NORM = 16.0
GLA_CHUNK = 64
GLA_BLOCK = 256

SWA_HEADS = 8
SWA_KV_HEADS = 2
SWA_GROUP = 4
SWA_HEAD_DIM = 128
SWA_WINDOW = 128
SWA_BLOCK = 128
SWA_SCALE = SWA_HEAD_DIM ** -0.5
SWA_QTILE = 512

MLA_HEADS = 8
MLA_Q_RANK = 384
MLA_KV_RANK = 256
MLA_NOPE = 128
MLA_ROPE = 64
MLA_V = 128
MLA_SCALE = (MLA_NOPE + MLA_ROPE) ** -0.5

FF_DIM = 4 * D_MODEL
FF_CHUNK = 4096
FLASH_SUB = 256

LANE = 128
MOD_ROWS = 8
VMEM_LIMIT = 56 * 1024 * 1024

PA_GATES, PA_GR, PA_GV, PA_GQ, PA_GK = 0, 3072, 4096, 5120, 5632
PA_MCQ, PA_SV, PA_MCKV, PA_Z = 6144, 6656, 6912, 7168
PA_WIDTH = 7296
PA_BN = 2432
MCQ_PAD = 512
PR_SQ, PR_SK, PR_KPE = 0, 1024, 1280
PR_WIDTH = 1408
PR_SLABS = PR_WIDTH // LANE


def _params(sem):
    return pltpu.CompilerParams(dimension_semantics=sem, vmem_limit_bytes=VMEM_LIMIT)


def _layer_spec(layer, block, index_map, **kwargs):
    return pl.BlockSpec((None,) + tuple(block), lambda *g: (layer,) + tuple(index_map(*g)), **kwargs)


def _row_tile(s):
    for tr in (640, 512, 256, 128):
        if s % tr == 0:
            return tr
    raise ValueError(f"unsupported stream length {s}")


def _rms(x, w):
    return x * lax.rsqrt(jnp.mean(x * x, axis=-1, keepdims=True) + EPS) * w


def _mod_rows(mod_ref, b, n_batch, row0, rows, t_lat, idx):
    lo = idx * D_MODEL
    lat = mod_ref[pl.ds(b, 1), lo:lo + D_MODEL]
    ctx = mod_ref[n_batch:n_batch + 1, lo:lo + D_MODEL]
    rid = row0 + lax.broadcasted_iota(jnp.int32, (rows, 1), 0)
    return jnp.where(rid >= t_lat, ctx, lat)


def _swap_halves(x, half):
    lane = lax.broadcasted_iota(jnp.int32, x.shape, 1)
    return jnp.where((lane & half) == 0, pltpu.roll(x, LANE - half, 1), pltpu.roll(x, half, 1))


def _log_sigmoid(x):
    return jnp.minimum(x, 0.0) - jnp.log1p(jnp.exp(-jnp.abs(x)))


def _ada_kernel(c_ref, w_ref, b_ref, o_ref):
    c = c_ref[...]
    a = (c * jax.nn.sigmoid(c)).astype(BF16)
    o_ref[0] = jnp.dot(a, w_ref[0].astype(BF16), preferred_element_type=F32) + b_ref[0]


def _ada_call(cin, ada_w, ada_b):
    depth = ada_w.shape[0]
    nblk = ada_w.shape[2] // D_MODEL
    return pl.pallas_call(
        _ada_kernel,
        grid=(depth, nblk),
        in_specs=[
            pl.BlockSpec((MOD_ROWS, D_MODEL), lambda l, j: (0, 0)),
            pl.BlockSpec((1, D_MODEL, D_MODEL), lambda l, j: (l, 0, j)),
            pl.BlockSpec((1, 1, D_MODEL), lambda l, j: (l, 0, j)),
        ],
        out_specs=pl.BlockSpec((1, MOD_ROWS, D_MODEL), lambda l, j: (l, 0, j)),
        out_shape=jax.ShapeDtypeStruct((depth, MOD_ROWS, ada_w.shape[2]), F32),
        compiler_params=_params(("parallel", "parallel")),
        name="ada_mod",
    )(cin, ada_w, ada_b.reshape(depth, 1, -1))


def _proj_kernel(x_ref, mod_ref, nw_ref, w_ref, *rest, n_batch, t_lat, tr, rope, chunk):
    o_ref = rest[-1]
    b = pl.program_id(1)
    row0 = pl.program_id(2) * tr
    x = x_ref[0]
    shift = _mod_rows(mod_ref, b, n_batch, row0, tr, t_lat, 0)
    scale = _mod_rows(mod_ref, b, n_batch, row0, tr, t_lat, 1)
    h = (_rms(x, nw_ref[...]) * (1.0 + scale) + shift).astype(BF16)
    if not rope:
        for j in range(w_ref.shape[1] // chunk):
            cols = slice(j * chunk, (j + 1) * chunk)
            o_ref[0, :, cols] = jnp.dot(h, w_ref[:, cols], preferred_element_type=F32).astype(BF16)
        return
    acc = jnp.dot(h, w_ref[...], preferred_element_type=F32)
    cs_ref, sn_ref, cm_ref, sm_ref = rest[:4]
    cos_s, sin_s = cs_ref[...], sn_ref[...]
    for s in range(PR_SLABS):
        a = acc[:, s * LANE:(s + 1) * LANE]
        if s * LANE < PR_KPE:
            r = a * cos_s + _swap_halves(a, SWA_HEAD_DIM // 4) * sin_s
            if s * LANE < PR_SK:
                r = r * (SWA_SCALE * LOG2E)
        else:
            r = a * cm_ref[...] + _swap_halves(a, MLA_ROPE // 4) * sm_ref[...]
        o_ref[0, :, s * LANE:(s + 1) * LANE] = r.astype(BF16)


def _proj_call(xa, mod, nw, w, tabs, *, layer, t_lat, bn, rope, name):
    n_batch, s_len, _ = xa.shape
    tr = _row_tile(s_len)
    width = w.shape[-1]
    grid = (1, n_batch, s_len // tr)
    in_specs = [
        pl.BlockSpec((1, tr, D_MODEL), lambda j, b, i: (b, i, 0)),
        _layer_spec(layer, mod.shape[1:], lambda j, b, i: (0, 0)),
        _layer_spec(layer, (1, D_MODEL), lambda j, b, i: (0, 0)),
        _layer_spec(layer, (D_MODEL, width), lambda j, b, i: (0, 0), pipeline_mode=pl.Buffered(1)),
    ]
    args = [xa, mod, nw, w]
    if rope:
        in_specs += [pl.BlockSpec((tr, LANE), lambda j, b, i: (i, 0))] * 4
        args += list(tabs)
    return pl.pallas_call(
        functools.partial(_proj_kernel, n_batch=n_batch, t_lat=t_lat, tr=tr, rope=rope, chunk=bn),
        grid=grid,
        in_specs=in_specs,
        out_specs=pl.BlockSpec((1, tr, width), lambda j, b, i: (b, i, 0)),
        out_shape=jax.ShapeDtypeStruct((n_batch, s_len, width), BF16),
        compiler_params=_params(("parallel", "parallel", "parallel")),
        name=name,
    )(*args)


def _gla_mask(reverse):
    gb, ch = GLA_BLOCK, GLA_CHUNK
    r = lax.broadcasted_iota(jnp.int32, (gb, gb), 0)
    c = lax.broadcasted_iota(jnp.int32, (gb, gb), 1)
    return ((r // ch) == (c // ch)) & ((c >= r) if reverse else (c <= r))


def _gla_log_decay(z_ref, aw, ab):
    la = jnp.dot(z_ref[0], aw, preferred_element_type=F32) + ab
    la = _log_sigmoid(la) * (1.0 / GLA_GATE_NORM)
    hi = la.astype(BF16)
    lo = (la - hi.astype(F32)).astype(BF16)
    return jnp.concatenate([hi, lo], axis=1)


def _gla_cumulate(hi_lo, reverse):
    tmat = jnp.where(_gla_mask(reverse), 1.0, 0.0).astype(BF16)
    hw = GLA_HEADS * GLA_HEAD_K
    res = jnp.dot(tmat, hi_lo, preferred_element_type=F32)
    return res[:, :hw] + res[:, hw:]


def _gla_decay(z_ref, aw, ab, reverse):
    return _gla_cumulate(_gla_log_decay(z_ref, aw, ab), reverse)


def _gla_kernel(qf_ref, kf_ref, vf_ref, zf_ref, zfn_ref, qb_ref, kb_ref, vb_ref, zb_ref, zbn_ref, aw_ref, ab_ref,
                of_ref, ob_ref, stf_ref, stb_ref, bc_ref):
    n = pl.program_id(1)

    @pl.when(n == 0)
    def _():
        stf_ref[...] = jnp.zeros_like(stf_ref)
        stb_ref[...] = jnp.zeros_like(stb_ref)
        bc_ref[0] = _gla_decay(zf_ref, aw_ref[0], ab_ref[0], False)
        bc_ref[1] = _gla_decay(zb_ref, aw_ref[1], ab_ref[1], True)

    gb, ch, dk, dv = GLA_BLOCK, GLA_CHUNK, GLA_HEAD_K, GLA_HEAD_V
    n_ch = gb // ch
    nt = (((1,), (1,)), ((), ()))
    dirs = ((qf_ref, kf_ref, vf_ref, zfn_ref, of_ref, stf_ref, False),
            (qb_ref, kb_ref, vb_ref, zbn_ref, ob_ref, stb_ref, True))
    fronts = [(bc_ref[d], _gla_mask(rev)) for d, (_, _, _, _, _, _, rev) in enumerate(dirs)]
    upcoming = [_gla_log_decay(z_next_ref, aw_ref[d], ab_ref[d]) for d, (_, _, _, z_next_ref, _, _, _) in enumerate(dirs)]

    chains = []
    for (q_ref, k_ref, v_ref, _, o_ref, st_ref, rev), (bcum_all, mask) in zip(dirs, fronts):
        order = range(n_ch - 1, -1, -1) if rev else range(n_ch)
        for h in range(GLA_HEADS):
            bcum = bcum_all[:, h * dk:(h + 1) * dk]
            q = q_ref[0, :, h * dk:(h + 1) * dk].astype(F32)
            k = k_ref[0, :, h * dk:(h + 1) * dk].astype(F32)
            q_dec = (q * (dk ** -0.5) * jnp.exp(bcum)).astype(BF16)
            k_inv = (k * jnp.exp(-bcum)).astype(BF16)
            v = v_ref[0, :, h * dv:(h + 1) * dv]
            a = lax.dot_general(q_dec, k_inv, nt, preferred_element_type=F32)
            tots, incs = [], []
            for cidx in order:
                lo_r, hi_r = cidx * ch, (cidx + 1) * ch
                last = lo_r if rev else hi_r - 1
                tot = bcum[last:last + 1, :]
                k_end = k[lo_r:hi_r] * jnp.exp(tot - bcum[lo_r:hi_r])
                incs.append(jnp.dot(k_end.T.astype(BF16), v[lo_r:hi_r], preferred_element_type=F32))
                tots.append(tot)
            pad = jnp.zeros((8 - n_ch, dk), F32)
            g_cols = jnp.exp(jnp.concatenate(tots + [pad], axis=0)).T
            chains.append((o_ref, st_ref, h, order, mask, q_dec, v, a, incs, g_cols))

    upcoming = [_gla_cumulate(hi_lo, rev) for hi_lo, (_, _, _, _, _, _, rev) in zip(upcoming, dirs)]

    stage2 = []
    for o_ref, st_ref, h, order, mask, q_dec, v, a, incs, g_cols in chains:
        a = jnp.where(mask, a, 0.0).astype(BF16)
        st = st_ref[h]
        entering = []
        for i, _ in enumerate(order):
            entering.append(st.astype(BF16))
            st = st * g_cols[:, i:i + 1] + incs[i]
        st_ref[h] = st
        stage2.append((o_ref, h, order, q_dec, v, a, entering))

    for o_ref, h, order, q_dec, v, a, entering in stage2:
        o_intra = jnp.dot(a, v, preferred_element_type=F32)
        for i, cidx in enumerate(order):
            lo_r, hi_r = cidx * ch, (cidx + 1) * ch
            o = o_intra[lo_r:hi_r] + jnp.dot(q_dec[lo_r:hi_r], entering[i], preferred_element_type=F32)
            o_ref[0, lo_r:hi_r, h * dv:(h + 1) * dv] = o.astype(BF16)

    for d, nxt in enumerate(upcoming):
        bc_ref[d] = nxt


def _gla_call(pa, aw, ab, *, layer, t_lat):
    n_batch, s_len, _ = pa.shape
    gb = GLA_BLOCK
    n_lat, n_ctx = t_lat // gb, (s_len - t_lat) // gb
    nblk = n_lat + n_ctx
    qk_w, v_w = GLA_HEADS * GLA_HEAD_K, GLA_HEADS * GLA_HEAD_V

    fwd = lambda n: jnp.where(n < n_ctx, n_lat + n, n - n_ctx)
    bwd = lambda n: jnp.where(n < n_ctx, n_lat + n_ctx - 1 - n, n_lat - 1 - (n - n_ctx))

    def specs(blk):
        return [
            pl.BlockSpec((1, gb, qk_w), lambda b, n: (b, blk(n), PA_GQ // qk_w)),
            pl.BlockSpec((1, gb, qk_w), lambda b, n: (b, blk(n), PA_GK // qk_w)),
            pl.BlockSpec((1, gb, v_w), lambda b, n: (b, blk(n), PA_GV // v_w)),
            pl.BlockSpec((1, gb, LANE), lambda b, n: (b, blk(n), PA_Z // LANE)),
            pl.BlockSpec((1, gb, LANE), lambda b, n: (b, blk(jnp.minimum(n + 1, nblk - 1)), PA_Z // LANE)),
        ]

    out = jax.ShapeDtypeStruct((n_batch, s_len, v_w), BF16)
    state = pltpu.VMEM((GLA_HEADS, GLA_HEAD_K, GLA_HEAD_V), F32)
    decay = pltpu.VMEM((2, gb, qk_w), F32)
    return pl.pallas_call(
        _gla_kernel,
        grid=(n_batch, nblk),
        in_specs=specs(fwd) + specs(bwd) + [
            _layer_spec(layer, aw.shape[1:], lambda b, n: (0, 0, 0)),
            _layer_spec(layer, ab.shape[1:], lambda b, n: (0, 0, 0)),
        ],
        out_specs=[pl.BlockSpec((1, gb, v_w), lambda b, n: (b, fwd(n), 0)),
                   pl.BlockSpec((1, gb, v_w), lambda b, n: (b, bwd(n), 0))],
        out_shape=[out, out],
        scratch_shapes=[state, state, decay],
        compiler_params=_params(("parallel", "arbitrary")),
        name="gla",
    )(pa, pa, pa, pa, pa, pa, pa, pa, pa, pa, aw, ab)


def _with_one_hot(v):
    one_hot = (lax.broadcasted_iota(jnp.int32, v.shape, 1) == 0).astype(v.dtype)
    return jnp.concatenate([v, one_hot], axis=1)


def _swa_kernel(*refs, local, aliased):
    if aliased:
        refs = refs[1:]
    if local:
        q_ref, kp_ref, kc_ref, kn_ref, vp_ref, vc_ref, vn_ref, kx_ref, vx_ref, sink_ref, band_ref, o_ref = refs
    else:
        q_ref, kx_ref, vx_ref, sink_ref, o_ref = refs
    nt = (((1,), (1,)), ((), ()))
    if local:
        n = pl.program_id(2)
        blk = SWA_BLOCK
        keys = jnp.concatenate([kp_ref[0], kc_ref[0], kn_ref[0], kx_ref[0]], axis=0)
        vals = _with_one_hot(jnp.concatenate([vp_ref[0], vc_ref[0], vn_ref[0], vx_ref[0]], axis=0))
        n_loc = kp_ref.shape[1] + kc_ref.shape[1] + kn_ref.shape[1]
        col = lax.broadcasted_iota(jnp.int32, (1, keys.shape[0]), 1)
        edge = jnp.where((col < blk) & (n == 0), NEG_INF, 0.0)
        edge = jnp.where((col >= n_loc - blk) & (col < n_loc) & (n == pl.num_programs(2) - 1), NEG_INF, edge)
        bias = band_ref[...] + edge
    else:
        keys = kx_ref[0]
        vals = _with_one_hot(vx_ref[0])
        bias = None
    heads = range(SWA_GROUP)
    qs = [q_ref[0, :, g * LANE:(g + 1) * LANE] for g in heads]
    scores = [lax.dot_general(q, keys, nt, preferred_element_type=F32) for q in qs]
    if local:
        scores = [s + bias for s in scores]
    stage2 = []
    for g in heads:
        sink = sink_ref[0, g:g + 1, 0:1]
        m = jnp.maximum(jnp.max(scores[g], axis=1, keepdims=True), sink)
        stage2.append((jnp.exp2(scores[g] - m).astype(BF16), jnp.exp2(sink - m)))
    for g, (p, p_sink) in enumerate(stage2):
        o = jnp.dot(p, vals, preferred_element_type=F32)
        den = o[:, SWA_HEAD_DIM:SWA_HEAD_DIM + 1] + p_sink
        o_ref[0, :, g * LANE:(g + 1) * LANE] = (o[:, :SWA_HEAD_DIM] / den).astype(BF16)


def _swa_band_table(bq, l_ctx):
    r = np.arange(bq)[:, None]
    c = np.arange(bq + 2 * SWA_BLOCK)[None, :] - SWA_BLOCK
    band = np.where(np.abs(r - c) <= SWA_WINDOW, 0.0, NEG_INF)
    return jnp.asarray(np.concatenate([band, np.zeros((bq, l_ctx))], axis=1), F32)


def _swa_call(pr, pa, sink2, *, layer, t_lat):
    n_batch, s_len, _ = pr.shape
    blk = SWA_BLOCK
    l_ctx = s_len - t_lat
    bq = SWA_QTILE if t_lat % SWA_QTILE == 0 else blk
    per = bq // blk
    n_blk = t_lat // blk
    ctx_blk = t_lat // l_ctx
    gw = SWA_GROUP * LANE
    kcol = lambda kh: PR_SK // LANE + kh
    vcol = lambda kh: PA_SV // LANE + kh
    prev = lambda n: jnp.maximum(n * per - 1, 0)
    nxt = lambda n: jnp.minimum((n + 1) * per, n_blk - 1)
    edge = lambda col, pos: pl.BlockSpec((1, blk, LANE), lambda b, kh, n: (b, pos(n), col(kh)))
    body = lambda col: pl.BlockSpec((1, bq, LANE), lambda b, kh, n: (b, n, col(kh)))
    ctx = lambda col: pl.BlockSpec((1, l_ctx, LANE), lambda b, kh, n: (b, ctx_blk, col(kh)))
    sink_spec = _layer_spec(layer, (1, SWA_GROUP, LANE), lambda b, kh, n: (kh, 0, 0))
    out_shape = jax.ShapeDtypeStruct((n_batch, s_len, SWA_HEADS * SWA_HEAD_DIM), BF16)
    band = _swa_band_table(bq, l_ctx)
    o_lat = pl.pallas_call(
        functools.partial(_swa_kernel, local=True, aliased=False),
        grid=(n_batch, SWA_KV_HEADS, t_lat // bq),
        in_specs=[
            pl.BlockSpec((1, bq, gw), lambda b, kh, n: (b, n, PR_SQ // gw + kh)),
            edge(kcol, prev), body(kcol), edge(kcol, nxt),
            edge(vcol, prev), body(vcol), edge(vcol, nxt),
            ctx(kcol), ctx(vcol), sink_spec,
            pl.BlockSpec(band.shape, lambda b, kh, n: (0, 0)),
        ],
        out_specs=pl.BlockSpec((1, bq, gw), lambda b, kh, n: (b, n, kh)),
        out_shape=out_shape,
        compiler_params=_params(("parallel", "parallel", "parallel")),
        name="swa",
    )(pr, pr, pr, pr, pa, pa, pa, pr, pa, sink2, band)
    return pl.pallas_call(
        functools.partial(_swa_kernel, local=False, aliased=True),
        grid=(n_batch, SWA_KV_HEADS, 1),
        in_specs=[
            pl.BlockSpec(memory_space=pl.ANY),
            pl.BlockSpec((1, l_ctx, gw), lambda b, kh, n: (b, ctx_blk, PR_SQ // gw + kh)),
            ctx(kcol), ctx(vcol), sink_spec,
        ],
        out_specs=pl.BlockSpec((1, l_ctx, gw), lambda b, kh, n: (b, ctx_blk, kh)),
        out_shape=out_shape,
        input_output_aliases={0: 0},
        compiler_params=_params(("parallel", "parallel", "parallel")),
        name="swa_ctx",
    )(o_lat, pr, pr, pa, sink2)


def _mla_proj_kernel(cq_ref, ckv_ref, kpe_ref, qnw_ref, kvnw_ref, wq_ref, wkv_ref, cm_ref, sm_ref,
                     q_ref, k_ref, v_ref):
    cq = cq_ref[0].astype(F32)
    ms = jnp.sum(cq * cq, axis=-1, keepdims=True) * (1.0 / MLA_Q_RANK)
    cqn = (cq * lax.rsqrt(ms + EPS) * qnw_ref[...]).astype(BF16)
    q = jnp.dot(cqn, wq_ref[...], preferred_element_type=F32)
    cos_m, sin_m = cm_ref[...], sm_ref[...]
    qs = MLA_SCALE * LOG2E
    hw = 2 * LANE
    for h in range(MLA_HEADS):
        q_ref[0, :, h * hw:h * hw + LANE] = (q[:, h * hw:h * hw + LANE] * qs).astype(BF16)
        pe = q[:, h * hw + LANE:(h + 1) * hw]
        pe = (pe * cos_m + _swap_halves(pe, MLA_ROPE // 4) * sin_m) * qs
        q_ref[0, :, h * hw + LANE:(h + 1) * hw] = pe.astype(BF16)
    ckv = _rms(ckv_ref[0].astype(F32), kvnw_ref[...]).astype(BF16)
    kv = jnp.dot(ckv, wkv_ref[...], preferred_element_type=F32)
    half = MLA_HEADS * MLA_NOPE
    kpe = kpe_ref[0]
    one_hot = (lax.broadcasted_iota(jnp.int32, kpe.shape, 1) == 0).astype(BF16)
    for h in range(MLA_HEADS):
        k_ref[0, :, h * hw:h * hw + LANE] = kv[:, h * LANE:(h + 1) * LANE].astype(BF16)
        k_ref[0, :, h * hw + LANE:(h + 1) * hw] = kpe
        v_ref[0, :, h * hw:h * hw + LANE] = kv[:, half + h * LANE:half + (h + 1) * LANE].astype(BF16)
        v_ref[0, :, h * hw + LANE:(h + 1) * hw] = one_hot


def _mla_proj_call(pa, pr, qnw, kvnw, wq, wkv, cos_m, sin_m, *, layer):
    n_batch, s_len, _ = pa.shape
    tr = _row_tile(s_len)
    qw = MLA_HEADS * 2 * LANE
    kw = MLA_HEADS * MLA_NOPE
    const = lambda shape: _layer_spec(layer, shape, lambda b, i: (0,) * len(shape))
    wide = pl.BlockSpec((1, tr, qw), lambda b, i: (b, i, 0))
    return pl.pallas_call(
        _mla_proj_kernel,
        grid=(n_batch, s_len // tr),
        in_specs=[
            pl.BlockSpec((1, tr, MCQ_PAD), lambda b, i: (b, i, PA_MCQ // MCQ_PAD)),
            pl.BlockSpec((1, tr, MLA_KV_RANK), lambda b, i: (b, i, PA_MCKV // MLA_KV_RANK)),
            pl.BlockSpec((1, tr, LANE), lambda b, i: (b, i, PR_KPE // LANE)),
            const((1, MCQ_PAD)), const((1, MLA_KV_RANK)),
            const((MCQ_PAD, qw)), const((MLA_KV_RANK, 2 * kw)),
            pl.BlockSpec((tr, LANE), lambda b, i: (i, 0)),
            pl.BlockSpec((tr, LANE), lambda b, i: (i, 0)),
        ],
        out_specs=[wide, wide, wide],
        out_shape=[jax.ShapeDtypeStruct((n_batch, s_len, qw), BF16)] * 3,
        compiler_params=_params(("parallel", "parallel")),
        name="mla_proj",
    )(pa, pa, pr, qnw, kvnw, wq, wkv, cos_m, sin_m)


def _flash_kernel(*refs, aliased, sub, kv_sub):
    if aliased:
        refs = refs[1:]
    q_ref, k_ref, v_ref, o_ref, m_ref, acc_ref = refs
    j = pl.program_id(3)

    @pl.when(j == 0)
    def _():
        m_ref[...] = jnp.full_like(m_ref, NEG_INF)
        acc_ref[...] = jnp.zeros_like(acc_ref)

    bq, bk = q_ref.shape[1], k_ref.shape[1]
    sub = min(sub, bq)
    kv_sub = min(kv_sub, bk)
    chains = [slice(c * sub, (c + 1) * sub) for c in range(bq // sub)]
    for t in range(bk // kv_sub):
        k = k_ref[0, t * kv_sub:(t + 1) * kv_sub, :]
        v = v_ref[0, t * kv_sub:(t + 1) * kv_sub, :]
        scores = [lax.dot_general(q_ref[0, rows, :], k, (((1,), (1,)), ((), ())), preferred_element_type=F32)
                  for rows in chains]
        probs, alphas = [], []
        for rows, s in zip(chains, scores):
            m_prev = m_ref[rows, :]
            m_new = jnp.maximum(m_prev, jnp.max(s, axis=1, keepdims=True))
            alphas.append(jnp.exp2(m_prev - m_new))
            probs.append(jnp.exp2(s - m_new).astype(BF16))
            m_ref[rows, :] = m_new
        for rows, p, alpha in zip(chains, probs, alphas):
            acc_ref[rows, :] = alpha * acc_ref[rows, :] + jnp.dot(p, v, preferred_element_type=F32)

    @pl.when(j == pl.num_programs(3) - 1)
    def _():
        acc = acc_ref[...]
        o_ref[0] = (acc[:, :MLA_V] / acc[:, MLA_V:MLA_V + 1]).astype(BF16)


def _flash_call(q, k, v, prev_out, *, bq, bk, kv_sub, q_blk0, n_q, kv_blk0, n_kv, name):
    n_batch, s_len, _ = q.shape
    aliased = prev_out is not None
    hw = 2 * LANE
    in_specs = [
        pl.BlockSpec((1, bq, hw), lambda b, h, i, j: (b, q_blk0 + i, h)),
        pl.BlockSpec((1, bk, hw), lambda b, h, i, j: (b, kv_blk0 + j, h)),
        pl.BlockSpec((1, bk, hw), lambda b, h, i, j: (b, kv_blk0 + j, h)),
    ]
    args = [q, k, v]
    if aliased:
        in_specs = [pl.BlockSpec(memory_space=pl.ANY)] + in_specs
        args = [prev_out] + args
    return pl.pallas_call(
        functools.partial(_flash_kernel, aliased=aliased, sub=FLASH_SUB, kv_sub=kv_sub),
        grid=(n_batch, MLA_HEADS, n_q, n_kv),
        in_specs=in_specs,
        out_specs=pl.BlockSpec((1, bq, LANE), lambda b, h, i, j: (b, q_blk0 + i, h)),
        out_shape=jax.ShapeDtypeStruct((n_batch, s_len, MLA_HEADS * MLA_V), BF16),
        scratch_shapes=[pltpu.VMEM((bq, 1), F32), pltpu.VMEM((bq, hw), F32)],
        input_output_aliases={0: 0} if aliased else {},
        compiler_params=_params(("parallel", "parallel", "parallel", "arbitrary")),
        name=name,
    )(*args)


def _kv_block(s_len):
    for bk in (3328, 1280, 640, 256, 128):
        if s_len % bk == 0:
            return bk
    raise ValueError(f"unsupported stream length {s_len}")


def _merge_kernel(x_ref, mod_ref, g_ref, gr_ref, of_ref, ob_ref, osw_ref, om_ref, gw_ref, wb_ref, wo_ref,
                  o_ref, *, n_batch, t_lat, tr):
    b = pl.program_id(0)
    row0 = pl.program_id(1) * tr
    gw = gw_ref[...]
    halves = [slice(0, tr // 2), slice(tr // 2, tr)]
    gla = []
    for rows in halves:
        og = of_ref[0, rows, :].astype(F32) + ob_ref[0, rows, :].astype(F32)
        heads = [_rms(og[:, h * GLA_HEAD_V:(h + 1) * GLA_HEAD_V], gw) for h in range(GLA_HEADS)]
        gr = gr_ref[0, rows, :].astype(F32)
        gla.append((jnp.concatenate(heads, axis=1) * (gr * jax.nn.sigmoid(gr))).astype(BF16))
    projected = [[jnp.dot(ob, wb_ref[idx], preferred_element_type=F32)
                  for idx, ob in enumerate((o_gla, osw_ref[0, rows, :], om_ref[0, rows, :]))]
                 for rows, o_gla in zip(halves, gla)]
    merged = []
    for rows, terms in zip(halves, projected):
        total = None
        for idx, term in enumerate(terms):
            gate = jax.nn.sigmoid(g_ref[0, rows, idx * D_MODEL:(idx + 1) * D_MODEL].astype(F32))
            total = gate * term if total is None else total + gate * term
        merged.append(total.astype(BF16))
    outs = [jnp.dot(m, wo_ref[...], preferred_element_type=F32) for m in merged]
    g_m = _mod_rows(mod_ref, b, n_batch, row0, tr, t_lat, 2)
    for rows, y in zip(halves, outs):
        o_ref[0, rows, :] = x_ref[0, rows, :] + g_m[rows] * y


def _merge_call(xa, mod, pa, o_f, o_b, o_swa, o_mla, gw, wb, wo, *, layer, t_lat):
    n_batch, s_len, _ = xa.shape
    tr = _row_tile(s_len)
    row = lambda width, col: pl.BlockSpec((1, tr, width), lambda b, i: (b, i, col))
    resident = dict(pipeline_mode=pl.Buffered(1))
    return pl.pallas_call(
        functools.partial(_merge_kernel, n_batch=n_batch, t_lat=t_lat, tr=tr),
        grid=(n_batch, s_len // tr),
        in_specs=[
            row(D_MODEL, 0),
            _layer_spec(layer, mod.shape[1:], lambda b, i: (0, 0)),
            row(3 * D_MODEL, PA_GATES // (3 * D_MODEL)),
            row(D_MODEL, PA_GR // D_MODEL),
            row(D_MODEL, 0), row(D_MODEL, 0), row(D_MODEL, 0), row(D_MODEL, 0),
            _layer_spec(layer, (1, GLA_HEAD_V), lambda b, i: (0, 0)),
            _layer_spec(layer, (3, D_MODEL, D_MODEL), lambda b, i: (0, 0, 0), **resident),
            _layer_spec(layer, (D_MODEL, D_MODEL), lambda b, i: (0, 0), **resident),
        ],
        out_specs=row(D_MODEL, 0),
        out_shape=jax.ShapeDtypeStruct(xa.shape, F32),
        compiler_params=_params(("parallel", "parallel")),
        name="merge",
    )(xa, mod, pa, pa, o_f, o_b, o_swa, o_mla, gw, wb, wo)


def _mlp_kernel(x_ref, mod_ref, nw_ref, w1_ref, w2_ref, fw_ref, o_ref, h_ref, acc_ref,
                *, n_batch, t_lat, tr, final):
    b = pl.program_id(0)
    row0 = pl.program_id(1) * tr
    c = pl.program_id(2)

    @pl.when(c == 0)
    def _():
        shift = _mod_rows(mod_ref, b, n_batch, row0, tr, t_lat, 3)
        scale = _mod_rows(mod_ref, b, n_batch, row0, tr, t_lat, 4)
        h_ref[...] = (_rms(x_ref[0], nw_ref[...]) * (1.0 + scale) + shift).astype(BF16)
        acc_ref[...] = jnp.zeros_like(acc_ref)

    halves = [slice(0, tr // 2), slice(tr // 2, tr)]
    ups = [jnp.maximum(jnp.dot(h_ref[rows, :], w1_ref[...], preferred_element_type=F32), 0.0) for rows in halves]
    for rows, u in zip(halves, ups):
        acc_ref[rows, :] += jnp.dot((u * u).astype(BF16), w2_ref[...], preferred_element_type=F32)

    @pl.when(c == pl.num_programs(2) - 1)
    def _():
        g_f = _mod_rows(mod_ref, b, n_batch, row0, tr, t_lat, 5)
        y = x_ref[0] + g_f * acc_ref[...]
        if final:
            y = _rms(y, fw_ref[...])
        o_ref[0] = y


def _mlp_call(xa, mod, nw, w1, w2, fw, *, layer, t_lat, final):
    n_batch, s_len, _ = xa.shape
    rows = t_lat if final else s_len
    tr = _row_tile(rows)
    w_mode = dict(pipeline_mode=pl.Buffered(1)) if FF_CHUNK == FF_DIM else {}
    return pl.pallas_call(
        functools.partial(_mlp_kernel, n_batch=n_batch, t_lat=t_lat, tr=tr, final=final),
        grid=(n_batch, rows // tr, FF_DIM // FF_CHUNK),
        in_specs=[
            pl.BlockSpec((1, tr, D_MODEL), lambda b, i, c: (b, i, 0)),
            _layer_spec(layer, mod.shape[1:], lambda b, i, c: (0, 0)),
            _layer_spec(layer, (1, D_MODEL), lambda b, i, c: (0, 0)),
            _layer_spec(layer, (D_MODEL, FF_CHUNK), lambda b, i, c: (0, c), **w_mode),
            _layer_spec(layer, (FF_CHUNK, D_MODEL), lambda b, i, c: (c, 0), **w_mode),
            pl.BlockSpec((1, D_MODEL), lambda b, i, c: (0, 0)),
        ],
        out_specs=pl.BlockSpec((1, tr, D_MODEL), lambda b, i, c: (b, i, 0)),
        out_shape=jax.ShapeDtypeStruct((n_batch, rows, D_MODEL), F32),
        scratch_shapes=[pltpu.VMEM((tr, D_MODEL), BF16), pltpu.VMEM((tr, D_MODEL), F32)],
        compiler_params=_params(("parallel", "parallel", "arbitrary")),
        name="mlp",
    )(xa, mod, nw, w1, w2, fw)


def _rope_tables(t_lat, l_ctx, dim):
    f32 = np.float32
    pos = np.arange(t_lat)
    row, col = pos // GRID_W, pos % GRID_W
    d_axis = dim // 2
    inv = (f32(ROPE_BASE) ** (-np.arange(0, d_axis, 2, dtype=f32) / f32(d_axis))).astype(f32)
    ang_r = row.astype(f32)[:, None] * inv
    ang_c = col.astype(f32)[:, None] * inv
    cos = np.concatenate([np.cos(ang_r)] * 2 + [np.cos(ang_c)] * 2, axis=1)
    sin = np.concatenate([-np.sin(ang_r), np.sin(ang_r), -np.sin(ang_c), np.sin(ang_c)], axis=1)
    if dim < LANE:
        cos = np.concatenate([cos, np.ones((t_lat, LANE - dim), f32)], axis=1)
        sin = np.concatenate([sin, np.zeros((t_lat, LANE - dim), f32)], axis=1)
    cos = np.concatenate([cos, np.ones((l_ctx, LANE), f32)], axis=0)
    sin = np.concatenate([sin, np.zeros((l_ctx, LANE), f32)], axis=0)
    return jnp.asarray(cos, F32), jnp.asarray(sin, F32)


def _split_w_in(w):
    sizes = (512, 512, 1024, 1024, GLA_RANK, GLA_RANK, 1024, 256, 256, MLA_Q_RANK, MLA_KV_RANK, MLA_ROPE,
             3 * D_MODEL)
    offs = np.cumsum((0,) + sizes)
    return [w[..., offs[i]:offs[i + 1]] for i in range(len(sizes))]


def _layer_weights(w_in, gla_a_w, gla_a_b, mla_q_norm_w, mla_w_uq, mla_kv_norm_w, mla_w_ukv, swa_sink):
    depth = w_in.shape[0]
    gq, gk, gv, gr, gzf, gzb, sq, sk, sv, mcq, mckv, mkr, gates = _split_w_in(w_in)
    zpad = lambda n: jnp.zeros((depth, D_MODEL, n), F32)
    w_plain = jnp.concatenate(
        [gates, gr, gv, gq, gk, mcq, zpad(MCQ_PAD - MLA_Q_RANK), sv, mckv, gzf, gzb, zpad(LANE - 2 * GLA_RANK)],
        axis=-1).astype(BF16)
    w_rope = jnp.concatenate([sq, sk, mkr, zpad(LANE - MLA_ROPE)], axis=-1).astype(BF16)
    aw = jnp.zeros((depth, 2, LANE, GLA_HEADS * GLA_HEAD_K), F32)
    aw = aw.at[:, 0, :GLA_RANK].set(gla_a_w[:, 0]).at[:, 1, GLA_RANK:2 * GLA_RANK].set(gla_a_w[:, 1])
    aw = aw.astype(BF16)
    ab = gla_a_b.reshape(depth, 2, 1, GLA_HEADS * GLA_HEAD_K)
    wq = mla_w_uq.reshape(depth, MLA_Q_RANK, MLA_HEADS, MLA_NOPE + MLA_ROPE)
    wq = jnp.pad(wq, ((0, 0), (0, MCQ_PAD - MLA_Q_RANK), (0, 0), (0, 2 * LANE - MLA_NOPE - MLA_ROPE)))
    wq = wq.reshape(depth, MCQ_PAD, MLA_HEADS * 2 * LANE).astype(BF16)
    qnw = jnp.pad(mla_q_norm_w, ((0, 0), (0, MCQ_PAD - MLA_Q_RANK))).reshape(depth, 1, MCQ_PAD)
    wkv = mla_w_ukv.reshape(depth, MLA_KV_RANK, MLA_HEADS, MLA_NOPE + MLA_V)
    wkv = jnp.concatenate([wkv[..., :MLA_NOPE].reshape(depth, MLA_KV_RANK, -1),
                           wkv[..., MLA_NOPE:].reshape(depth, MLA_KV_RANK, -1)], axis=-1).astype(BF16)
    kvnw = mla_kv_norm_w.reshape(depth, 1, MLA_KV_RANK)
    sink2 = jnp.broadcast_to((swa_sink * LOG2E).reshape(depth, SWA_KV_HEADS, SWA_GROUP, 1),
                             (depth, SWA_KV_HEADS, SWA_GROUP, LANE))
    return w_plain, w_rope, aw, ab, wq, qnw, wkv, kvnw, sink2


def kernel(x, c, ctx, c_ctx, ada_w, ada_b, norm_mix_w, w_in, gla_a_w, gla_a_b, gla_norm_w, swa_sink,
           mla_q_norm_w, mla_w_uq, mla_kv_norm_w, mla_w_ukv, w_branch, w_out, norm_mlp_w, mlp_w1, mlp_w2,
           final_norm_w):
    n_batch, t_lat, d_model = x.shape
    l_ctx = ctx.shape[1]
    depth = ada_w.shape[0]
    s_len = t_lat + l_ctx
    assert d_model == D_MODEL and n_batch + 1 <= MOD_ROWS
    assert t_lat % GLA_BLOCK == 0 and l_ctx % GLA_BLOCK == 0 and t_lat % l_ctx == 0 and t_lat % GRID_W == 0

    xa = jnp.concatenate([x, ctx], axis=1)
    cin = jnp.zeros((MOD_ROWS, D_MODEL), F32).at[:n_batch].set(c).at[n_batch].set(c_ctx)
    mod_all = _ada_call(cin, ada_w, ada_b)

    cos_s, sin_s = _rope_tables(t_lat, l_ctx, SWA_HEAD_DIM)
    cos_m, sin_m = _rope_tables(t_lat, l_ctx, MLA_ROPE)
    tabs = (cos_s, sin_s, cos_m, sin_m)

    bq = min(1024, t_lat)
    bk = _kv_block(s_len)

    (w_plain, w_rope, aw, ab, wq, qnw, wkv, kvnw, sink2) = _layer_weights(
        w_in, gla_a_w, gla_a_b, mla_q_norm_w, mla_w_uq, mla_kv_norm_w, mla_w_ukv, swa_sink)
    row_vec = lambda v: v.reshape(depth, 1, -1)
    nw_mix, nw_mlp, gw = row_vec(norm_mix_w), row_vec(norm_mlp_w), row_vec(gla_norm_w)
    wb, wo, w1, w2 = (w.astype(BF16) for w in (w_branch, w_out, mlp_w1, mlp_w2))
    mod = mod_all

    for layer in range(depth):
        pa = _proj_call(xa, mod, nw_mix, w_plain, None, layer=layer, t_lat=t_lat, bn=PA_BN, rope=False,
                        name="proj_plain")
        pr = _proj_call(xa, mod, nw_mix, w_rope, tabs, layer=layer, t_lat=t_lat, bn=PR_WIDTH, rope=True,
                        name="proj_rope")

        o_f, o_b = _gla_call(pa, aw, ab, layer=layer, t_lat=t_lat)
        o_swa = _swa_call(pr, pa, sink2, layer=layer, t_lat=t_lat)

        q, k, v = _mla_proj_call(pa, pr, qnw, kvnw, wq, wkv, cos_m, sin_m, layer=layer)
        o_mla = _flash_call(q, k, v, None, bq=bq, bk=s_len, kv_sub=bk, q_blk0=0, n_q=t_lat // bq, kv_blk0=0,
                            n_kv=1, name="mla_flash")
        o_mla = _flash_call(q, k, v, o_mla, bq=l_ctx, bk=l_ctx, kv_sub=l_ctx, q_blk0=t_lat // l_ctx, n_q=1,
                            kv_blk0=t_lat // l_ctx, n_kv=1, name="mla_flash_ctx")

        xa = _merge_call(xa, mod, pa, o_f, o_b, o_swa, o_mla, gw, wb, wo, layer=layer, t_lat=t_lat)
        xa = _mlp_call(xa, mod, nw_mlp, w1, w2, final_norm_w.reshape(1, -1), layer=layer, t_lat=t_lat,
                       final=layer == depth - 1)
    return xa
```

```python
import functools

import jax
import jax.numpy as jnp
import numpy as np
from jax import lax
from jax.experimental import pallas as pl
from jax.experimental.pallas import tpu as pltpu

F32 = jnp.float32
BF16 = jnp.bfloat16

D_MODEL = 1024
EPS = 1e-6
ROPE_BASE = 10000.0
NEG_INF = -1e30
LOG2E = 1.4426950408889634
GRID_W = 64

GLA_HEADS = 4
GLA_HEAD_K = 128
GLA_HEAD_V = 256
GLA_RANK = 16
GLA_GATE_NORM = 16.0
GLA_CHUNK = 64
GLA_BLOCK = 256

SWA_HEADS = 8
SWA_KV_HEADS = 2
SWA_GROUP = 4
SWA_HEAD_DIM = 128
SWA_WINDOW = 128
SWA_BLOCK = 128
SWA_SCALE = SWA_HEAD_DIM ** -0.5
SWA_QTILE = 1024

MLA_HEADS = 8
MLA_Q_RANK = 384
MLA_KV_RANK = 256
MLA_NOPE = 128
MLA_ROPE = 64
MLA_V = 128
MLA_SCALE = (MLA_NOPE + MLA_ROPE) ** -0.5

FF_DIM = 4 * D_MODEL
FF_CHUNK = 4096
FLASH_SUB = 256

LANE = 128
MOD_ROWS = 8
VMEM_LIMIT = 56 * 1024 * 1024

PA_GATES, PA_GR, PA_GV, PA_GQ, PA_GK = 0, 3072, 4096, 5120, 5632
PA_MCQ, PA_SV, PA_MCKV, PA_Z = 6144, 6656, 6912, 7168
PA_WIDTH = 7296
PA_BN = 2432
MCQ_PAD = 512
PR_SQ, PR_SK, PR_KPE = 0, 1024, 1280
PR_WIDTH = 1408
PR_SLABS = PR_WIDTH // LANE


def _params(sem):
    return pltpu.CompilerParams(dimension_semantics=sem, vmem_limit_bytes=VMEM_LIMIT)


def _layer_spec(layer, block, index_map, **kwargs):
    return pl.BlockSpec((None,) + tuple(block), lambda *g: (layer,) + tuple(index_map(*g)), **kwargs)


def _row_tile(s):
    for tr in (640, 512, 256, 128):
        if s % tr == 0:
            return tr
    raise ValueError(f"unsupported stream length {s}")


def _rms(x, w):
    return x * lax.rsqrt(jnp.mean(x * x, axis=-1, keepdims=True) + EPS) * w


def _mod_rows(mod_ref, b, n_batch, row0, rows, t_lat, idx):
    lo = idx * D_MODEL
    lat = mod_ref[pl.ds(b, 1), lo:lo + D_MODEL]
    ctx = mod_ref[n_batch:n_batch + 1, lo:lo + D_MODEL]
    rid = row0 + lax.broadcasted_iota(jnp.int32, (rows, 1), 0)
    return jnp.where(rid >= t_lat, ctx, lat)


def _swap_halves(x, half):
    lane = lax.broadcasted_iota(jnp.int32, x.shape, 1)
    return jnp.where((lane & half) == 0, pltpu.roll(x, LANE - half, 1), pltpu.roll(x, half, 1))


def _log_sigmoid(x):
    return jnp.minimum(x, 0.0) - jnp.log1p(jnp.exp(-jnp.abs(x)))


def _ada_kernel(c_ref, w_ref, b_ref, o_ref):
    c = c_ref[...]
    a = (c * jax.nn.sigmoid(c)).astype(BF16)
    o_ref[0] = jnp.dot(a, w_ref[0].astype(BF16), preferred_element_type=F32) + b_ref[0]


def _ada_call(cin, ada_w, ada_b):
    depth = ada_w.shape[0]
    nblk = ada_w.shape[2] // D_MODEL
    return pl.pallas_call(
        _ada_kernel,
        grid=(depth, nblk),
        in_specs=[
            pl.BlockSpec((MOD_ROWS, D_MODEL), lambda l, j: (0, 0)),
            pl.BlockSpec((1, D_MODEL, D_MODEL), lambda l, j: (l, 0, j)),
            pl.BlockSpec((1, 1, D_MODEL), lambda l, j: (l, 0, j)),
        ],
        out_specs=pl.BlockSpec((1, MOD_ROWS, D_MODEL), lambda l, j: (l, 0, j)),
        out_shape=jax.ShapeDtypeStruct((depth, MOD_ROWS, ada_w.shape[2]), F32),
        compiler_params=_params(("parallel", "parallel")),
        name="ada_mod",
    )(cin, ada_w, ada_b.reshape(depth, 1, -1))


def _proj_kernel(x_ref, mod_ref, nw_ref, w_ref, *rest, n_batch, t_lat, tr, rope, chunk):
    o_ref = rest[-1]
    b = pl.program_id(1)
    row0 = pl.program_id(2) * tr
    x = x_ref[0]
    shift = _mod_rows(mod_ref, b, n_batch, row0, tr, t_lat, 0)
    scale = _mod_rows(mod_ref, b, n_batch, row0, tr, t_lat, 1)
    h = (_rms(x, nw_ref[...]) * (1.0 + scale) + shift).astype(BF16)
    if not rope:
        for j in range(w_ref.shape[1] // chunk):
            cols = slice(j * chunk, (j + 1) * chunk)
            o_ref[0, :, cols] = jnp.dot(h, w_ref[:, cols], preferred_element_type=F32).astype(BF16)
        return
    acc = jnp.dot(h, w_ref[...], preferred_element_type=F32)
    cs_ref, sn_ref, cm_ref, sm_ref = rest[:4]
    cos_s, sin_s = cs_ref[...], sn_ref[...]
    for s in range(PR_SLABS):
        a = acc[:, s * LANE:(s + 1) * LANE]
        if s * LANE < PR_KPE:
            r = a * cos_s + _swap_halves(a, SWA_HEAD_DIM // 4) * sin_s
            if s * LANE < PR_SK:
                r = r * (SWA_SCALE * LOG2E)
        else:
            r = a * cm_ref[...] + _swap_halves(a, MLA_ROPE // 4) * sm_ref[...]
        o_ref[0, :, s * LANE:(s + 1) * LANE] = r.astype(BF16)


def _proj_call(xa, mod, nw, w, tabs, *, layer, t_lat, bn, rope, name):
    n_batch, s_len, _ = xa.shape
    tr = _row_tile(s_len)
    width = w.shape[-1]
    grid = (1, n_batch, s_len // tr)
    in_specs = [
        pl.BlockSpec((1, tr, D_MODEL), lambda j, b, i: (b, i, 0)),
        _layer_spec(layer, mod.shape[1:], lambda j, b, i: (0, 0)),
        _layer_spec(layer, (1, D_MODEL), lambda j, b, i: (0, 0)),
        _layer_spec(layer, (D_MODEL, width), lambda j, b, i: (0, 0), pipeline_mode=pl.Buffered(1)),
    ]
    args = [xa, mod, nw, w]
    if rope:
        in_specs += [pl.BlockSpec((tr, LANE), lambda j, b, i: (i, 0))] * 4
        args += list(tabs)
    return pl.pallas_call(
        functools.partial(_proj_kernel, n_batch=n_batch, t_lat=t_lat, tr=tr, rope=rope, chunk=bn),
        grid=grid,
        in_specs=in_specs,
        out_specs=pl.BlockSpec((1, tr, width), lambda j, b, i: (b, i, 0)),
        out_shape=jax.ShapeDtypeStruct((n_batch, s_len, width), BF16),
        compiler_params=_params(("parallel", "parallel", "parallel")),
        name=name,
    )(*args)


def _gla_mask(reverse):
    gb, ch = GLA_BLOCK, GLA_CHUNK
    r = lax.broadcasted_iota(jnp.int32, (gb, gb), 0)
    c = lax.broadcasted_iota(jnp.int32, (gb, gb), 1)
    return ((r // ch) == (c // ch)) & ((c >= r) if reverse else (c <= r))


def _gla_log_decay(z_ref, aw, ab):
    la = jnp.dot(z_ref[0], aw, preferred_element_type=F32) + ab
    la = _log_sigmoid(la) * (1.0 / GLA_GATE_NORM)
    hi = la.astype(BF16)
    lo = (la - hi.astype(F32)).astype(BF16)
    return jnp.concatenate([hi, lo], axis=1)


def _gla_cumulate(hi_lo, reverse):
    tmat = jnp.where(_gla_mask(reverse), 1.0, 0.0).astype(BF16)
    hw = GLA_HEADS * GLA_HEAD_K
    res = jnp.dot(tmat, hi_lo, preferred_element_type=F32)
    return res[:, :hw] + res[:, hw:]


def _gla_decay(z_ref, aw, ab, reverse):
    return _gla_cumulate(_gla_log_decay(z_ref, aw, ab), reverse)


def _gla_kernel(qf_ref, kf_ref, vf_ref, zf_ref, zfn_ref, qb_ref, kb_ref, vb_ref, zb_ref, zbn_ref, aw_ref, ab_ref,
                of_ref, ob_ref, stf_ref, stb_ref, bc_ref):
    n = pl.program_id(1)

    @pl.when(n == 0)
    def _():
        stf_ref[...] = jnp.zeros_like(stf_ref)
        stb_ref[...] = jnp.zeros_like(stb_ref)
        bc_ref[0] = _gla_decay(zf_ref, aw_ref[0], ab_ref[0], False)
        bc_ref[1] = _gla_decay(zb_ref, aw_ref[1], ab_ref[1], True)

    gb, ch, dk, dv = GLA_BLOCK, GLA_CHUNK, GLA_HEAD_K, GLA_HEAD_V
    n_ch = gb // ch
    nt = (((1,), (1,)), ((), ()))
    dirs = ((qf_ref, kf_ref, vf_ref, zfn_ref, of_ref, stf_ref, False),
            (qb_ref, kb_ref, vb_ref, zbn_ref, ob_ref, stb_ref, True))
    fronts = [(bc_ref[d], _gla_mask(rev)) for d, (_, _, _, _, _, _, rev) in enumerate(dirs)]
    upcoming = [_gla_log_decay(z_next_ref, aw_ref[d], ab_ref[d]) for d, (_, _, _, z_next_ref, _, _, _) in enumerate(dirs)]

    chains = []
    for (q_ref, k_ref, v_ref, _, o_ref, st_ref, rev), (bcum_all, mask) in zip(dirs, fronts):
        order = range(n_ch - 1, -1, -1) if rev else range(n_ch)
        for h in range(GLA_HEADS):
            bcum = bcum_all[:, h * dk:(h + 1) * dk]
            q = q_ref[0, :, h * dk:(h + 1) * dk].astype(F32)
            k = k_ref[0, :, h * dk:(h + 1) * dk].astype(F32)
            q_dec = (q * (dk ** -0.5) * jnp.exp(bcum)).astype(BF16)
            k_inv = (k * jnp.exp(-bcum)).astype(BF16)
            v = v_ref[0, :, h * dv:(h + 1) * dv]
            a = lax.dot_general(q_dec, k_inv, nt, preferred_element_type=F32)
            tots, incs = [], []
            for cidx in order:
                lo_r, hi_r = cidx * ch, (cidx + 1) * ch
                last = lo_r if rev else hi_r - 1
                tot = bcum[last:last + 1, :]
                k_end = k[lo_r:hi_r] * jnp.exp(tot - bcum[lo_r:hi_r])
                incs.append(jnp.dot(k_end.T.astype(BF16), v[lo_r:hi_r], preferred_element_type=F32))
                tots.append(tot)
            pad = jnp.zeros((8 - n_ch, dk), F32)
            g_cols = jnp.exp(jnp.concatenate(tots + [pad], axis=0)).T
            chains.append((o_ref, st_ref, h, order, mask, q_dec, v, a, incs, g_cols))

    upcoming = [_gla_cumulate(hi_lo, rev) for hi_lo, (_, _, _, _, _, _, rev) in zip(upcoming, dirs)]

    stage2 = []
    for o_ref, st_ref, h, order, mask, q_dec, v, a, incs, g_cols in chains:
        a = jnp.where(mask, a, 0.0).astype(BF16)
        st = st_ref[h]
        entering = []
        for i, _ in enumerate(order):
            entering.append(st.astype(BF16))
            st = st * g_cols[:, i:i + 1] + incs[i]
        st_ref[h] = st
        stage2.append((o_ref, h, order, q_dec, v, a, entering))

    for o_ref, h, order, q_dec, v, a, entering in stage2:
        o_intra = jnp.dot(a, v, preferred_element_type=F32)
        for i, cidx in enumerate(order):
            lo_r, hi_r = cidx * ch, (cidx + 1) * ch
            o = o_intra[lo_r:hi_r] + jnp.dot(q_dec[lo_r:hi_r], entering[i], preferred_element_type=F32)
            o_ref[0, lo_r:hi_r, h * dv:(h + 1) * dv] = o.astype(BF16)

    for d, nxt in enumerate(upcoming):
        bc_ref[d] = nxt


def _gla_call(pa, aw, ab, *, layer, t_lat):
    n_batch, s_len, _ = pa.shape
    gb = GLA_BLOCK
    n_lat, n_ctx = t_lat // gb, (s_len - t_lat) // gb
    nblk = n_lat + n_ctx
    qk_w, v_w = GLA_HEADS * GLA_HEAD_K, GLA_HEADS * GLA_HEAD_V

    fwd = lambda n: jnp.where(n < n_ctx, n_lat + n, n - n_ctx)
    bwd = lambda n: jnp.where(n < n_ctx, n_lat + n_ctx - 1 - n, n_lat - 1 - (n - n_ctx))

    def specs(blk):
        return [
            pl.BlockSpec((1, gb, qk_w), lambda b, n: (b, blk(n), PA_GQ // qk_w)),
            pl.BlockSpec((1, gb, qk_w), lambda b, n: (b, blk(n), PA_GK // qk_w)),
            pl.BlockSpec((1, gb, v_w), lambda b, n: (b, blk(n), PA_GV // v_w)),
            pl.BlockSpec((1, gb, LANE), lambda b, n: (b, blk(n), PA_Z // LANE)),
            pl.BlockSpec((1, gb, LANE), lambda b, n: (b, blk(jnp.minimum(n + 1, nblk - 1)), PA_Z // LANE)),
        ]

    out = jax.ShapeDtypeStruct((n_batch, s_len, v_w), BF16)
    state = pltpu.VMEM((GLA_HEADS, GLA_HEAD_K, GLA_HEAD_V), F32)
    decay = pltpu.VMEM((2, gb, qk_w), F32)
    return pl.pallas_call(
        _gla_kernel,
        grid=(n_batch, nblk),
        in_specs=specs(fwd) + specs(bwd) + [
            _layer_spec(layer, aw.shape[1:], lambda b, n: (0, 0, 0)),
            _layer_spec(layer, ab.shape[1:], lambda b, n: (0, 0, 0)),
        ],
        out_specs=[pl.BlockSpec((1, gb, v_w), lambda b, n: (b, fwd(n), 0)),
                   pl.BlockSpec((1, gb, v_w), lambda b, n: (b, bwd(n), 0))],
        out_shape=[out, out],
        scratch_shapes=[state, state, decay],
        compiler_params=_params(("parallel", "arbitrary")),
        name="gla",
    )(pa, pa, pa, pa, pa, pa, pa, pa, pa, pa, aw, ab)


def _with_one_hot(v):
    one_hot = (lax.broadcasted_iota(jnp.int32, v.shape, 1) == 0).astype(v.dtype)
    return jnp.concatenate([v, one_hot], axis=1)


def _swa_kernel(*refs, local, aliased):
    if aliased:
        refs = refs[1:]
    if local:
        q_ref, kp_ref, kc_ref, kn_ref, vp_ref, vc_ref, vn_ref, kx_ref, vx_ref, sink_ref, band_ref, o_ref = refs
    else:
        q_ref, kx_ref, vx_ref, sink_ref, o_ref = refs
    nt = (((1,), (1,)), ((), ()))
    heads = range(SWA_GROUP)
    kx, vx = kx_ref[0], vx_ref[0]
    if local:
        n = pl.program_id(2)
        blk = SWA_BLOCK
        n_rb = q_ref.shape[1] // blk
        band_k = jnp.concatenate([kp_ref[0], kc_ref[0], kn_ref[0]], axis=0)
        band_v = jnp.concatenate([vp_ref[0], vc_ref[0], vn_ref[0]], axis=0)
        col = lax.broadcasted_iota(jnp.int32, (1, band_ref.shape[1]), 1)
        band = band_ref[...]
        sink = jnp.concatenate([jnp.broadcast_to(sink_ref[0, g:g + 1, 0:1], (blk, 1)) for g in heads], axis=0)
        chains = []
        for i in range(n_rb):
            rows = slice(i * blk, (i + 1) * blk)
            q = jnp.concatenate([q_ref[0, rows, g * LANE:(g + 1) * LANE] for g in heads], axis=0)
            keys = jnp.concatenate([band_k[i * blk:(i + 3) * blk], kx], axis=0)
            vals = _with_one_hot(jnp.concatenate([band_v[i * blk:(i + 3) * blk], vx], axis=0))
            bias = band
            if i == 0:
                bias = bias + jnp.where((col < blk) & (n == 0), NEG_INF, 0.0)
            if i == n_rb - 1:
                last = (col >= 2 * blk) & (col < 3 * blk) & (n == pl.num_programs(2) - 1)
                bias = bias + jnp.where(last, NEG_INF, 0.0)
            chains.append(([(g, rows, slice(g * blk, (g + 1) * blk)) for g in heads], q, keys, vals, bias, sink))
    else:
        vals = _with_one_hot(vx)
        chains = [([(g, slice(None), slice(None))], q_ref[0, :, g * LANE:(g + 1) * LANE], kx, vals, None,
                   sink_ref[0, g:g + 1, 0:1]) for g in heads]
    scores = [lax.dot_general(q, keys, nt, preferred_element_type=F32) for _, q, keys, _, _, _ in chains]
    stage2 = []
    for (_, _, _, _, bias, sink), s in zip(chains, scores):
        if bias is not None:
            s = s + bias
        m = jnp.maximum(jnp.max(s, axis=1, keepdims=True), sink)
        stage2.append((jnp.exp2(s - m).astype(BF16), jnp.exp2(sink - m)))
    for (outs, _, _, vals, _, _), (p, p_sink) in zip(chains, stage2):
        o = jnp.dot(p, vals, preferred_element_type=F32)
        o = o[:, :SWA_HEAD_DIM] / (o[:, SWA_HEAD_DIM:SWA_HEAD_DIM + 1] + p_sink)
        for g, dst_rows, src_rows in outs:
            o_ref[0, dst_rows, g * LANE:(g + 1) * LANE] = o[src_rows].astype(BF16)


def _swa_band_table(l_ctx):
    blk = SWA_BLOCK
    r = np.arange(blk)[:, None]
    c = np.arange(3 * blk)[None, :] - blk
    band = np.where(np.abs(r - c) <= SWA_WINDOW, 0.0, NEG_INF)
    band = np.concatenate([band, np.zeros((blk, l_ctx))], axis=1)
    return jnp.asarray(np.tile(band, (SWA_GROUP, 1)), F32)


def _swa_call(pr, pa, sink2, *, layer, t_lat):
    n_batch, s_len, _ = pr.shape
    blk = SWA_BLOCK
    l_ctx = s_len - t_lat
    bq = SWA_QTILE if t_lat % SWA_QTILE == 0 else blk
    per = bq // blk
    n_blk = t_lat // blk
    ctx_blk = t_lat // l_ctx
    gw = SWA_GROUP * LANE
    kcol = lambda kh: PR_SK // LANE + kh
    vcol = lambda kh: PA_SV // LANE + kh
    prev = lambda n: jnp.maximum(n * per - 1, 0)
    nxt = lambda n: jnp.minimum((n + 1) * per, n_blk - 1)
    edge = lambda col, pos: pl.BlockSpec((1, blk, LANE), lambda b, kh, n: (b, pos(n), col(kh)))
    body = lambda col: pl.BlockSpec((1, bq, LANE), lambda b, kh, n: (b, n, col(kh)))
    ctx = lambda col: pl.BlockSpec((1, l_ctx, LANE), lambda b, kh, n: (b, ctx_blk, col(kh)))
    sink_spec = _layer_spec(layer, (1, SWA_GROUP, LANE), lambda b, kh, n: (kh, 0, 0))
    out_shape = jax.ShapeDtypeStruct((n_batch, s_len, SWA_HEADS * SWA_HEAD_DIM), BF16)
    band = _swa_band_table(l_ctx)
    o_lat = pl.pallas_call(
        functools.partial(_swa_kernel, local=True, aliased=False),
        grid=(n_batch, SWA_KV_HEADS, t_lat // bq),
        in_specs=[
            pl.BlockSpec((1, bq, gw), lambda b, kh, n: (b, n, PR_SQ // gw + kh)),
            edge(kcol, prev), body(kcol), edge(kcol, nxt),
            edge(vcol, prev), body(vcol), edge(vcol, nxt),
            ctx(kcol), ctx(vcol), sink_spec,
            pl.BlockSpec(band.shape, lambda b, kh, n: (0, 0)),
        ],
        out_specs=pl.BlockSpec((1, bq, gw), lambda b, kh, n: (b, n, kh)),
        out_shape=out_shape,
        compiler_params=_params(("parallel", "parallel", "parallel")),
        name="swa",
    )(pr, pr, pr, pr, pa, pa, pa, pr, pa, sink2, band)
    return pl.pallas_call(
        functools.partial(_swa_kernel, local=False, aliased=True),
        grid=(n_batch, SWA_KV_HEADS, 1),
        in_specs=[
            pl.BlockSpec(memory_space=pl.ANY),
            pl.BlockSpec((1, l_ctx, gw), lambda b, kh, n: (b, ctx_blk, PR_SQ // gw + kh)),
            ctx(kcol), ctx(vcol), sink_spec,
        ],
        out_specs=pl.BlockSpec((1, l_ctx, gw), lambda b, kh, n: (b, ctx_blk, kh)),
        out_shape=out_shape,
        input_output_aliases={0: 0},
        compiler_params=_params(("parallel", "parallel", "parallel")),
        name="swa_ctx",
    )(o_lat, pr, pr, pa, sink2)


def _mla_proj_kernel(cq_ref, ckv_ref, kpe_ref, qnw_ref, kvnw_ref, wq_ref, wkv_ref, cm_ref, sm_ref,
                     q_ref, k_ref, v_ref):
    cq = cq_ref[0].astype(F32)
    ms = jnp.sum(cq * cq, axis=-1, keepdims=True) * (1.0 / MLA_Q_RANK)
    cqn = (cq * lax.rsqrt(ms + EPS) * qnw_ref[...]).astype(BF16)
    q = jnp.dot(cqn, wq_ref[...], preferred_element_type=F32)
    cos_m, sin_m = cm_ref[...], sm_ref[...]
    qs = MLA_SCALE * LOG2E
    hw = 2 * LANE
    for h in range(MLA_HEADS):
        q_ref[0, :, h * hw:h * hw + LANE] = (q[:, h * hw:h * hw + LANE] * qs).astype(BF16)
        pe = q[:, h * hw + LANE:(h + 1) * hw]
        pe = (pe * cos_m + _swap_halves(pe, MLA_ROPE // 4) * sin_m) * qs
        q_ref[0, :, h * hw + LANE:(h + 1) * hw] = pe.astype(BF16)
    ckv = _rms(ckv_ref[0].astype(F32), kvnw_ref[...]).astype(BF16)
    kv = jnp.dot(ckv, wkv_ref[...], preferred_element_type=F32)
    half = MLA_HEADS * MLA_NOPE
    kpe = kpe_ref[0]
    one_hot = (lax.broadcasted_iota(jnp.int32, kpe.shape, 1) == 0).astype(BF16)
    for h in range(MLA_HEADS):
        k_ref[0, :, h * hw:h * hw + LANE] = kv[:, h * LANE:(h + 1) * LANE].astype(BF16)
        k_ref[0, :, h * hw + LANE:(h + 1) * hw] = kpe
        v_ref[0, :, h * hw:h * hw + LANE] = kv[:, half + h * LANE:half + (h + 1) * LANE].astype(BF16)
        v_ref[0, :, h * hw + LANE:(h + 1) * hw] = one_hot


def _mla_proj_call(pa, pr, qnw, kvnw, wq, wkv, cos_m, sin_m, *, layer):
    n_batch, s_len, _ = pa.shape
    tr = _row_tile(s_len)
    qw = MLA_HEADS * 2 * LANE
    kw = MLA_HEADS * MLA_NOPE
    const = lambda shape: _layer_spec(layer, shape, lambda b, i: (0,) * len(shape))
    wide = pl.BlockSpec((1, tr, qw), lambda b, i: (b, i, 0))
    return pl.pallas_call(
        _mla_proj_kernel,
        grid=(n_batch, s_len // tr),
        in_specs=[
            pl.BlockSpec((1, tr, MCQ_PAD), lambda b, i: (b, i, PA_MCQ // MCQ_PAD)),
            pl.BlockSpec((1, tr, MLA_KV_RANK), lambda b, i: (b, i, PA_MCKV // MLA_KV_RANK)),
            pl.BlockSpec((1, tr, LANE), lambda b, i: (b, i, PR_KPE // LANE)),
            const((1, MCQ_PAD)), const((1, MLA_KV_RANK)),
            const((MCQ_PAD, qw)), const((MLA_KV_RANK, 2 * kw)),
            pl.BlockSpec((tr, LANE), lambda b, i: (i, 0)),
            pl.BlockSpec((tr, LANE), lambda b, i: (i, 0)),
        ],
        out_specs=[wide, wide, wide],
        out_shape=[jax.ShapeDtypeStruct((n_batch, s_len, qw), BF16)] * 3,
        compiler_params=_params(("parallel", "parallel")),
        name="mla_proj",
    )(pa, pa, pr, qnw, kvnw, wq, wkv, cos_m, sin_m)


def _flash_kernel(*refs, aliased, sub, kv_sub):
    if aliased:
        refs = refs[1:]
    q_ref, k_ref, v_ref, o_ref, m_ref, acc_ref = refs
    j = pl.program_id(3)

    @pl.when(j == 0)
    def _():
        m_ref[...] = jnp.full_like(m_ref, NEG_INF)
        acc_ref[...] = jnp.zeros_like(acc_ref)

    bq, bk = q_ref.shape[1], k_ref.shape[1]
    sub = min(sub, bq)
    kv_sub = min(kv_sub, bk)
    chains = [slice(c * sub, (c + 1) * sub) for c in range(bq // sub)]
    for t in range(bk // kv_sub):
        k = k_ref[0, t * kv_sub:(t + 1) * kv_sub, :]
        v = v_ref[0, t * kv_sub:(t + 1) * kv_sub, :]
        scores = [lax.dot_general(q_ref[0, rows, :], k, (((1,), (1,)), ((), ())), preferred_element_type=F32)
                  for rows in chains]
        probs, alphas = [], []
        for rows, s in zip(chains, scores):
            m_prev = m_ref[rows, :]
            m_new = jnp.maximum(m_prev, jnp.max(s, axis=1, keepdims=True))
            alphas.append(jnp.exp2(m_prev - m_new))
            probs.append(jnp.exp2(s - m_new).astype(BF16))
            m_ref[rows, :] = m_new
        for rows, p, alpha in zip(chains, probs, alphas):
            acc_ref[rows, :] = alpha * acc_ref[rows, :] + jnp.dot(p, v, preferred_element_type=F32)

    @pl.when(j == pl.num_programs(3) - 1)
    def _():
        acc = acc_ref[...]
        o_ref[0] = (acc[:, :MLA_V] / acc[:, MLA_V:MLA_V + 1]).astype(BF16)


def _flash_call(q, k, v, prev_out, *, bq, bk, kv_sub, q_blk0, n_q, kv_blk0, n_kv, name):
    n_batch, s_len, _ = q.shape
    aliased = prev_out is not None
    hw = 2 * LANE
    in_specs = [
        pl.BlockSpec((1, bq, hw), lambda b, h, i, j: (b, q_blk0 + i, h)),
        pl.BlockSpec((1, bk, hw), lambda b, h, i, j: (b, kv_blk0 + j, h)),
        pl.BlockSpec((1, bk, hw), lambda b, h, i, j: (b, kv_blk0 + j, h)),
    ]
    args = [q, k, v]
    if aliased:
        in_specs = [pl.BlockSpec(memory_space=pl.ANY)] + in_specs
        args = [prev_out] + args
    return pl.pallas_call(
        functools.partial(_flash_kernel, aliased=aliased, sub=FLASH_SUB, kv_sub=kv_sub),
        grid=(n_batch, MLA_HEADS, n_q, n_kv),
        in_specs=in_specs,
        out_specs=pl.BlockSpec((1, bq, LANE), lambda b, h, i, j: (b, q_blk0 + i, h)),
        out_shape=jax.ShapeDtypeStruct((n_batch, s_len, MLA_HEADS * MLA_V), BF16),
        scratch_shapes=[pltpu.VMEM((bq, 1), F32), pltpu.VMEM((bq, hw), F32)],
        input_output_aliases={0: 0} if aliased else {},
        compiler_params=_params(("parallel", "parallel", "parallel", "arbitrary")),
        name=name,
    )(*args)


def _kv_block(s_len):
    for bk in (3328, 1280, 640, 256, 128):
        if s_len % bk == 0:
            return bk
    raise ValueError(f"unsupported stream length {s_len}")


def _merge_kernel(x_ref, mod_ref, g_ref, gr_ref, of_ref, ob_ref, osw_ref, om_ref, gw_ref, wb_ref, wo_ref,
                  o_ref, *, n_batch, t_lat, tr):
    b = pl.program_id(0)
    row0 = pl.program_id(1) * tr
    gw = gw_ref[...]
    halves = [slice(0, tr // 2), slice(tr // 2, tr)]
    gla = []
    for rows in halves:
        og = of_ref[0, rows, :].astype(F32) + ob_ref[0, rows, :].astype(F32)
        heads = [_rms(og[:, h * GLA_HEAD_V:(h + 1) * GLA_HEAD_V], gw) for h in range(GLA_HEADS)]
        gr = gr_ref[0, rows, :].astype(F32)
        gla.append((jnp.concatenate(heads, axis=1) * (gr * jax.nn.sigmoid(gr))).astype(BF16))
    projected = [[jnp.dot(ob, wb_ref[idx], preferred_element_type=F32)
                  for idx, ob in enumerate((o_gla, osw_ref[0, rows, :], om_ref[0, rows, :]))]
                 for rows, o_gla in zip(halves, gla)]
    merged = []
    for rows, terms in zip(halves, projected):
        total = None
        for idx, term in enumerate(terms):
            gate = jax.nn.sigmoid(g_ref[0, rows, idx * D_MODEL:(idx + 1) * D_MODEL].astype(F32))
            total = gate * term if total is None else total + gate * term
        merged.append(total.astype(BF16))
    outs = [jnp.dot(m, wo_ref[...], preferred_element_type=F32) for m in merged]
    g_m = _mod_rows(mod_ref, b, n_batch, row0, tr, t_lat, 2)
    for rows, y in zip(halves, outs):
        o_ref[0, rows, :] = x_ref[0, rows, :] + g_m[rows] * y


def _merge_call(xa, mod, pa, o_f, o_b, o_swa, o_mla, gw, wb, wo, *, layer, t_lat):
    n_batch, s_len, _ = xa.shape
    tr = _row_tile(s_len)
    row = lambda width, col: pl.BlockSpec((1, tr, width), lambda b, i: (b, i, col))
    resident = dict(pipeline_mode=pl.Buffered(1))
    return pl.pallas_call(
        functools.partial(_merge_kernel, n_batch=n_batch, t_lat=t_lat, tr=tr),
        grid=(n_batch, s_len // tr),
        in_specs=[
            row(D_MODEL, 0),
            _layer_spec(layer, mod.shape[1:], lambda b, i: (0, 0)),
            row(3 * D_MODEL, PA_GATES // (3 * D_MODEL)),
            row(D_MODEL, PA_GR // D_MODEL),
            row(D_MODEL, 0), row(D_MODEL, 0), row(D_MODEL, 0), row(D_MODEL, 0),
            _layer_spec(layer, (1, GLA_HEAD_V), lambda b, i: (0, 0)),
            _layer_spec(layer, (3, D_MODEL, D_MODEL), lambda b, i: (0, 0, 0), **resident),
            _layer_spec(layer, (D_MODEL, D_MODEL), lambda b, i: (0, 0), **resident),
        ],
        out_specs=row(D_MODEL, 0),
        out_shape=jax.ShapeDtypeStruct(xa.shape, F32),
        compiler_params=_params(("parallel", "parallel")),
        name="merge",
    )(xa, mod, pa, pa, o_f, o_b, o_swa, o_mla, gw, wb, wo)


def _mlp_kernel(x_ref, mod_ref, nw_ref, w1_ref, w2_ref, fw_ref, o_ref, h_ref, acc_ref,
                *, n_batch, t_lat, tr, final):
    b = pl.program_id(0)
    row0 = pl.program_id(1) * tr
    c = pl.program_id(2)

    @pl.when(c == 0)
    def _():
        shift = _mod_rows(mod_ref, b, n_batch, row0, tr, t_lat, 3)
        scale = _mod_rows(mod_ref, b, n_batch, row0, tr, t_lat, 4)
        h_ref[...] = (_rms(x_ref[0], nw_ref[...]) * (1.0 + scale) + shift).astype(BF16)
        acc_ref[...] = jnp.zeros_like(acc_ref)

    halves = [slice(0, tr // 2), slice(tr // 2, tr)]
    ups = [jnp.maximum(jnp.dot(h_ref[rows, :], w1_ref[...], preferred_element_type=F32), 0.0) for rows in halves]
    for rows, u in zip(halves, ups):
        acc_ref[rows, :] += jnp.dot((u * u).astype(BF16), w2_ref[...], preferred_element_type=F32)

    @pl.when(c == pl.num_programs(2) - 1)
    def _():
        g_f = _mod_rows(mod_ref, b, n_batch, row0, tr, t_lat, 5)
        y = x_ref[0] + g_f * acc_ref[...]
        if final:
            y = _rms(y, fw_ref[...])
        o_ref[0] = y


def _mlp_call(xa, mod, nw, w1, w2, fw, *, layer, t_lat, final):
    n_batch, s_len, _ = xa.shape
    rows = t_lat if final else s_len
    tr = _row_tile(rows)
    w_mode = dict(pipeline_mode=pl.Buffered(1)) if FF_CHUNK == FF_DIM else {}
    return pl.pallas_call(
        functools.partial(_mlp_kernel, n_batch=n_batch, t_lat=t_lat, tr=tr, final=final),
        grid=(n_batch, rows // tr, FF_DIM // FF_CHUNK),
        in_specs=[
            pl.BlockSpec((1, tr, D_MODEL), lambda b, i, c: (b, i, 0)),
            _layer_spec(layer, mod.shape[1:], lambda b, i, c: (0, 0)),
            _layer_spec(layer, (1, D_MODEL), lambda b, i, c: (0, 0)),
            _layer_spec(layer, (D_MODEL, FF_CHUNK), lambda b, i, c: (0, c), **w_mode),
            _layer_spec(layer, (FF_CHUNK, D_MODEL), lambda b, i, c: (c, 0), **w_mode),
            pl.BlockSpec((1, D_MODEL), lambda b, i, c: (0, 0)),
        ],
        out_specs=pl.BlockSpec((1, tr, D_MODEL), lambda b, i, c: (b, i, 0)),
        out_shape=jax.ShapeDtypeStruct((n_batch, rows, D_MODEL), F32),
        scratch_shapes=[pltpu.VMEM((tr, D_MODEL), BF16), pltpu.VMEM((tr, D_MODEL), F32)],
        compiler_params=_params(("parallel", "parallel", "arbitrary")),
        name="mlp",
    )(xa, mod, nw, w1, w2, fw)


def _rope_tables(t_lat, l_ctx, dim):
    f32 = np.float32
    pos = np.arange(t_lat)
    row, col = pos // GRID_W, pos % GRID_W
    d_axis = dim // 2
    inv = (f32(ROPE_BASE) ** (-np.arange(0, d_axis, 2, dtype=f32) / f32(d_axis))).astype(f32)
    ang_r = row.astype(f32)[:, None] * inv
    ang_c = col.astype(f32)[:, None] * inv
    cos = np.concatenate([np.cos(ang_r)] * 2 + [np.cos(ang_c)] * 2, axis=1)
    sin = np.concatenate([-np.sin(ang_r), np.sin(ang_r), -np.sin(ang_c), np.sin(ang_c)], axis=1)
    if dim < LANE:
        cos = np.concatenate([cos, np.ones((t_lat, LANE - dim), f32)], axis=1)
        sin = np.concatenate([sin, np.zeros((t_lat, LANE - dim), f32)], axis=1)
    cos = np.concatenate([cos, np.ones((l_ctx, LANE), f32)], axis=0)
    sin = np.concatenate([sin, np.zeros((l_ctx, LANE), f32)], axis=0)
    return jnp.asarray(cos, F32), jnp.asarray(sin, F32)


def _split_w_in(w):
    sizes = (512, 512, 1024, 1024, GLA_RANK, GLA_RANK, 1024, 256, 256, MLA_Q_RANK, MLA_KV_RANK, MLA_ROPE,
             3 * D_MODEL)
    offs = np.cumsum((0,) + sizes)
    return [w[..., offs[i]:offs[i + 1]] for i in range(len(sizes))]


def _layer_weights(w_in, gla_a_w, gla_a_b, mla_q_norm_w, mla_w_uq, mla_kv_norm_w, mla_w_ukv, swa_sink):
    depth = w_in.shape[0]
    gq, gk, gv, gr, gzf, gzb, sq, sk, sv, mcq, mckv, mkr, gates = _split_w_in(w_in)
    zpad = lambda n: jnp.zeros((depth, D_MODEL, n), F32)
    w_plain = jnp.concatenate(
        [gates, gr, gv, gq, gk, mcq, zpad(MCQ_PAD - MLA_Q_RANK), sv, mckv, gzf, gzb, zpad(LANE - 2 * GLA_RANK)],
        axis=-1).astype(BF16)
    w_rope = jnp.concatenate([sq, sk, mkr, zpad(LANE - MLA_ROPE)], axis=-1).astype(BF16)
    aw = jnp.zeros((depth, 2, LANE, GLA_HEADS * GLA_HEAD_K), F32)
    aw = aw.at[:, 0, :GLA_RANK].set(gla_a_w[:, 0]).at[:, 1, GLA_RANK:2 * GLA_RANK].set(gla_a_w[:, 1])
    aw = aw.astype(BF16)
    ab = gla_a_b.reshape(depth, 2, 1, GLA_HEADS * GLA_HEAD_K)
    wq = mla_w_uq.reshape(depth, MLA_Q_RANK, MLA_HEADS, MLA_NOPE + MLA_ROPE)
    wq = jnp.pad(wq, ((0, 0), (0, MCQ_PAD - MLA_Q_RANK), (0, 0), (0, 2 * LANE - MLA_NOPE - MLA_ROPE)))
    wq = wq.reshape(depth, MCQ_PAD, MLA_HEADS * 2 * LANE).astype(BF16)
    qnw = jnp.pad(mla_q_norm_w, ((0, 0), (0, MCQ_PAD - MLA_Q_RANK))).reshape(depth, 1, MCQ_PAD)
    wkv = mla_w_ukv.reshape(depth, MLA_KV_RANK, MLA_HEADS, MLA_NOPE + MLA_V)
    wkv = jnp.concatenate([wkv[..., :MLA_NOPE].reshape(depth, MLA_KV_RANK, -1),
                           wkv[..., MLA_NOPE:].reshape(depth, MLA_KV_RANK, -1)], axis=-1).astype(BF16)
    kvnw = mla_kv_norm_w.reshape(depth, 1, MLA_KV_RANK)
    sink2 = jnp.broadcast_to((swa_sink * LOG2E).reshape(depth, SWA_KV_HEADS, SWA_GROUP, 1),
                             (depth, SWA_KV_HEADS, SWA_GROUP, LANE))
    return w_plain, w_rope, aw, ab, wq, qnw, wkv, kvnw, sink2


def kernel(x, c, ctx, c_ctx, ada_w, ada_b, norm_mix_w, w_in, gla_a_w, gla_a_b, gla_norm_w, swa_sink,
           mla_q_norm_w, mla_w_uq, mla_kv_norm_w, mla_w_ukv, w_branch, w_out, norm_mlp_w, mlp_w1, mlp_w2,
           final_norm_w):
    n_batch, t_lat, d_model = x.shape
    l_ctx = ctx.shape[1]
    depth = ada_w.shape[0]
    s_len = t_lat + l_ctx
    assert d_model == D_MODEL and n_batch + 1 <= MOD_ROWS
    assert t_lat % GLA_BLOCK == 0 and l_ctx % GLA_BLOCK == 0 and t_lat % l_ctx == 0 and t_lat % GRID_W == 0

    xa = jnp.concatenate([x, ctx], axis=1)
    cin = jnp.zeros((MOD_ROWS, D_MODEL), F32).at[:n_batch].set(c).at[n_batch].set(c_ctx)
    mod_all = _ada_call(cin, ada_w, ada_b)

    cos_s, sin_s = _rope_tables(t_lat, l_ctx, SWA_HEAD_DIM)
    cos_m, sin_m = _rope_tables(t_lat, l_ctx, MLA_ROPE)
    tabs = (cos_s, sin_s, cos_m, sin_m)

    bq = min(1024, t_lat)
    bk = _kv_block(s_len)

    (w_plain, w_rope, aw, ab, wq, qnw, wkv, kvnw, sink2) = _layer_weights(
        w_in, gla_a_w, gla_a_b, mla_q_norm_w, mla_w_uq, mla_kv_norm_w, mla_w_ukv, swa_sink)
    row_vec = lambda v: v.reshape(depth, 1, -1)
    nw_mix, nw_mlp, gw = row_vec(norm_mix_w), row_vec(norm_mlp_w), row_vec(gla_norm_w)
    wb, wo, w1, w2 = (w.astype(BF16) for w in (w_branch, w_out, mlp_w1, mlp_w2))
    mod = mod_all

    for layer in range(depth):
        pa = _proj_call(xa, mod, nw_mix, w_plain, None, layer=layer, t_lat=t_lat, bn=PA_BN, rope=False,
                        name="proj_plain")
        pr = _proj_call(xa, mod, nw_mix, w_rope, tabs, layer=layer, t_lat=t_lat, bn=PR_WIDTH, rope=True,
                        name="proj_rope")

        o_f, o_b = _gla_call(pa, aw, ab, layer=layer, t_lat=t_lat)
        o_swa = _swa_call(pr, pa, sink2, layer=layer, t_lat=t_lat)

        q, k, v = _mla_proj_call(pa, pr, qnw, kvnw, wq, wkv, cos_m, sin_m, layer=layer)
        o_mla = _flash_call(q, k, v, None, bq=bq, bk=s_len, kv_sub=bk, q_blk0=0, n_q=t_lat // bq, kv_blk0=0,
                            n_kv=1, name="mla_flash")
        o_mla = _flash_call(q, k, v, o_mla, bq=l_ctx, bk=l_ctx, kv_sub=l_ctx, q_blk0=t_lat // l_ctx, n_q=1,
                            kv_blk0=t_lat // l_ctx, n_kv=1, name="mla_flash_ctx")

        xa = _merge_call(xa, mod, pa, o_f, o_b, o_swa, o_mla, gw, wb, wo, layer=layer, t_lat=t_lat)
        xa = _mlp_call(xa, mod, nw_mlp, w1, w2, final_norm_w.reshape(1, -1), layer=layer, t_lat=t_lat,
                       final=layer == depth - 1)
    return xa
```

```python
import functools

import jax
import jax.numpy as jnp
import numpy as np
from jax import lax
from jax.experimental import pallas as pl
from jax.experimental.pallas import tpu as pltpu

F32 = jnp.float32
BF16 = jnp.bfloat16

D_MODEL = 1024
EPS = 1e-6
ROPE_BASE = 10000.0
NEG_INF = -1e30
LOG2E = 1.4426950408889634
GRID_W = 64

GLA_HEADS = 4
GLA_HEAD_K = 128
GLA_HEAD_V = 256
GLA_RANK = 16
GLA_GATE_NORM = 16.0
GLA_CHUNK = 64
GLA_BLOCK = 256

SWA_HEADS = 8
SWA_KV_HEADS = 2
SWA_GROUP = 4
SWA_HEAD_DIM = 128
SWA_WINDOW = 128
SWA_BLOCK = 128
SWA_SCALE = SWA_HEAD_DIM ** -0.5
SWA_QTILE = 2048

MLA_HEADS = 8
MLA_Q_RANK = 384
MLA_KV_RANK = 256
MLA_NOPE = 128
MLA_ROPE = 64
MLA_V = 128
MLA_SCALE = (MLA_NOPE + MLA_ROPE) ** -0.5

FF_DIM = 4 * D_MODEL
FF_CHUNK = 4096
FLASH_SUB = 256

LANE = 128
MOD_ROWS = 8
VMEM_LIMIT = 56 * 1024 * 1024

PA_GATES, PA_GR, PA_GV, PA_GQ, PA_GK = 0, 3072, 4096, 5120, 5632
PA_MCQ, PA_SV, PA_MCKV, PA_Z = 6144, 6656, 6912, 7168
PA_WIDTH = 7296
PA_BN = 2432
MCQ_PAD = 512
PR_SQ, PR_SK, PR_KPE = 0, 1024, 1280
PR_WIDTH = 1408
PR_SLABS = PR_WIDTH // LANE


def _params(sem):
    return pltpu.CompilerParams(dimension_semantics=sem, vmem_limit_bytes=VMEM_LIMIT)


def _layer_spec(layer, block, index_map, **kwargs):
    return pl.BlockSpec((None,) + tuple(block), lambda *g: (layer,) + tuple(index_map(*g)), **kwargs)


def _row_tile(s):
    for tr in (640, 512, 256, 128):
        if s % tr == 0:
            return tr
    raise ValueError(f"unsupported stream length {s}")


def _rms(x, w):
    return x * lax.rsqrt(jnp.mean(x * x, axis=-1, keepdims=True) + EPS) * w


def _mod_rows(mod_ref, b, n_batch, row0, rows, t_lat, idx):
    lo = idx * D_MODEL
    lat = mod_ref[pl.ds(b, 1), lo:lo + D_MODEL]
    ctx = mod_ref[n_batch:n_batch + 1, lo:lo + D_MODEL]
    rid = row0 + lax.broadcasted_iota(jnp.int32, (rows, 1), 0)
    return jnp.where(rid >= t_lat, ctx, lat)


def _swap_halves(x, half):
    lane = lax.broadcasted_iota(jnp.int32, x.shape, 1)
    return jnp.where((lane & half) == 0, pltpu.roll(x, LANE - half, 1), pltpu.roll(x, half, 1))


def _log_sigmoid(x):
    return jnp.minimum(x, 0.0) - jnp.log1p(jnp.exp(-jnp.abs(x)))


def _ada_kernel(c_ref, w_ref, b_ref, o_ref):
    c = c_ref[...]
    a = (c * jax.nn.sigmoid(c)).astype(BF16)
    o_ref[0] = jnp.dot(a, w_ref[0].astype(BF16), preferred_element_type=F32) + b_ref[0]


def _ada_call(cin, ada_w, ada_b):
    depth = ada_w.shape[0]
    nblk = ada_w.shape[2] // D_MODEL
    return pl.pallas_call(
        _ada_kernel,
        grid=(depth, nblk),
        in_specs=[
            pl.BlockSpec((MOD_ROWS, D_MODEL), lambda l, j: (0, 0)),
            pl.BlockSpec((1, D_MODEL, D_MODEL), lambda l, j: (l, 0, j)),
            pl.BlockSpec((1, 1, D_MODEL), lambda l, j: (l, 0, j)),
        ],
        out_specs=pl.BlockSpec((1, MOD_ROWS, D_MODEL), lambda l, j: (l, 0, j)),
        out_shape=jax.ShapeDtypeStruct((depth, MOD_ROWS, ada_w.shape[2]), F32),
        compiler_params=_params(("parallel", "parallel")),
        name="ada_mod",
    )(cin, ada_w, ada_b.reshape(depth, 1, -1))


def _proj_kernel(x_ref, mod_ref, nw_ref, w_ref, *rest, n_batch, t_lat, tr, rope, chunk):
    o_ref = rest[-1]
    b = pl.program_id(1)
    row0 = pl.program_id(2) * tr
    x = x_ref[0]
    shift = _mod_rows(mod_ref, b, n_batch, row0, tr, t_lat, 0)
    scale = _mod_rows(mod_ref, b, n_batch, row0, tr, t_lat, 1)
    h = (_rms(x, nw_ref[...]) * (1.0 + scale) + shift).astype(BF16)
    if not rope:
        for j in range(w_ref.shape[1] // chunk):
            cols = slice(j * chunk, (j + 1) * chunk)
            o_ref[0, :, cols] = jnp.dot(h, w_ref[:, cols], preferred_element_type=F32).astype(BF16)
        return
    acc = jnp.dot(h, w_ref[...], preferred_element_type=F32)
    cs_ref, sn_ref, cm_ref, sm_ref = rest[:4]
    cos_s, sin_s = cs_ref[...], sn_ref[...]
    for s in range(PR_SLABS):
        a = acc[:, s * LANE:(s + 1) * LANE]
        if s * LANE < PR_KPE:
            r = a * cos_s + _swap_halves(a, SWA_HEAD_DIM // 4) * sin_s
            if s * LANE < PR_SK:
                r = r * (SWA_SCALE * LOG2E)
        else:
            r = a * cm_ref[...] + _swap_halves(a, MLA_ROPE // 4) * sm_ref[...]
        o_ref[0, :, s * LANE:(s + 1) * LANE] = r.astype(BF16)


def _proj_call(xa, mod, nw, w, tabs, *, layer, t_lat, bn, rope, name):
    n_batch, s_len, _ = xa.shape
    tr = _row_tile(s_len)
    width = w.shape[-1]
    grid = (1, n_batch, s_len // tr)
    in_specs = [
        pl.BlockSpec((1, tr, D_MODEL), lambda j, b, i: (b, i, 0)),
        _layer_spec(layer, mod.shape[1:], lambda j, b, i: (0, 0)),
        _layer_spec(layer, (1, D_MODEL), lambda j, b, i: (0, 0)),
        _layer_spec(layer, (D_MODEL, width), lambda j, b, i: (0, 0), pipeline_mode=pl.Buffered(1)),
    ]
    args = [xa, mod, nw, w]
    if rope:
        in_specs += [pl.BlockSpec((tr, LANE), lambda j, b, i: (i, 0))] * 4
        args += list(tabs)
    return pl.pallas_call(
        functools.partial(_proj_kernel, n_batch=n_batch, t_lat=t_lat, tr=tr, rope=rope, chunk=bn),
        grid=grid,
        in_specs=in_specs,
        out_specs=pl.BlockSpec((1, tr, width), lambda j, b, i: (b, i, 0)),
        out_shape=jax.ShapeDtypeStruct((n_batch, s_len, width), BF16),
        compiler_params=_params(("parallel", "parallel", "parallel")),
        name=name,
    )(*args)


def _gla_mask(reverse):
    gb, ch = GLA_BLOCK, GLA_CHUNK
    r = lax.broadcasted_iota(jnp.int32, (gb, gb), 0)
    c = lax.broadcasted_iota(jnp.int32, (gb, gb), 1)
    return ((r // ch) == (c // ch)) & ((c >= r) if reverse else (c <= r))


def _gla_log_decay(z_ref, aw, ab):
    la = jnp.dot(z_ref[0], aw, preferred_element_type=F32) + ab
    la = _log_sigmoid(la) * (1.0 / GLA_GATE_NORM)
    hi = la.astype(BF16)
    lo = (la - hi.astype(F32)).astype(BF16)
    return jnp.concatenate([hi, lo], axis=1)


def _gla_cumulate(hi_lo, reverse):
    tmat = jnp.where(_gla_mask(reverse), 1.0, 0.0).astype(BF16)
    hw = GLA_HEADS * GLA_HEAD_K
    res = jnp.dot(tmat, hi_lo, preferred_element_type=F32)
    return res[:, :hw] + res[:, hw:]


def _gla_decay(z_ref, aw, ab, reverse):
    return _gla_cumulate(_gla_log_decay(z_ref, aw, ab), reverse)


def _gla_kernel(qf_ref, kf_ref, vf_ref, zf_ref, zfn_ref, qb_ref, kb_ref, vb_ref, zb_ref, zbn_ref, aw_ref, ab_ref,
                of_ref, ob_ref, stf_ref, stb_ref, bc_ref):
    n = pl.program_id(1)

    @pl.when(n == 0)
    def _():
        stf_ref[...] = jnp.zeros_like(stf_ref)
        stb_ref[...] = jnp.zeros_like(stb_ref)
        bc_ref[0] = _gla_decay(zf_ref, aw_ref[0], ab_ref[0], False)
        bc_ref[1] = _gla_decay(zb_ref, aw_ref[1], ab_ref[1], True)

    gb, ch, dk, dv = GLA_BLOCK, GLA_CHUNK, GLA_HEAD_K, GLA_HEAD_V
    n_ch = gb // ch
    nt = (((1,), (1,)), ((), ()))
    dirs = ((qf_ref, kf_ref, vf_ref, zfn_ref, of_ref, stf_ref, False),
            (qb_ref, kb_ref, vb_ref, zbn_ref, ob_ref, stb_ref, True))
    fronts = [(bc_ref[d], _gla_mask(rev)) for d, (_, _, _, _, _, _, rev) in enumerate(dirs)]
    upcoming = [_gla_log_decay(z_next_ref, aw_ref[d], ab_ref[d]) for d, (_, _, _, z_next_ref, _, _, _) in enumerate(dirs)]

    chains = []
    for (q_ref, k_ref, v_ref, _, o_ref, st_ref, rev), (bcum_all, mask) in zip(dirs, fronts):
        order = range(n_ch - 1, -1, -1) if rev else range(n_ch)
        for h in range(GLA_HEADS):
            bcum = bcum_all[:, h * dk:(h + 1) * dk]
            q = q_ref[0, :, h * dk:(h + 1) * dk].astype(F32)
            k = k_ref[0, :, h * dk:(h + 1) * dk].astype(F32)
            q_dec = (q * (dk ** -0.5) * jnp.exp(bcum)).astype(BF16)
            k_inv = (k * jnp.exp(-bcum)).astype(BF16)
            v = v_ref[0, :, h * dv:(h + 1) * dv]
            a = lax.dot_general(q_dec, k_inv, nt, preferred_element_type=F32)
            tots, incs = [], []
            for cidx in order:
                lo_r, hi_r = cidx * ch, (cidx + 1) * ch
                last = lo_r if rev else hi_r - 1
                tot = bcum[last:last + 1, :]
                k_end = k[lo_r:hi_r] * jnp.exp(tot - bcum[lo_r:hi_r])
                incs.append(jnp.dot(k_end.T.astype(BF16), v[lo_r:hi_r], preferred_element_type=F32))
                tots.append(tot)
            pad = jnp.zeros((8 - n_ch, dk), F32)
            g_cols = jnp.exp(jnp.concatenate(tots + [pad], axis=0)).T
            chains.append((o_ref, st_ref, h, order, mask, q_dec, v, a, incs, g_cols))

    upcoming = [_gla_cumulate(hi_lo, rev) for hi_lo, (_, _, _, _, _, _, rev) in zip(upcoming, dirs)]

    stage2 = []
    for o_ref, st_ref, h, order, mask, q_dec, v, a, incs, g_cols in chains:
        a = jnp.where(mask, a, 0.0).astype(BF16)
        st = st_ref[h]
        entering = []
        for i, _ in enumerate(order):
            entering.append(st.astype(BF16))
            st = st * g_cols[:, i:i + 1] + incs[i]
        st_ref[h] = st
        stage2.append((o_ref, h, order, q_dec, v, a, entering))

    for o_ref, h, order, q_dec, v, a, entering in stage2:
        o_intra = jnp.dot(a, v, preferred_element_type=F32)
        for i, cidx in enumerate(order):
            lo_r, hi_r = cidx * ch, (cidx + 1) * ch
            o = o_intra[lo_r:hi_r] + jnp.dot(q_dec[lo_r:hi_r], entering[i], preferred_element_type=F32)
            o_ref[0, lo_r:hi_r, h * dv:(h + 1) * dv] = o.astype(BF16)

    for d, nxt in enumerate(upcoming):
        bc_ref[d] = nxt


def _gla_call(pa, aw, ab, *, layer, t_lat):
    n_batch, s_len, _ = pa.shape
    gb = GLA_BLOCK
    n_lat, n_ctx = t_lat // gb, (s_len - t_lat) // gb
    nblk = n_lat + n_ctx
    qk_w, v_w = GLA_HEADS * GLA_HEAD_K, GLA_HEADS * GLA_HEAD_V

    fwd = lambda n: jnp.where(n < n_ctx, n_lat + n, n - n_ctx)
    bwd = lambda n: jnp.where(n < n_ctx, n_lat + n_ctx - 1 - n, n_lat - 1 - (n - n_ctx))

    def specs(blk):
        return [
            pl.BlockSpec((1, gb, qk_w), lambda b, n: (b, blk(n), PA_GQ // qk_w)),
            pl.BlockSpec((1, gb, qk_w), lambda b, n: (b, blk(n), PA_GK // qk_w)),
            pl.BlockSpec((1, gb, v_w), lambda b, n: (b, blk(n), PA_GV // v_w)),
            pl.BlockSpec((1, gb, LANE), lambda b, n: (b, blk(n), PA_Z // LANE)),
            pl.BlockSpec((1, gb, LANE), lambda b, n: (b, blk(jnp.minimum(n + 1, nblk - 1)), PA_Z // LANE)),
        ]

    out = jax.ShapeDtypeStruct((n_batch, s_len, v_w), BF16)
    state = pltpu.VMEM((GLA_HEADS, GLA_HEAD_K, GLA_HEAD_V), F32)
    decay = pltpu.VMEM((2, gb, qk_w), F32)
    return pl.pallas_call(
        _gla_kernel,
        grid=(n_batch, nblk),
        in_specs=specs(fwd) + specs(bwd) + [
            _layer_spec(layer, aw.shape[1:], lambda b, n: (0, 0, 0)),
            _layer_spec(layer, ab.shape[1:], lambda b, n: (0, 0, 0)),
        ],
        out_specs=[pl.BlockSpec((1, gb, v_w), lambda b, n: (b, fwd(n), 0)),
                   pl.BlockSpec((1, gb, v_w), lambda b, n: (b, bwd(n), 0))],
        out_shape=[out, out],
        scratch_shapes=[state, state, decay],
        compiler_params=_params(("parallel", "arbitrary")),
        name="gla",
    )(pa, pa, pa, pa, pa, pa, pa, pa, pa, pa, aw, ab)


def _with_one_hot(v):
    one_hot = (lax.broadcasted_iota(jnp.int32, v.shape, 1) == 0).astype(v.dtype)
    return jnp.concatenate([v, one_hot], axis=1)


def _swa_kernel(*refs, local, aliased):
    if aliased:
        refs = refs[1:]
    if local:
        q_ref, kp_ref, kc_ref, kn_ref, vp_ref, vc_ref, vn_ref, kx_ref, vx_ref, sink_ref, band_ref, o_ref = refs
    else:
        q_ref, kx_ref, vx_ref, sink_ref, o_ref = refs
    nt = (((1,), (1,)), ((), ()))
    heads = range(SWA_GROUP)
    kx, vx = kx_ref[0], vx_ref[0]
    if local:
        n = pl.program_id(2)
        blk = SWA_BLOCK
        n_rb = q_ref.shape[1] // blk
        band_k = jnp.concatenate([kp_ref[0], kc_ref[0], kn_ref[0]], axis=0)
        band_v = jnp.concatenate([vp_ref[0], vc_ref[0], vn_ref[0]], axis=0)
        col = lax.broadcasted_iota(jnp.int32, (1, band_ref.shape[1]), 1)
        band = band_ref[...]
        sink = jnp.concatenate([jnp.broadcast_to(sink_ref[0, g:g + 1, 0:1], (blk, 1)) for g in heads], axis=0)
        chains = []
        for i in range(n_rb):
            rows = slice(i * blk, (i + 1) * blk)
            q = jnp.concatenate([q_ref[0, rows, g * LANE:(g + 1) * LANE] for g in heads], axis=0)
            keys = jnp.concatenate([band_k[i * blk:(i + 3) * blk], kx], axis=0)
            vals = _with_one_hot(jnp.concatenate([band_v[i * blk:(i + 3) * blk], vx], axis=0))
            bias = band
            if i == 0:
                bias = bias + jnp.where((col < blk) & (n == 0), NEG_INF, 0.0)
            if i == n_rb - 1:
                last = (col >= 2 * blk) & (col < 3 * blk) & (n == pl.num_programs(2) - 1)
                bias = bias + jnp.where(last, NEG_INF, 0.0)
            chains.append(([(g, rows, slice(g * blk, (g + 1) * blk)) for g in heads], q, keys, vals, bias, sink))
    else:
        vals = _with_one_hot(vx)
        chains = [([(g, slice(None), slice(None))], q_ref[0, :, g * LANE:(g + 1) * LANE], kx, vals, None,
                   sink_ref[0, g:g + 1, 0:1]) for g in heads]
    scores = [lax.dot_general(q, keys, nt, preferred_element_type=F32) for _, q, keys, _, _, _ in chains]
    stage2 = []
    for (_, _, _, _, bias, sink), s in zip(chains, scores):
        if bias is not None:
            s = s + bias
        m = jnp.maximum(jnp.max(s, axis=1, keepdims=True), sink)
        stage2.append((jnp.exp2(s - m).astype(BF16), jnp.exp2(sink - m)))
    for (outs, _, _, vals, _, _), (p, p_sink) in zip(chains, stage2):
        o = jnp.dot(p, vals, preferred_element_type=F32)
        o = o[:, :SWA_HEAD_DIM] / (o[:, SWA_HEAD_DIM:SWA_HEAD_DIM + 1] + p_sink)
        for g, dst_rows, src_rows in outs:
            o_ref[0, dst_rows, g * LANE:(g + 1) * LANE] = o[src_rows].astype(BF16)


def _swa_band_table(l_ctx):
    blk = SWA_BLOCK
    r = np.arange(blk)[:, None]
    c = np.arange(3 * blk)[None, :] - blk
    band = np.where(np.abs(r - c) <= SWA_WINDOW, 0.0, NEG_INF)
    band = np.concatenate([band, np.zeros((blk, l_ctx))], axis=1)
    return jnp.asarray(np.tile(band, (SWA_GROUP, 1)), F32)


def _swa_call(pr, pa, sink2, *, layer, t_lat):
    n_batch, s_len, _ = pr.shape
    blk = SWA_BLOCK
    l_ctx = s_len - t_lat
    bq = SWA_QTILE if t_lat % SWA_QTILE == 0 else blk
    per = bq // blk
    n_blk = t_lat // blk
    ctx_blk = t_lat // l_ctx
    gw = SWA_GROUP * LANE
    kcol = lambda kh: PR_SK // LANE + kh
    vcol = lambda kh: PA_SV // LANE + kh
    prev = lambda n: jnp.maximum(n * per - 1, 0)
    nxt = lambda n: jnp.minimum((n + 1) * per, n_blk - 1)
    edge = lambda col, pos: pl.BlockSpec((1, blk, LANE), lambda b, kh, n: (b, pos(n), col(kh)))
    body = lambda col: pl.BlockSpec((1, bq, LANE), lambda b, kh, n: (b, n, col(kh)))
    ctx = lambda col: pl.BlockSpec((1, l_ctx, LANE), lambda b, kh, n: (b, ctx_blk, col(kh)))
    sink_spec = _layer_spec(layer, (1, SWA_GROUP, LANE), lambda b, kh, n: (kh, 0, 0))
    out_shape = jax.ShapeDtypeStruct((n_batch, s_len, SWA_HEADS * SWA_HEAD_DIM), BF16)
    band = _swa_band_table(l_ctx)
    o_lat = pl.pallas_call(
        functools.partial(_swa_kernel, local=True, aliased=False),
        grid=(n_batch, SWA_KV_HEADS, t_lat // bq),
        in_specs=[
            pl.BlockSpec((1, bq, gw), lambda b, kh, n: (b, n, PR_SQ // gw + kh)),
            edge(kcol, prev), body(kcol), edge(kcol, nxt),
            edge(vcol, prev), body(vcol), edge(vcol, nxt),
            ctx(kcol), ctx(vcol), sink_spec,
            pl.BlockSpec(band.shape, lambda b, kh, n: (0, 0)),
        ],
        out_specs=pl.BlockSpec((1, bq, gw), lambda b, kh, n: (b, n, kh)),
        out_shape=out_shape,
        compiler_params=_params(("parallel", "parallel", "parallel")),
        name="swa",
    )(pr, pr, pr, pr, pa, pa, pa, pr, pa, sink2, band)
    return pl.pallas_call(
        functools.partial(_swa_kernel, local=False, aliased=True),
        grid=(n_batch, SWA_KV_HEADS, 1),
        in_specs=[
            pl.BlockSpec(memory_space=pl.ANY),
            pl.BlockSpec((1, l_ctx, gw), lambda b, kh, n: (b, ctx_blk, PR_SQ // gw + kh)),
            ctx(kcol), ctx(vcol), sink_spec,
        ],
        out_specs=pl.BlockSpec((1, l_ctx, gw), lambda b, kh, n: (b, ctx_blk, kh)),
        out_shape=out_shape,
        input_output_aliases={0: 0},
        compiler_params=_params(("parallel", "parallel", "parallel")),
        name="swa_ctx",
    )(o_lat, pr, pr, pa, sink2)


def _mla_proj_kernel(cq_ref, ckv_ref, kpe_ref, qnw_ref, kvnw_ref, wq_ref, wkv_ref, cm_ref, sm_ref,
                     q_ref, k_ref, v_ref):
    cq = cq_ref[0].astype(F32)
    ms = jnp.sum(cq * cq, axis=-1, keepdims=True) * (1.0 / MLA_Q_RANK)
    cqn = (cq * lax.rsqrt(ms + EPS) * qnw_ref[...]).astype(BF16)
    q = jnp.dot(cqn, wq_ref[...], preferred_element_type=F32)
    cos_m, sin_m = cm_ref[...], sm_ref[...]
    qs = MLA_SCALE * LOG2E
    hw = 2 * LANE
    for h in range(MLA_HEADS):
        q_ref[0, :, h * hw:h * hw + LANE] = (q[:, h * hw:h * hw + LANE] * qs).astype(BF16)
        pe = q[:, h * hw + LANE:(h + 1) * hw]
        pe = (pe * cos_m + _swap_halves(pe, MLA_ROPE // 4) * sin_m) * qs
        q_ref[0, :, h * hw + LANE:(h + 1) * hw] = pe.astype(BF16)
    ckv = _rms(ckv_ref[0].astype(F32), kvnw_ref[...]).astype(BF16)
    kv = jnp.dot(ckv, wkv_ref[...], preferred_element_type=F32)
    half = MLA_HEADS * MLA_NOPE
    kpe = kpe_ref[0]
    one_hot = (lax.broadcasted_iota(jnp.int32, kpe.shape, 1) == 0).astype(BF16)
    for h in range(MLA_HEADS):
        k_ref[0, :, h * hw:h * hw + LANE] = kv[:, h * LANE:(h + 1) * LANE].astype(BF16)
        k_ref[0, :, h * hw + LANE:(h + 1) * hw] = kpe
        v_ref[0, :, h * hw:h * hw + LANE] = kv[:, half + h * LANE:half + (h + 1) * LANE].astype(BF16)
        v_ref[0, :, h * hw + LANE:(h + 1) * hw] = one_hot


def _mla_proj_call(pa, pr, qnw, kvnw, wq, wkv, cos_m, sin_m, *, layer):
    n_batch, s_len, _ = pa.shape
    tr = _row_tile(s_len)
    qw = MLA_HEADS * 2 * LANE
    kw = MLA_HEADS * MLA_NOPE
    const = lambda shape: _layer_spec(layer, shape, lambda b, i: (0,) * len(shape))
    wide = pl.BlockSpec((1, tr, qw), lambda b, i: (b, i, 0))
    return pl.pallas_call(
        _mla_proj_kernel,
        grid=(n_batch, s_len // tr),
        in_specs=[
            pl.BlockSpec((1, tr, MCQ_PAD), lambda b, i: (b, i, PA_MCQ // MCQ_PAD)),
            pl.BlockSpec((1, tr, MLA_KV_RANK), lambda b, i: (b, i, PA_MCKV // MLA_KV_RANK)),
            pl.BlockSpec((1, tr, LANE), lambda b, i: (b, i, PR_KPE // LANE)),
            const((1, MCQ_PAD)), const((1, MLA_KV_RANK)),
            const((MCQ_PAD, qw)), const((MLA_KV_RANK, 2 * kw)),
            pl.BlockSpec((tr, LANE), lambda b, i: (i, 0)),
            pl.BlockSpec((tr, LANE), lambda b, i: (i, 0)),
        ],
        out_specs=[wide, wide, wide],
        out_shape=[jax.ShapeDtypeStruct((n_batch, s_len, qw), BF16)] * 3,
        compiler_params=_params(("parallel", "parallel")),
        name="mla_proj",
    )(pa, pa, pr, qnw, kvnw, wq, wkv, cos_m, sin_m)


def _flash_kernel(*refs, aliased, sub, kv_sub):
    if aliased:
        refs = refs[1:]
    q_ref, k_ref, v_ref, o_ref, m_ref, acc_ref = refs
    j = pl.program_id(3)

    @pl.when(j == 0)
    def _():
        m_ref[...] = jnp.full_like(m_ref, NEG_INF)
        acc_ref[...] = jnp.zeros_like(acc_ref)

    bq, bk = q_ref.shape[1], k_ref.shape[1]
    sub = min(sub, bq)
    kv_sub = min(kv_sub, bk)
    chains = [slice(c * sub, (c + 1) * sub) for c in range(bq // sub)]
    for t in range(bk // kv_sub):
        k = k_ref[0, t * kv_sub:(t + 1) * kv_sub, :]
        v = v_ref[0, t * kv_sub:(t + 1) * kv_sub, :]
        scores = [lax.dot_general(q_ref[0, rows, :], k, (((1,), (1,)), ((), ())), preferred_element_type=F32)
                  for rows in chains]
        probs, alphas = [], []
        for rows, s in zip(chains, scores):
            m_prev = m_ref[rows, :]
            m_new = jnp.maximum(m_prev, jnp.max(s, axis=1, keepdims=True))
            alphas.append(jnp.exp2(m_prev - m_new))
            probs.append(jnp.exp2(s - m_new).astype(BF16))
            m_ref[rows, :] = m_new
        for rows, p, alpha in zip(chains, probs, alphas):
            acc_ref[rows, :] = alpha * acc_ref[rows, :] + jnp.dot(p, v, preferred_element_type=F32)

    @pl.when(j == pl.num_programs(3) - 1)
    def _():
        acc = acc_ref[...]
        o_ref[0] = (acc[:, :MLA_V] / acc[:, MLA_V:MLA_V + 1]).astype(BF16)


def _flash_call(q, k, v, prev_out, *, bq, bk, kv_sub, q_blk0, n_q, kv_blk0, n_kv, name):
    n_batch, s_len, _ = q.shape
    aliased = prev_out is not None
    hw = 2 * LANE
    in_specs = [
        pl.BlockSpec((1, bq, hw), lambda b, h, i, j: (b, q_blk0 + i, h)),
        pl.BlockSpec((1, bk, hw), lambda b, h, i, j: (b, kv_blk0 + j, h)),
        pl.BlockSpec((1, bk, hw), lambda b, h, i, j: (b, kv_blk0 + j, h)),
    ]
    args = [q, k, v]
    if aliased:
        in_specs = [pl.BlockSpec(memory_space=pl.ANY)] + in_specs
        args = [prev_out] + args
    return pl.pallas_call(
        functools.partial(_flash_kernel, aliased=aliased, sub=FLASH_SUB, kv_sub=kv_sub),
        grid=(n_batch, MLA_HEADS, n_q, n_kv),
        in_specs=in_specs,
        out_specs=pl.BlockSpec((1, bq, LANE), lambda b, h, i, j: (b, q_blk0 + i, h)),
        out_shape=jax.ShapeDtypeStruct((n_batch, s_len, MLA_HEADS * MLA_V), BF16),
        scratch_shapes=[pltpu.VMEM((bq, 1), F32), pltpu.VMEM((bq, hw), F32)],
        input_output_aliases={0: 0} if aliased else {},
        compiler_params=_params(("parallel", "parallel", "parallel", "arbitrary")),
        name=name,
    )(*args)


def _kv_block(s_len):
    for bk in (3328, 1280, 640, 256, 128):
        if s_len % bk == 0:
            return bk
    raise ValueError(f"unsupported stream length {s_len}")


def _merge_kernel(x_ref, mod_ref, g_ref, gr_ref, of_ref, ob_ref, osw_ref, om_ref, gw_ref, wb_ref, wo_ref,
                  o_ref, *, n_batch, t_lat, tr):
    b = pl.program_id(0)
    row0 = pl.program_id(1) * tr
    gw = gw_ref[...]
    halves = [slice(0, tr // 2), slice(tr // 2, tr)]
    gla = []
    for rows in halves:
        og = of_ref[0, rows, :].astype(F32) + ob_ref[0, rows, :].astype(F32)
        heads = [_rms(og[:, h * GLA_HEAD_V:(h + 1) * GLA_HEAD_V], gw) for h in range(GLA_HEADS)]
        gr = gr_ref[0, rows, :].astype(F32)
        gla.append((jnp.concatenate(heads, axis=1) * (gr * jax.nn.sigmoid(gr))).astype(BF16))
    projected = [[jnp.dot(ob, wb_ref[idx], preferred_element_type=F32)
                  for idx, ob in enumerate((o_gla, osw_ref[0, rows, :], om_ref[0, rows, :]))]
                 for rows, o_gla in zip(halves, gla)]
    merged = []
    for rows, terms in zip(halves, projected):
        total = None
        for idx, term in enumerate(terms):
            gate = jax.nn.sigmoid(g_ref[0, rows, idx * D_MODEL:(idx + 1) * D_MODEL].astype(F32))
            total = gate * term if total is None else total + gate * term
        merged.append(total.astype(BF16))
    outs = [jnp.dot(m, wo_ref[...], preferred_element_type=F32) for m in merged]
    g_m = _mod_rows(mod_ref, b, n_batch, row0, tr, t_lat, 2)
    for rows, y in zip(halves, outs):
        o_ref[0, rows, :] = x_ref[0, rows, :] + g_m[rows] * y


def _merge_call(xa, mod, pa, o_f, o_b, o_swa, o_mla, gw, wb, wo, *, layer, t_lat):
    n_batch, s_len, _ = xa.shape
    tr = _row_tile(s_len)
    row = lambda width, col: pl.BlockSpec((1, tr, width), lambda b, i: (b, i, col))
    resident = dict(pipeline_mode=pl.Buffered(1))
    return pl.pallas_call(
        functools.partial(_merge_kernel, n_batch=n_batch, t_lat=t_lat, tr=tr),
        grid=(n_batch, s_len // tr),
        in_specs=[
            row(D_MODEL, 0),
            _layer_spec(layer, mod.shape[1:], lambda b, i: (0, 0)),
            row(3 * D_MODEL, PA_GATES // (3 * D_MODEL)),
            row(D_MODEL, PA_GR // D_MODEL),
            row(D_MODEL, 0), row(D_MODEL, 0), row(D_MODEL, 0), row(D_MODEL, 0),
            _layer_spec(layer, (1, GLA_HEAD_V), lambda b, i: (0, 0)),
            _layer_spec(layer, (3, D_MODEL, D_MODEL), lambda b, i: (0, 0, 0), **resident),
            _layer_spec(layer, (D_MODEL, D_MODEL), lambda b, i: (0, 0), **resident),
        ],
        out_specs=row(D_MODEL, 0),
        out_shape=jax.ShapeDtypeStruct(xa.shape, F32),
        compiler_params=_params(("parallel", "parallel")),
        name="merge",
    )(xa, mod, pa, pa, o_f, o_b, o_swa, o_mla, gw, wb, wo)


def _mlp_kernel(x_ref, mod_ref, nw_ref, w1_ref, w2_ref, fw_ref, o_ref, h_ref, acc_ref,
                *, n_batch, t_lat, tr, final):
    b = pl.program_id(0)
    row0 = pl.program_id(1) * tr
    c = pl.program_id(2)

    @pl.when(c == 0)
    def _():
        shift = _mod_rows(mod_ref, b, n_batch, row0, tr, t_lat, 3)
        scale = _mod_rows(mod_ref, b, n_batch, row0, tr, t_lat, 4)
        h_ref[...] = (_rms(x_ref[0], nw_ref[...]) * (1.0 + scale) + shift).astype(BF16)
        acc_ref[...] = jnp.zeros_like(acc_ref)

    halves = [slice(0, tr // 2), slice(tr // 2, tr)]
    ups = [jnp.maximum(jnp.dot(h_ref[rows, :], w1_ref[...], preferred_element_type=F32), 0.0) for rows in halves]
    for rows, u in zip(halves, ups):
        acc_ref[rows, :] += jnp.dot((u * u).astype(BF16), w2_ref[...], preferred_element_type=F32)

    @pl.when(c == pl.num_programs(2) - 1)
    def _():
        g_f = _mod_rows(mod_ref, b, n_batch, row0, tr, t_lat, 5)
        y = x_ref[0] + g_f * acc_ref[...]
        if final:
            y = _rms(y, fw_ref[...])
        o_ref[0] = y


def _mlp_call(xa, mod, nw, w1, w2, fw, *, layer, t_lat, final):
    n_batch, s_len, _ = xa.shape
    rows = t_lat if final else s_len
    tr = _row_tile(rows)
    w_mode = dict(pipeline_mode=pl.Buffered(1)) if FF_CHUNK == FF_DIM else {}
    return pl.pallas_call(
        functools.partial(_mlp_kernel, n_batch=n_batch, t_lat=t_lat, tr=tr, final=final),
        grid=(n_batch, rows // tr, FF_DIM // FF_CHUNK),
        in_specs=[
            pl.BlockSpec((1, tr, D_MODEL), lambda b, i, c: (b, i, 0)),
            _layer_spec(layer, mod.shape[1:], lambda b, i, c: (0, 0)),
            _layer_spec(layer, (1, D_MODEL), lambda b, i, c: (0, 0)),
            _layer_spec(layer, (D_MODEL, FF_CHUNK), lambda b, i, c: (0, c), **w_mode),
            _layer_spec(layer, (FF_CHUNK, D_MODEL), lambda b, i, c: (c, 0), **w_mode),
            pl.BlockSpec((1, D_MODEL), lambda b, i, c: (0, 0)),
        ],
        out_specs=pl.BlockSpec((1, tr, D_MODEL), lambda b, i, c: (b, i, 0)),
        out_shape=jax.ShapeDtypeStruct((n_batch, rows, D_MODEL), F32),
        scratch_shapes=[pltpu.VMEM((tr, D_MODEL), BF16), pltpu.VMEM((tr, D_MODEL), F32)],
        compiler_params=_params(("parallel", "parallel", "arbitrary")),
        name="mlp",
    )(xa, mod, nw, w1, w2, fw)


def _rope_tables(t_lat, l_ctx, dim):
    f32 = np.float32
    pos = np.arange(t_lat)
    row, col = pos // GRID_W, pos % GRID_W
    d_axis = dim // 2
    inv = (f32(ROPE_BASE) ** (-np.arange(0, d_axis, 2, dtype=f32) / f32(d_axis))).astype(f32)
    ang_r = row.astype(f32)[:, None] * inv
    ang_c = col.astype(f32)[:, None] * inv
    cos = np.concatenate([np.cos(ang_r)] * 2 + [np.cos(ang_c)] * 2, axis=1)
    sin = np.concatenate([-np.sin(ang_r), np.sin(ang_r), -np.sin(ang_c), np.sin(ang_c)], axis=1)
    if dim < LANE:
        cos = np.concatenate([cos, np.ones((t_lat, LANE - dim), f32)], axis=1)
        sin = np.concatenate([sin, np.zeros((t_lat, LANE - dim), f32)], axis=1)
    cos = np.concatenate([cos, np.ones((l_ctx, LANE), f32)], axis=0)
    sin = np.concatenate([sin, np.zeros((l_ctx, LANE), f32)], axis=0)
    return jnp.asarray(cos, F32), jnp.asarray(sin, F32)


def _split_w_in(w):
    sizes = (512, 512, 1024, 1024, GLA_RANK, GLA_RANK, 1024, 256, 256, MLA_Q_RANK, MLA_KV_RANK, MLA_ROPE,
             3 * D_MODEL)
    offs = np.cumsum((0,) + sizes)
    return [w[..., offs[i]:offs[i + 1]] for i in range(len(sizes))]


def _layer_weights(w_in, gla_a_w, gla_a_b, mla_q_norm_w, mla_w_uq, mla_kv_norm_w, mla_w_ukv, swa_sink):
    depth = w_in.shape[0]
    gq, gk, gv, gr, gzf, gzb, sq, sk, sv, mcq, mckv, mkr, gates = _split_w_in(w_in)
    zpad = lambda n: jnp.zeros((depth, D_MODEL, n), F32)
    w_plain = jnp.concatenate(
        [gates, gr, gv, gq, gk, mcq, zpad(MCQ_PAD - MLA_Q_RANK), sv, mckv, gzf, gzb, zpad(LANE - 2 * GLA_RANK)],
        axis=-1).astype(BF16)
    w_rope = jnp.concatenate([sq, sk, mkr, zpad(LANE - MLA_ROPE)], axis=-1).astype(BF16)
    aw = jnp.zeros((depth, 2, LANE, GLA_HEADS * GLA_HEAD_K), F32)
    aw = aw.at[:, 0, :GLA_RANK].set(gla_a_w[:, 0]).at[:, 1, GLA_RANK:2 * GLA_RANK].set(gla_a_w[:, 1])
    aw = aw.astype(BF16)
    ab = gla_a_b.reshape(depth, 2, 1, GLA_HEADS * GLA_HEAD_K)
    wq = mla_w_uq.reshape(depth, MLA_Q_RANK, MLA_HEADS, MLA_NOPE + MLA_ROPE)
    wq = jnp.pad(wq, ((0, 0), (0, MCQ_PAD - MLA_Q_RANK), (0, 0), (0, 2 * LANE - MLA_NOPE - MLA_ROPE)))
    wq = wq.reshape(depth, MCQ_PAD, MLA_HEADS * 2 * LANE).astype(BF16)
    qnw = jnp.pad(mla_q_norm_w, ((0, 0), (0, MCQ_PAD - MLA_Q_RANK))).reshape(depth, 1, MCQ_PAD)
    wkv = mla_w_ukv.reshape(depth, MLA_KV_RANK, MLA_HEADS, MLA_NOPE + MLA_V)
    wkv = jnp.concatenate([wkv[..., :MLA_NOPE].reshape(depth, MLA_KV_RANK, -1),
                           wkv[..., MLA_NOPE:].reshape(depth, MLA_KV_RANK, -1)], axis=-1).astype(BF16)
    kvnw = mla_kv_norm_w.reshape(depth, 1, MLA_KV_RANK)
    sink2 = jnp.broadcast_to((swa_sink * LOG2E).reshape(depth, SWA_KV_HEADS, SWA_GROUP, 1),
                             (depth, SWA_KV_HEADS, SWA_GROUP, LANE))
    return w_plain, w_rope, aw, ab, wq, qnw, wkv, kvnw, sink2


def kernel(x, c, ctx, c_ctx, ada_w, ada_b, norm_mix_w, w_in, gla_a_w, gla_a_b, gla_norm_w, swa_sink,
           mla_q_norm_w, mla_w_uq, mla_kv_norm_w, mla_w_ukv, w_branch, w_out, norm_mlp_w, mlp_w1, mlp_w2,
           final_norm_w):
    n_batch, t_lat, d_model = x.shape
    l_ctx = ctx.shape[1]
    depth = ada_w.shape[0]
    s_len = t_lat + l_ctx
    assert d_model == D_MODEL and n_batch + 1 <= MOD_ROWS
    assert t_lat % GLA_BLOCK == 0 and l_ctx % GLA_BLOCK == 0 and t_lat % l_ctx == 0 and t_lat % GRID_W == 0

    xa = jnp.concatenate([x, ctx], axis=1)
    cin = jnp.zeros((MOD_ROWS, D_MODEL), F32).at[:n_batch].set(c).at[n_batch].set(c_ctx)
    mod_all = _ada_call(cin, ada_w, ada_b)

    cos_s, sin_s = _rope_tables(t_lat, l_ctx, SWA_HEAD_DIM)
    cos_m, sin_m = _rope_tables(t_lat, l_ctx, MLA_ROPE)
    tabs = (cos_s, sin_s, cos_m, sin_m)

    bq = min(1024, t_lat)
    bk = _kv_block(s_len)

    (w_plain, w_rope, aw, ab, wq, qnw, wkv, kvnw, sink2) = _layer_weights(
        w_in, gla_a_w, gla_a_b, mla_q_norm_w, mla_w_uq, mla_kv_norm_w, mla_w_ukv, swa_sink)
    row_vec = lambda v: v.reshape(depth, 1, -1)
    nw_mix, nw_mlp, gw = row_vec(norm_mix_w), row_vec(norm_mlp_w), row_vec(gla_norm_w)
    wb, wo, w1, w2 = (w.astype(BF16) for w in (w_branch, w_out, mlp_w1, mlp_w2))
    mod = mod_all

    for layer in range(depth):
        pa = _proj_call(xa, mod, nw_mix, w_plain, None, layer=layer, t_lat=t_lat, bn=PA_BN, rope=False,
                        name="proj_plain")
        pr = _proj_call(xa, mod, nw_mix, w_rope, tabs, layer=layer, t_lat=t_lat, bn=PR_WIDTH, rope=True,
                        name="proj_rope")

        o_f, o_b = _gla_call(pa, aw, ab, layer=layer, t_lat=t_lat)
        o_swa = _swa_call(pr, pa, sink2, layer=layer, t_lat=t_lat)

        q, k, v = _mla_proj_call(pa, pr, qnw, kvnw, wq, wkv, cos_m, sin_m, layer=layer)
        o_mla = _flash_call(q, k, v, None, bq=bq, bk=s_len, kv_sub=bk, q_blk0=0, n_q=t_lat // bq, kv_blk0=0,
                            n_kv=1, name="mla_flash")
        o_mla = _flash_call(q, k, v, o_mla, bq=l_ctx, bk=l_ctx, kv_sub=l_ctx, q_blk0=t_lat // l_ctx, n_q=1,
                            kv_blk0=t_lat // l_ctx, n_kv=1, name="mla_flash_ctx")

        xa = _merge_call(xa, mod, pa, o_f, o_b, o_swa, o_mla, gw, wb, wo, layer=layer, t_lat=t_lat)
        xa = _mlp_call(xa, mod, nw_mlp, w1, w2, final_norm_w.reshape(1, -1), layer=layer, t_lat=t_lat,
                       final=layer == depth - 1)
    return xa
```

```python
import functools

import jax
import jax.numpy as jnp
import numpy as np
from jax import lax
from jax.experimental import pallas as pl
from jax.experimental.pallas import tpu as pltpu

F32 = jnp.float32
BF16 = jnp.bfloat16

D_MODEL = 1024
EPS = 1e-6
ROPE_BASE = 10000.0
NEG_INF = -1e30
LOG2E = 1.4426950408889634
GRID_W = 64

GLA_HEADS = 4
GLA_HEAD_K = 128
GLA_HEAD_V = 256
GLA_RANK = 16
GLA_GATE_NORM = 16.0
GLA_CHUNK = 64
GLA_BLOCK = 256

SWA_HEADS = 8
SWA_KV_HEADS = 2
SWA_GROUP = 4
SWA_HEAD_DIM = 128
SWA_WINDOW = 128
SWA_BLOCK = 128
SWA_SCALE = SWA_HEAD_DIM ** -0.5
SWA_QTILE = 2048

MLA_HEADS = 8
MLA_Q_RANK = 384
MLA_KV_RANK = 256
MLA_NOPE = 128
MLA_ROPE = 64
MLA_V = 128
MLA_SCALE = (MLA_NOPE + MLA_ROPE) ** -0.5

FF_DIM = 4 * D_MODEL
FLASH_SUB = 256

LANE = 128
MOD_ROWS = 8
VMEM_LIMIT = 56 * 1024 * 1024

PA_GATES, PA_GR, PA_GV, PA_GQ, PA_GK = 0, 3072, 4096, 5120, 5632
PA_MCQ, PA_SV, PA_MCKV, PA_Z = 6144, 6656, 6912, 7168
PA_WIDTH = 7296
PA_BN = 2432
MCQ_PAD = 512
PR_SQ, PR_SK, PR_KPE = 0, 1024, 1280
PR_WIDTH = 1408
PR_SLABS = PR_WIDTH // LANE


def _params(sem):
    return pltpu.CompilerParams(dimension_semantics=sem, vmem_limit_bytes=VMEM_LIMIT)


def _layer_spec(layer, block, index_map, **kwargs):
    return pl.BlockSpec((None,) + tuple(block), lambda *g: (layer,) + tuple(index_map(*g)), **kwargs)


def _row_tile(s):
    for tr in (640, 512, 256, 128):
        if s % tr == 0:
            return tr
    raise ValueError(f"unsupported stream length {s}")


def _rms(x, w):
    return x * lax.rsqrt(jnp.mean(x * x, axis=-1, keepdims=True) + EPS) * w


def _mod_rows(mod_ref, b, n_batch, row0, rows, t_lat, idx):
    lo = idx * D_MODEL
    lat = mod_ref[pl.ds(b, 1), lo:lo + D_MODEL]
    ctx = mod_ref[n_batch:n_batch + 1, lo:lo + D_MODEL]
    rid = row0 + lax.broadcasted_iota(jnp.int32, (rows, 1), 0)
    return jnp.where(rid >= t_lat, ctx, lat)


def _swap_halves(x, half):
    lane = lax.broadcasted_iota(jnp.int32, x.shape, 1)
    return jnp.where((lane & half) == 0, pltpu.roll(x, LANE - half, 1), pltpu.roll(x, half, 1))


def _log_sigmoid(x):
    return jnp.minimum(x, 0.0) - jnp.log1p(jnp.exp(-jnp.abs(x)))


def _ada_kernel(c_ref, w_ref, b_ref, o_ref):
    c = c_ref[...]
    a = (c * jax.nn.sigmoid(c)).astype(BF16)
    o_ref[0] = jnp.dot(a, w_ref[0].astype(BF16), preferred_element_type=F32) + b_ref[0]


def _ada_call(cin, ada_w, ada_b):
    depth = ada_w.shape[0]
    nblk = ada_w.shape[2] // D_MODEL
    return pl.pallas_call(
        _ada_kernel,
        grid=(depth, nblk),
        in_specs=[
            pl.BlockSpec((MOD_ROWS, D_MODEL), lambda l, j: (0, 0)),
            pl.BlockSpec((1, D_MODEL, D_MODEL), lambda l, j: (l, 0, j)),
            pl.BlockSpec((1, 1, D_MODEL), lambda l, j: (l, 0, j)),
        ],
        out_specs=pl.BlockSpec((1, MOD_ROWS, D_MODEL), lambda l, j: (l, 0, j)),
        out_shape=jax.ShapeDtypeStruct((depth, MOD_ROWS, ada_w.shape[2]), F32),
        compiler_params=_params(("parallel", "parallel")),
        name="ada_mod",
    )(cin, ada_w, ada_b.reshape(depth, 1, -1))


def _proj_kernel(x_ref, mod_ref, nw_ref, w_ref, *rest, n_batch, t_lat, tr, rope, chunk):
    o_ref = rest[-1]
    b = pl.program_id(1)
    row0 = pl.program_id(2) * tr
    x = x_ref[0]
    shift = _mod_rows(mod_ref, b, n_batch, row0, tr, t_lat, 0)
    scale = _mod_rows(mod_ref, b, n_batch, row0, tr, t_lat, 1)
    h = (_rms(x, nw_ref[...]) * (1.0 + scale) + shift).astype(BF16)
    if not rope:
        for j in range(w_ref.shape[1] // chunk):
            cols = slice(j * chunk, (j + 1) * chunk)
            o_ref[0, :, cols] = jnp.dot(h, w_ref[:, cols], preferred_element_type=F32).astype(BF16)
        return
    acc = jnp.dot(h, w_ref[...], preferred_element_type=F32)
    cs_ref, sn_ref, cm_ref, sm_ref = rest[:4]
    cos_s, sin_s = cs_ref[...], sn_ref[...]
    for s in range(PR_SLABS):
        a = acc[:, s * LANE:(s + 1) * LANE]
        if s * LANE < PR_KPE:
            r = a * cos_s + _swap_halves(a, SWA_HEAD_DIM // 4) * sin_s
            if s * LANE < PR_SK:
                r = r * (SWA_SCALE * LOG2E)
        else:
            r = a * cm_ref[...] + _swap_halves(a, MLA_ROPE // 4) * sm_ref[...]
        o_ref[0, :, s * LANE:(s + 1) * LANE] = r.astype(BF16)


def _proj_call(xa, mod, nw, w, tabs, *, layer, t_lat, bn, rope, name):
    n_batch, s_len, _ = xa.shape
    tr = _row_tile(s_len)
    width = w.shape[-1]
    grid = (1, n_batch, s_len // tr)
    in_specs = [
        pl.BlockSpec((1, tr, D_MODEL), lambda j, b, i: (b, i, 0)),
        _layer_spec(layer, mod.shape[1:], lambda j, b, i: (0, 0)),
        _layer_spec(layer, (1, D_MODEL), lambda j, b, i: (0, 0)),
        _layer_spec(layer, (D_MODEL, width), lambda j, b, i: (0, 0), pipeline_mode=pl.Buffered(1)),
    ]
    args = [xa, mod, nw, w]
    if rope:
        in_specs += [pl.BlockSpec((tr, LANE), lambda j, b, i: (i, 0))] * 4
        args += list(tabs)
    return pl.pallas_call(
        functools.partial(_proj_kernel, n_batch=n_batch, t_lat=t_lat, tr=tr, rope=rope, chunk=bn),
        grid=grid,
        in_specs=in_specs,
        out_specs=pl.BlockSpec((1, tr, width), lambda j, b, i: (b, i, 0)),
        out_shape=jax.ShapeDtypeStruct((n_batch, s_len, width), BF16),
        compiler_params=_params(("parallel", "parallel", "parallel")),
        name=name,
    )(*args)


def _gla_mask(reverse):
    gb, ch = GLA_BLOCK, GLA_CHUNK
    r = lax.broadcasted_iota(jnp.int32, (gb, gb), 0)
    c = lax.broadcasted_iota(jnp.int32, (gb, gb), 1)
    return ((r // ch) == (c // ch)) & ((c >= r) if reverse else (c <= r))


def _gla_log_decay(z_ref, aw, ab):
    la = jnp.dot(z_ref[0], aw, preferred_element_type=F32) + ab
    la = _log_sigmoid(la) * (1.0 / GLA_GATE_NORM)
    hi = la.astype(BF16)
    lo = (la - hi.astype(F32)).astype(BF16)
    return jnp.concatenate([hi, lo], axis=1)


def _gla_cumulate(hi_lo, reverse):
    tmat = jnp.where(_gla_mask(reverse), 1.0, 0.0).astype(BF16)
    hw = GLA_HEADS * GLA_HEAD_K
    res = jnp.dot(tmat, hi_lo, preferred_element_type=F32)
    return res[:, :hw] + res[:, hw:]


def _gla_decay(z_ref, aw, ab, reverse):
    return _gla_cumulate(_gla_log_decay(z_ref, aw, ab), reverse)


def _gla_kernel(qf_ref, kf_ref, vf_ref, zf_ref, zfn_ref, qb_ref, kb_ref, vb_ref, zb_ref, zbn_ref, aw_ref, ab_ref,
                of_ref, ob_ref, stf_ref, stb_ref, bc_ref):
    n = pl.program_id(1)

    @pl.when(n == 0)
    def _():
        stf_ref[...] = jnp.zeros_like(stf_ref)
        stb_ref[...] = jnp.zeros_like(stb_ref)
        bc_ref[0] = _gla_decay(zf_ref, aw_ref[0], ab_ref[0], False)
        bc_ref[1] = _gla_decay(zb_ref, aw_ref[1], ab_ref[1], True)

    gb, ch, dk, dv = GLA_BLOCK, GLA_CHUNK, GLA_HEAD_K, GLA_HEAD_V
    n_ch = gb // ch
    nt = (((1,), (1,)), ((), ()))
    dirs = ((qf_ref, kf_ref, vf_ref, zfn_ref, of_ref, stf_ref, False),
            (qb_ref, kb_ref, vb_ref, zbn_ref, ob_ref, stb_ref, True))
    fronts = [(bc_ref[d], _gla_mask(rev)) for d, (_, _, _, _, _, _, rev) in enumerate(dirs)]
    upcoming = [_gla_log_decay(z_next_ref, aw_ref[d], ab_ref[d]) for d, (_, _, _, z_next_ref, _, _, _) in enumerate(dirs)]

    chains = []
    for (q_ref, k_ref, v_ref, _, o_ref, st_ref, rev), (bcum_all, mask) in zip(dirs, fronts):
        order = range(n_ch - 1, -1, -1) if rev else range(n_ch)
        for h in range(GLA_HEADS):
            bcum = bcum_all[:, h * dk:(h + 1) * dk]
            q = q_ref[0, :, h * dk:(h + 1) * dk].astype(F32)
            k = k_ref[0, :, h * dk:(h + 1) * dk].astype(F32)
            q_dec = (q * (dk ** -0.5) * jnp.exp(bcum)).astype(BF16)
            k_inv = (k * jnp.exp(-bcum)).astype(BF16)
            v = v_ref[0, :, h * dv:(h + 1) * dv]
            a = lax.dot_general(q_dec, k_inv, nt, preferred_element_type=F32)
            tots, incs = [], []
            for cidx in order:
                lo_r, hi_r = cidx * ch, (cidx + 1) * ch
                last = lo_r if rev else hi_r - 1
                tot = bcum[last:last + 1, :]
                k_end = k[lo_r:hi_r] * jnp.exp(tot - bcum[lo_r:hi_r])
                incs.append(jnp.dot(k_end.T.astype(BF16), v[lo_r:hi_r], preferred_element_type=F32))
                tots.append(tot)
            pad = jnp.zeros((8 - n_ch, dk), F32)
            g_cols = jnp.exp(jnp.concatenate(tots + [pad], axis=0)).T
            chains.append((o_ref, st_ref, h, order, mask, q_dec, v, a, incs, g_cols))

    upcoming = [_gla_cumulate(hi_lo, rev) for hi_lo, (_, _, _, _, _, _, rev) in zip(upcoming, dirs)]

    stage2 = []
    for o_ref, st_ref, h, order, mask, q_dec, v, a, incs, g_cols in chains:
        a = jnp.where(mask, a, 0.0).astype(BF16)
        st = st_ref[h]
        entering = []
        for i, _ in enumerate(order):
            entering.append(st.astype(BF16))
            st = st * g_cols[:, i:i + 1] + incs[i]
        st_ref[h] = st
        stage2.append((o_ref, h, order, q_dec, v, a, entering))

    for o_ref, h, order, q_dec, v, a, entering in stage2:
        o_intra = jnp.dot(a, v, preferred_element_type=F32)
        for i, cidx in enumerate(order):
            lo_r, hi_r = cidx * ch, (cidx + 1) * ch
            o = o_intra[lo_r:hi_r] + jnp.dot(q_dec[lo_r:hi_r], entering[i], preferred_element_type=F32)
            o_ref[0, lo_r:hi_r, h * dv:(h + 1) * dv] = o.astype(BF16)

    for d, nxt in enumerate(upcoming):
        bc_ref[d] = nxt


def _gla_call(pa, aw, ab, *, layer, t_lat):
    n_batch, s_len, _ = pa.shape
    gb = GLA_BLOCK
    n_lat, n_ctx = t_lat // gb, (s_len - t_lat) // gb
    nblk = n_lat + n_ctx
    qk_w, v_w = GLA_HEADS * GLA_HEAD_K, GLA_HEADS * GLA_HEAD_V

    fwd = lambda n: jnp.where(n < n_ctx, n_lat + n, n - n_ctx)
    bwd = lambda n: jnp.where(n < n_ctx, n_lat + n_ctx - 1 - n, n_lat - 1 - (n - n_ctx))

    def specs(blk):
        return [
            pl.BlockSpec((1, gb, qk_w), lambda b, n: (b, blk(n), PA_GQ // qk_w)),
            pl.BlockSpec((1, gb, qk_w), lambda b, n: (b, blk(n), PA_GK // qk_w)),
            pl.BlockSpec((1, gb, v_w), lambda b, n: (b, blk(n), PA_GV // v_w)),
            pl.BlockSpec((1, gb, LANE), lambda b, n: (b, blk(n), PA_Z // LANE)),
            pl.BlockSpec((1, gb, LANE), lambda b, n: (b, blk(jnp.minimum(n + 1, nblk - 1)), PA_Z // LANE)),
        ]

    out = jax.ShapeDtypeStruct((n_batch, s_len, v_w), BF16)
    state = pltpu.VMEM((GLA_HEADS, GLA_HEAD_K, GLA_HEAD_V), F32)
    decay = pltpu.VMEM((2, gb, qk_w), F32)
    return pl.pallas_call(
        _gla_kernel,
        grid=(n_batch, nblk),
        in_specs=specs(fwd) + specs(bwd) + [
            _layer_spec(layer, aw.shape[1:], lambda b, n: (0, 0, 0)),
            _layer_spec(layer, ab.shape[1:], lambda b, n: (0, 0, 0)),
        ],
        out_specs=[pl.BlockSpec((1, gb, v_w), lambda b, n: (b, fwd(n), 0)),
                   pl.BlockSpec((1, gb, v_w), lambda b, n: (b, bwd(n), 0))],
        out_shape=[out, out],
        scratch_shapes=[state, state, decay],
        compiler_params=_params(("parallel", "arbitrary")),
        name="gla",
    )(pa, pa, pa, pa, pa, pa, pa, pa, pa, pa, aw, ab)


def _with_one_hot(v):
    one_hot = (lax.broadcasted_iota(jnp.int32, v.shape, 1) == 0).astype(v.dtype)
    return jnp.concatenate([v, one_hot], axis=1)


def _swa_kernel(*refs, local, aliased):
    if aliased:
        refs = refs[1:]
    if local:
        q_ref, kp_ref, kc_ref, kn_ref, vp_ref, vc_ref, vn_ref, kx_ref, vx_ref, sink_ref, band_ref, o_ref = refs
    else:
        q_ref, kx_ref, vx_ref, sink_ref, o_ref = refs
    nt = (((1,), (1,)), ((), ()))
    heads = range(SWA_GROUP)
    kx, vx = kx_ref[0], vx_ref[0]
    if local:
        n = pl.program_id(2)
        blk = SWA_BLOCK
        n_rb = q_ref.shape[1] // blk
        band_k = jnp.concatenate([kp_ref[0], kc_ref[0], kn_ref[0]], axis=0)
        band_v = jnp.concatenate([vp_ref[0], vc_ref[0], vn_ref[0]], axis=0)
        col = lax.broadcasted_iota(jnp.int32, (1, band_ref.shape[1]), 1)
        band = band_ref[...]
        sink = jnp.concatenate([jnp.broadcast_to(sink_ref[0, g:g + 1, 0:1], (blk, 1)) for g in heads], axis=0)
        chains = []
        for i in range(n_rb):
            rows = slice(i * blk, (i + 1) * blk)
            q = jnp.concatenate([q_ref[0, rows, g * LANE:(g + 1) * LANE] for g in heads], axis=0)
            keys = jnp.concatenate([band_k[i * blk:(i + 3) * blk], kx], axis=0)
            vals = _with_one_hot(jnp.concatenate([band_v[i * blk:(i + 3) * blk], vx], axis=0))
            bias = band
            if i == 0:
                bias = bias + jnp.where((col < blk) & (n == 0), NEG_INF, 0.0)
            if i == n_rb - 1:
                last = (col >= 2 * blk) & (col < 3 * blk) & (n == pl.num_programs(2) - 1)
                bias = bias + jnp.where(last, NEG_INF, 0.0)
            chains.append(([(g, rows, slice(g * blk, (g + 1) * blk)) for g in heads], q, keys, vals, bias, sink))
    else:
        vals = _with_one_hot(vx)
        chains = [([(g, slice(None), slice(None))], q_ref[0, :, g * LANE:(g + 1) * LANE], kx, vals, None,
                   sink_ref[0, g:g + 1, 0:1]) for g in heads]
    scores = [lax.dot_general(q, keys, nt, preferred_element_type=F32) for _, q, keys, _, _, _ in chains]
    stage2 = []
    for (_, _, _, _, bias, sink), s in zip(chains, scores):
        if bias is not None:
            s = s + bias
        m = jnp.maximum(jnp.max(s, axis=1, keepdims=True), sink)
        stage2.append((jnp.exp2(s - m).astype(BF16), jnp.exp2(sink - m)))
    for (outs, _, _, vals, _, _), (p, p_sink) in zip(chains, stage2):
        o = jnp.dot(p, vals, preferred_element_type=F32)
        o = o[:, :SWA_HEAD_DIM] / (o[:, SWA_HEAD_DIM:SWA_HEAD_DIM + 1] + p_sink)
        for g, dst_rows, src_rows in outs:
            o_ref[0, dst_rows, g * LANE:(g + 1) * LANE] = o[src_rows].astype(BF16)


def _swa_band_table(l_ctx):
    blk = SWA_BLOCK
    r = np.arange(blk)[:, None]
    c = np.arange(3 * blk)[None, :] - blk
    band = np.where(np.abs(r - c) <= SWA_WINDOW, 0.0, NEG_INF)
    band = np.concatenate([band, np.zeros((blk, l_ctx))], axis=1)
    return jnp.asarray(np.tile(band, (SWA_GROUP, 1)), F32)


def _swa_call(pr, pa, sink2, *, layer, t_lat):
    n_batch, s_len, _ = pr.shape
    blk = SWA_BLOCK
    l_ctx = s_len - t_lat
    bq = SWA_QTILE if t_lat % SWA_QTILE == 0 else blk
    per = bq // blk
    n_blk = t_lat // blk
    ctx_blk = t_lat // l_ctx
    gw = SWA_GROUP * LANE
    kcol = lambda kh: PR_SK // LANE + kh
    vcol = lambda kh: PA_SV // LANE + kh
    prev = lambda n: jnp.maximum(n * per - 1, 0)
    nxt = lambda n: jnp.minimum((n + 1) * per, n_blk - 1)
    edge = lambda col, pos: pl.BlockSpec((1, blk, LANE), lambda b, kh, n: (b, pos(n), col(kh)))
    body = lambda col: pl.BlockSpec((1, bq, LANE), lambda b, kh, n: (b, n, col(kh)))
    ctx = lambda col: pl.BlockSpec((1, l_ctx, LANE), lambda b, kh, n: (b, ctx_blk, col(kh)))
    sink_spec = _layer_spec(layer, (1, SWA_GROUP, LANE), lambda b, kh, n: (kh, 0, 0))
    out_shape = jax.ShapeDtypeStruct((n_batch, s_len, SWA_HEADS * SWA_HEAD_DIM), BF16)
    band = _swa_band_table(l_ctx)
    o_lat = pl.pallas_call(
        functools.partial(_swa_kernel, local=True, aliased=False),
        grid=(n_batch, SWA_KV_HEADS, t_lat // bq),
        in_specs=[
            pl.BlockSpec((1, bq, gw), lambda b, kh, n: (b, n, PR_SQ // gw + kh)),
            edge(kcol, prev), body(kcol), edge(kcol, nxt),
            edge(vcol, prev), body(vcol), edge(vcol, nxt),
            ctx(kcol), ctx(vcol), sink_spec,
            pl.BlockSpec(band.shape, lambda b, kh, n: (0, 0)),
        ],
        out_specs=pl.BlockSpec((1, bq, gw), lambda b, kh, n: (b, n, kh)),
        out_shape=out_shape,
        compiler_params=_params(("parallel", "parallel", "parallel")),
        name="swa",
    )(pr, pr, pr, pr, pa, pa, pa, pr, pa, sink2, band)
    return pl.pallas_call(
        functools.partial(_swa_kernel, local=False, aliased=True),
        grid=(n_batch, SWA_KV_HEADS, 1),
        in_specs=[
            pl.BlockSpec(memory_space=pl.ANY),
            pl.BlockSpec((1, l_ctx, gw), lambda b, kh, n: (b, ctx_blk, PR_SQ // gw + kh)),
            ctx(kcol), ctx(vcol), sink_spec,
        ],
        out_specs=pl.BlockSpec((1, l_ctx, gw), lambda b, kh, n: (b, ctx_blk, kh)),
        out_shape=out_shape,
        input_output_aliases={0: 0},
        compiler_params=_params(("parallel", "parallel", "parallel")),
        name="swa_ctx",
    )(o_lat, pr, pr, pa, sink2)


def _mla_proj_kernel(cq_ref, ckv_ref, kpe_ref, qnw_ref, kvnw_ref, wq_ref, wkv_ref, cm_ref, sm_ref,
                     q_ref, k_ref, v_ref):
    cq = cq_ref[0].astype(F32)
    ms = jnp.sum(cq * cq, axis=-1, keepdims=True) * (1.0 / MLA_Q_RANK)
    cqn = (cq * lax.rsqrt(ms + EPS) * qnw_ref[...]).astype(BF16)
    q = jnp.dot(cqn, wq_ref[...], preferred_element_type=F32)
    cos_m, sin_m = cm_ref[...], sm_ref[...]
    qs = MLA_SCALE * LOG2E
    hw = 2 * LANE
    for h in range(MLA_HEADS):
        q_ref[0, :, h * hw:h * hw + LANE] = (q[:, h * hw:h * hw + LANE] * qs).astype(BF16)
        pe = q[:, h * hw + LANE:(h + 1) * hw]
        pe = (pe * cos_m + _swap_halves(pe, MLA_ROPE // 4) * sin_m) * qs
        q_ref[0, :, h * hw + LANE:(h + 1) * hw] = pe.astype(BF16)
    ckv = _rms(ckv_ref[0].astype(F32), kvnw_ref[...]).astype(BF16)
    kv = jnp.dot(ckv, wkv_ref[...], preferred_element_type=F32)
    half = MLA_HEADS * MLA_NOPE
    kpe = kpe_ref[0]
    one_hot = (lax.broadcasted_iota(jnp.int32, kpe.shape, 1) == 0).astype(BF16)
    for h in range(MLA_HEADS):
        k_ref[0, :, h * hw:h * hw + LANE] = kv[:, h * LANE:(h + 1) * LANE].astype(BF16)
        k_ref[0, :, h * hw + LANE:(h + 1) * hw] = kpe
        v_ref[0, :, h * hw:h * hw + LANE] = kv[:, half + h * LANE:half + (h + 1) * LANE].astype(BF16)
        v_ref[0, :, h * hw + LANE:(h + 1) * hw] = one_hot


def _mla_proj_call(pa, pr, qnw, kvnw, wq, wkv, cos_m, sin_m, *, layer):
    n_batch, s_len, _ = pa.shape
    tr = _row_tile(s_len)
    qw = MLA_HEADS * 2 * LANE
    kw = MLA_HEADS * MLA_NOPE
    const = lambda shape: _layer_spec(layer, shape, lambda b, i: (0,) * len(shape))
    wide = pl.BlockSpec((1, tr, qw), lambda b, i: (b, i, 0))
    return pl.pallas_call(
        _mla_proj_kernel,
        grid=(n_batch, s_len // tr),
        in_specs=[
            pl.BlockSpec((1, tr, MCQ_PAD), lambda b, i: (b, i, PA_MCQ // MCQ_PAD)),
            pl.BlockSpec((1, tr, MLA_KV_RANK), lambda b, i: (b, i, PA_MCKV // MLA_KV_RANK)),
            pl.BlockSpec((1, tr, LANE), lambda b, i: (b, i, PR_KPE // LANE)),
            const((1, MCQ_PAD)), const((1, MLA_KV_RANK)),
            const((MCQ_PAD, qw)), const((MLA_KV_RANK, 2 * kw)),
            pl.BlockSpec((tr, LANE), lambda b, i: (i, 0)),
            pl.BlockSpec((tr, LANE), lambda b, i: (i, 0)),
        ],
        out_specs=[wide, wide, wide],
        out_shape=[jax.ShapeDtypeStruct((n_batch, s_len, qw), BF16)] * 3,
        compiler_params=_params(("parallel", "parallel")),
        name="mla_proj",
    )(pa, pa, pr, qnw, kvnw, wq, wkv, cos_m, sin_m)


def _flash_kernel(*refs, aliased, sub, kv_sub):
    if aliased:
        refs = refs[1:]
    q_ref, k_ref, v_ref, o_ref, m_ref, acc_ref = refs
    j = pl.program_id(3)

    @pl.when(j == 0)
    def _():
        m_ref[...] = jnp.full_like(m_ref, NEG_INF)
        acc_ref[...] = jnp.zeros_like(acc_ref)

    bq, bk = q_ref.shape[1], k_ref.shape[1]
    sub = min(sub, bq)
    kv_sub = min(kv_sub, bk)
    chains = [slice(c * sub, (c + 1) * sub) for c in range(bq // sub)]
    for t in range(bk // kv_sub):
        k = k_ref[0, t * kv_sub:(t + 1) * kv_sub, :]
        v = v_ref[0, t * kv_sub:(t + 1) * kv_sub, :]
        scores = [lax.dot_general(q_ref[0, rows, :], k, (((1,), (1,)), ((), ())), preferred_element_type=F32)
                  for rows in chains]
        probs, alphas = [], []
        for rows, s in zip(chains, scores):
            m_prev = m_ref[rows, :]
            m_new = jnp.maximum(m_prev, jnp.max(s, axis=1, keepdims=True))
            alphas.append(jnp.exp2(m_prev - m_new))
            probs.append(jnp.exp2(s - m_new).astype(BF16))
            m_ref[rows, :] = m_new
        for rows, p, alpha in zip(chains, probs, alphas):
            acc_ref[rows, :] = alpha * acc_ref[rows, :] + jnp.dot(p, v, preferred_element_type=F32)

    @pl.when(j == pl.num_programs(3) - 1)
    def _():
        acc = acc_ref[...]
        o_ref[0] = (acc[:, :MLA_V] / acc[:, MLA_V:MLA_V + 1]).astype(BF16)


def _flash_call(q, k, v, prev_out, *, bq, bk, kv_sub, q_blk0, n_q, kv_blk0, n_kv, name):
    n_batch, s_len, _ = q.shape
    aliased = prev_out is not None
    hw = 2 * LANE
    in_specs = [
        pl.BlockSpec((1, bq, hw), lambda b, h, i, j: (b, q_blk0 + i, h)),
        pl.BlockSpec((1, bk, hw), lambda b, h, i, j: (b, kv_blk0 + j, h)),
        pl.BlockSpec((1, bk, hw), lambda b, h, i, j: (b, kv_blk0 + j, h)),
    ]
    args = [q, k, v]
    if aliased:
        in_specs = [pl.BlockSpec(memory_space=pl.ANY)] + in_specs
        args = [prev_out] + args
    return pl.pallas_call(
        functools.partial(_flash_kernel, aliased=aliased, sub=FLASH_SUB, kv_sub=kv_sub),
        grid=(n_batch, MLA_HEADS, n_q, n_kv),
        in_specs=in_specs,
        out_specs=pl.BlockSpec((1, bq, LANE), lambda b, h, i, j: (b, q_blk0 + i, h)),
        out_shape=jax.ShapeDtypeStruct((n_batch, s_len, MLA_HEADS * MLA_V), BF16),
        scratch_shapes=[pltpu.VMEM((bq, 1), F32), pltpu.VMEM((bq, hw), F32)],
        input_output_aliases={0: 0} if aliased else {},
        compiler_params=_params(("parallel", "parallel", "parallel", "arbitrary")),
        name=name,
    )(*args)


def _kv_block(s_len):
    for bk in (3328, 1280, 640, 256, 128):
        if s_len % bk == 0:
            return bk
    raise ValueError(f"unsupported stream length {s_len}")


def _merge_kernel(x_ref, mod_ref, g_ref, gr_ref, of_ref, ob_ref, osw_ref, om_ref, gw_ref, wb_ref, wo_ref,
                  o_ref, *, n_batch, t_lat, tr):
    b = pl.program_id(0)
    row0 = pl.program_id(1) * tr
    gw = gw_ref[...]
    halves = [slice(0, tr // 2), slice(tr // 2, tr)]
    gla = []
    for rows in halves:
        og = of_ref[0, rows, :].astype(F32) + ob_ref[0, rows, :].astype(F32)
        heads = [_rms(og[:, h * GLA_HEAD_V:(h + 1) * GLA_HEAD_V], gw) for h in range(GLA_HEADS)]
        gr = gr_ref[0, rows, :].astype(F32)
        gla.append((jnp.concatenate(heads, axis=1) * (gr * jax.nn.sigmoid(gr))).astype(BF16))
    projected = [[jnp.dot(ob, wb_ref[idx], preferred_element_type=F32)
                  for idx, ob in enumerate((o_gla, osw_ref[0, rows, :], om_ref[0, rows, :]))]
                 for rows, o_gla in zip(halves, gla)]
    merged = []
    for rows, terms in zip(halves, projected):
        total = None
        for idx, term in enumerate(terms):
            gate = jax.nn.sigmoid(g_ref[0, rows, idx * D_MODEL:(idx + 1) * D_MODEL].astype(F32))
            total = gate * term if total is None else total + gate * term
        merged.append(total.astype(BF16))
    outs = [jnp.dot(m, wo_ref[...], preferred_element_type=F32) for m in merged]
    g_m = _mod_rows(mod_ref, b, n_batch, row0, tr, t_lat, 2)
    for rows, y in zip(halves, outs):
        o_ref[0, rows, :] = x_ref[0, rows, :] + g_m[rows] * y


def _merge_call(xa, mod, pa, o_f, o_b, o_swa, o_mla, gw, wb, wo, *, layer, t_lat):
    n_batch, s_len, _ = xa.shape
    tr = _row_tile(s_len)
    row = lambda width, col: pl.BlockSpec((1, tr, width), lambda b, i: (b, i, col))
    resident = dict(pipeline_mode=pl.Buffered(1))
    return pl.pallas_call(
        functools.partial(_merge_kernel, n_batch=n_batch, t_lat=t_lat, tr=tr),
        grid=(n_batch, s_len // tr),
        in_specs=[
            row(D_MODEL, 0),
            _layer_spec(layer, mod.shape[1:], lambda b, i: (0, 0)),
            row(3 * D_MODEL, PA_GATES // (3 * D_MODEL)),
            row(D_MODEL, PA_GR // D_MODEL),
            row(D_MODEL, 0), row(D_MODEL, 0), row(D_MODEL, 0), row(D_MODEL, 0),
            _layer_spec(layer, (1, GLA_HEAD_V), lambda b, i: (0, 0)),
            _layer_spec(layer, (3, D_MODEL, D_MODEL), lambda b, i: (0, 0, 0), **resident),
            _layer_spec(layer, (D_MODEL, D_MODEL), lambda b, i: (0, 0), **resident),
        ],
        out_specs=row(D_MODEL, 0),
        out_shape=jax.ShapeDtypeStruct(xa.shape, F32),
        compiler_params=_params(("parallel", "parallel")),
        name="merge",
    )(xa, mod, pa, pa, o_f, o_b, o_swa, o_mla, gw, wb, wo)


def _mlp_kernel(x_ref, mod_ref, nw_ref, w1_ref, w2_ref, fw_ref, o_ref, *, n_batch, t_lat, tr, final):
    b = pl.program_id(0)
    row0 = pl.program_id(1) * tr
    shift = _mod_rows(mod_ref, b, n_batch, row0, tr, t_lat, 3)
    scale = _mod_rows(mod_ref, b, n_batch, row0, tr, t_lat, 4)
    g_f = _mod_rows(mod_ref, b, n_batch, row0, tr, t_lat, 5)
    halves = [slice(0, tr // 2), slice(tr // 2, tr)]
    hs = [(_rms(x_ref[0, rows, :], nw_ref[...]) * (1.0 + scale[rows]) + shift[rows]).astype(BF16)
          for rows in halves]
    ups = [jnp.maximum(jnp.dot(h, w1_ref[...], preferred_element_type=F32), 0.0) for h in hs]
    downs = [jnp.dot((u * u).astype(BF16), w2_ref[...], preferred_element_type=F32) for u in ups]
    for rows, d in zip(halves, downs):
        y = x_ref[0, rows, :] + g_f[rows] * d
        if final:
            y = _rms(y, fw_ref[...])
        o_ref[0, rows, :] = y


def _mlp_call(xa, mod, nw, w1, w2, fw, *, layer, t_lat, final):
    n_batch, s_len, _ = xa.shape
    rows = t_lat if final else s_len
    tr = _row_tile(rows)
    resident = dict(pipeline_mode=pl.Buffered(1))
    return pl.pallas_call(
        functools.partial(_mlp_kernel, n_batch=n_batch, t_lat=t_lat, tr=tr, final=final),
        grid=(n_batch, rows // tr),
        in_specs=[
            pl.BlockSpec((1, tr, D_MODEL), lambda b, i: (b, i, 0)),
            _layer_spec(layer, mod.shape[1:], lambda b, i: (0, 0)),
            _layer_spec(layer, (1, D_MODEL), lambda b, i: (0, 0)),
            _layer_spec(layer, (D_MODEL, FF_DIM), lambda b, i: (0, 0), **resident),
            _layer_spec(layer, (FF_DIM, D_MODEL), lambda b, i: (0, 0), **resident),
            pl.BlockSpec((1, D_MODEL), lambda b, i: (0, 0)),
        ],
        out_specs=pl.BlockSpec((1, tr, D_MODEL), lambda b, i: (b, i, 0)),
        out_shape=jax.ShapeDtypeStruct((n_batch, rows, D_MODEL), F32),
        compiler_params=_params(("parallel", "parallel")),
        name="mlp",
    )(xa, mod, nw, w1, w2, fw)


def _rope_tables(t_lat, l_ctx, dim):
    f32 = np.float32
    pos = np.arange(t_lat)
    row, col = pos // GRID_W, pos % GRID_W
    d_axis = dim // 2
    inv = (f32(ROPE_BASE) ** (-np.arange(0, d_axis, 2, dtype=f32) / f32(d_axis))).astype(f32)
    ang_r = row.astype(f32)[:, None] * inv
    ang_c = col.astype(f32)[:, None] * inv
    cos = np.concatenate([np.cos(ang_r)] * 2 + [np.cos(ang_c)] * 2, axis=1)
    sin = np.concatenate([-np.sin(ang_r), np.sin(ang_r), -np.sin(ang_c), np.sin(ang_c)], axis=1)
    if dim < LANE:
        cos = np.concatenate([cos, np.ones((t_lat, LANE - dim), f32)], axis=1)
        sin = np.concatenate([sin, np.zeros((t_lat, LANE - dim), f32)], axis=1)
    cos = np.concatenate([cos, np.ones((l_ctx, LANE), f32)], axis=0)
    sin = np.concatenate([sin, np.zeros((l_ctx, LANE), f32)], axis=0)
    return jnp.asarray(cos, F32), jnp.asarray(sin, F32)


def _split_w_in(w):
    sizes = (512, 512, 1024, 1024, GLA_RANK, GLA_RANK, 1024, 256, 256, MLA_Q_RANK, MLA_KV_RANK, MLA_ROPE,
             3 * D_MODEL)
    offs = np.cumsum((0,) + sizes)
    return [w[..., offs[i]:offs[i + 1]] for i in range(len(sizes))]


def _layer_weights(w_in, gla_a_w, gla_a_b, mla_q_norm_w, mla_w_uq, mla_kv_norm_w, mla_w_ukv, swa_sink):
    depth = w_in.shape[0]
    gq, gk, gv, gr, gzf, gzb, sq, sk, sv, mcq, mckv, mkr, gates = _split_w_in(w_in)
    zpad = lambda n: jnp.zeros((depth, D_MODEL, n), F32)
    w_plain = jnp.concatenate(
        [gates, gr, gv, gq, gk, mcq, zpad(MCQ_PAD - MLA_Q_RANK), sv, mckv, gzf, gzb, zpad(LANE - 2 * GLA_RANK)],
        axis=-1).astype(BF16)
    w_rope = jnp.concatenate([sq, sk, mkr, zpad(LANE - MLA_ROPE)], axis=-1).astype(BF16)
    aw = jnp.zeros((depth, 2, LANE, GLA_HEADS * GLA_HEAD_K), F32)
    aw = aw.at[:, 0, :GLA_RANK].set(gla_a_w[:, 0]).at[:, 1, GLA_RANK:2 * GLA_RANK].set(gla_a_w[:, 1])
    aw = aw.astype(BF16)
    ab = gla_a_b.reshape(depth, 2, 1, GLA_HEADS * GLA_HEAD_K)
    wq = mla_w_uq.reshape(depth, MLA_Q_RANK, MLA_HEADS, MLA_NOPE + MLA_ROPE)
    wq = jnp.pad(wq, ((0, 0), (0, MCQ_PAD - MLA_Q_RANK), (0, 0), (0, 2 * LANE - MLA_NOPE - MLA_ROPE)))
    wq = wq.reshape(depth, MCQ_PAD, MLA_HEADS * 2 * LANE).astype(BF16)
    qnw = jnp.pad(mla_q_norm_w, ((0, 0), (0, MCQ_PAD - MLA_Q_RANK))).reshape(depth, 1, MCQ_PAD)
    wkv = mla_w_ukv.reshape(depth, MLA_KV_RANK, MLA_HEADS, MLA_NOPE + MLA_V)
    wkv = jnp.concatenate([wkv[..., :MLA_NOPE].reshape(depth, MLA_KV_RANK, -1),
                           wkv[..., MLA_NOPE:].reshape(depth, MLA_KV_RANK, -1)], axis=-1).astype(BF16)
    kvnw = mla_kv_norm_w.reshape(depth, 1, MLA_KV_RANK)
    sink2 = jnp.broadcast_to((swa_sink * LOG2E).reshape(depth, SWA_KV_HEADS, SWA_GROUP, 1),
                             (depth, SWA_KV_HEADS, SWA_GROUP, LANE))
    return w_plain, w_rope, aw, ab, wq, qnw, wkv, kvnw, sink2


def kernel(x, c, ctx, c_ctx, ada_w, ada_b, norm_mix_w, w_in, gla_a_w, gla_a_b, gla_norm_w, swa_sink,
           mla_q_norm_w, mla_w_uq, mla_kv_norm_w, mla_w_ukv, w_branch, w_out, norm_mlp_w, mlp_w1, mlp_w2,
           final_norm_w):
    n_batch, t_lat, d_model = x.shape
    l_ctx = ctx.shape[1]
    depth = ada_w.shape[0]
    s_len = t_lat + l_ctx
    assert d_model == D_MODEL and n_batch + 1 <= MOD_ROWS
    assert t_lat % GLA_BLOCK == 0 and l_ctx % GLA_BLOCK == 0 and t_lat % l_ctx == 0 and t_lat % GRID_W == 0

    xa = jnp.concatenate([x, ctx], axis=1)
    cin = jnp.zeros((MOD_ROWS, D_MODEL), F32).at[:n_batch].set(c).at[n_batch].set(c_ctx)
    mod_all = _ada_call(cin, ada_w, ada_b)

    cos_s, sin_s = _rope_tables(t_lat, l_ctx, SWA_HEAD_DIM)
    cos_m, sin_m = _rope_tables(t_lat, l_ctx, MLA_ROPE)
    tabs = (cos_s, sin_s, cos_m, sin_m)

    bq = min(1024, t_lat)
    bk = _kv_block(s_len)

    (w_plain, w_rope, aw, ab, wq, qnw, wkv, kvnw, sink2) = _layer_weights(
        w_in, gla_a_w, gla_a_b, mla_q_norm_w, mla_w_uq, mla_kv_norm_w, mla_w_ukv, swa_sink)
    row_vec = lambda v: v.reshape(depth, 1, -1)
    nw_mix, nw_mlp, gw = row_vec(norm_mix_w), row_vec(norm_mlp_w), row_vec(gla_norm_w)
    wb, wo, w1, w2 = (w.astype(BF16) for w in (w_branch, w_out, mlp_w1, mlp_w2))
    mod = mod_all

    for layer in range(depth):
        pa = _proj_call(xa, mod, nw_mix, w_plain, None, layer=layer, t_lat=t_lat, bn=PA_BN, rope=False,
                        name="proj_plain")
        pr = _proj_call(xa, mod, nw_mix, w_rope, tabs, layer=layer, t_lat=t_lat, bn=PR_WIDTH, rope=True,
                        name="proj_rope")

        o_f, o_b = _gla_call(pa, aw, ab, layer=layer, t_lat=t_lat)
        o_swa = _swa_call(pr, pa, sink2, layer=layer, t_lat=t_lat)

        q, k, v = _mla_proj_call(pa, pr, qnw, kvnw, wq, wkv, cos_m, sin_m, layer=layer)
        o_mla = _flash_call(q, k, v, None, bq=bq, bk=s_len, kv_sub=bk, q_blk0=0, n_q=t_lat // bq, kv_blk0=0,
                            n_kv=1, name="mla_flash")
        o_mla = _flash_call(q, k, v, o_mla, bq=l_ctx, bk=l_ctx, kv_sub=l_ctx, q_blk0=t_lat // l_ctx, n_q=1,
                            kv_blk0=t_lat // l_ctx, n_kv=1, name="mla_flash_ctx")

        xa = _merge_call(xa, mod, pa, o_f, o_b, o_swa, o_mla, gw, wb, wo, layer=layer, t_lat=t_lat)
        xa = _mlp_call(xa, mod, nw_mlp, w1, w2, final_norm_w.reshape(1, -1), layer=layer, t_lat=t_lat,
                       final=layer == depth - 1)
    return xa
```
